```python
import math
import jax, jax.numpy as jnp
from jax import lax
import numpy as np

D_MODEL = 1024
BATCH = 8
SEQ = 2048
DEPTH = 2

CHUNK = 64
N_MEM = 256
EPS = 1e-6
N_SUBLAYER_NORMS = 6

GDN_HEADS = 4
GDN_DK = 128
GDN_DV = 128
GDN_QK = GDN_HEADS * GDN_DK
GDN_VW = GDN_HEADS * GDN_DV
GDN_QKV = 2 * GDN_QK + GDN_VW
CONV_W = 4

MLA_HEADS = 4
MLA_Q_RANK = 256
MLA_KV_RANK = 256
MLA_NOPE = 128
MLA_ROPE = 64
MLA_V = 128
MLA_VW = MLA_HEADS * MLA_V
MLA_SCALE = (MLA_NOPE + MLA_ROPE) ** -0.5
ROPE_BASE = 10000.0
Q_BLOCK = 128

E_SECTIONS = (GDN_QKV, GDN_VW, GDN_HEADS, GDN_HEADS, MLA_Q_RANK, MLA_KV_RANK, MLA_ROPE)
E_IN = GDN_QKV + GDN_VW + 2 * GDN_HEADS + MLA_Q_RANK + MLA_KV_RANK + MLA_ROPE
E_MIX = GDN_VW + MLA_VW

LRU_WIDTH = D_MODEL
LRU_BLOCKS = 4
LRU_BW = LRU_WIDTH // LRU_BLOCKS
LRU_C = 8.0

XA_HEADS = 4
XA_HD = D_MODEL // XA_HEADS

D_FF = ((8 * D_MODEL + 3 * 256 - 1) // (3 * 256)) * 256

N_EVEN = (DEPTH + 1) // 2
N_ODD = DEPTH // 2

kernel_name = "hybrid_gdn_mla_rglru_streaming_block"


def rmsnorm(x, g):
    xf = x.astype(jnp.float32)
    y = xf * lax.rsqrt(jnp.mean(xf * xf, axis=-1, keepdims=True) + EPS)
    return (y * g.astype(jnp.float32)).astype(x.dtype)


def l2norm(x):
    return x * lax.rsqrt(jnp.sum(x * x, axis=-1, keepdims=True) + EPS)


def causal_conv(x, w):
    c = x.shape[-1]
    return lax.conv_general_dilated(
        x, w[:, None, :].astype(x.dtype), window_strides=(1,),
        padding=[(w.shape[0] - 1, 0)], dimension_numbers=("NWC", "WIO", "NWC"),
        feature_group_count=c)


def rope_tables(positions):
    inv_freq = ROPE_BASE ** (-jnp.arange(0, MLA_ROPE, 2, dtype=jnp.float32) / MLA_ROPE)
    ang = positions.astype(jnp.float32)[..., None] * inv_freq
    return jnp.cos(ang), jnp.sin(ang)


def apply_rope(x, cos, sin):
    x1, x2 = jnp.split(x, 2, axis=-1)
    cos = cos.astype(x.dtype)
    sin = sin.astype(x.dtype)
    return jnp.concatenate([x1 * cos - x2 * sin, x1 * sin + x2 * cos], axis=-1)


def gated_delta_rule(q, k, v, g, beta):
    b_, t_, h_, dk = q.shape
    dv = v.shape[-1]
    n = t_ // CHUNK

    def to_chunks(a):
        a = jnp.moveaxis(a, 2, 1)
        return a.reshape(b_, h_, n, CHUNK, *a.shape[3:])

    q, k, v, g, beta = map(to_chunks, (q, k, v, g, beta))
    gc = jnp.cumsum(g, axis=-1)
    idx = jnp.arange(CHUNK)
    causal = idx[:, None] >= idx[None, :]
    strict = idx[:, None] > idx[None, :]
    decay = jnp.exp(jnp.where(causal, gc[..., :, None] - gc[..., None, :], -jnp.inf))
    kb = k * beta[..., None]
    vb = v * beta[..., None]
    lower = jnp.where(strict, jnp.einsum("bhncd,bhnsd->bhncs", kb, k) * decay, 0.0)
    rhs = jnp.concatenate([vb, kb * jnp.exp(gc)[..., None]], axis=-1)
    sol = lax.linalg.triangular_solve(lower, rhs, left_side=True, lower=True,
                                      unit_diagonal=True)
    u, w = sol[..., :dv], sol[..., dv:]
    a_qk = jnp.where(causal, jnp.einsum("bhncd,bhnsd->bhncs", q, k) * decay, 0.0)

    def step(state, xs):
        q_i, k_i, u_i, w_i, gc_i, a_i = xs
        v_new = u_i - jnp.einsum("bhck,bhkv->bhcv", w_i, state)
        o_i = (jnp.einsum("bhck,bhkv->bhcv", q_i * jnp.exp(gc_i)[..., None], state)
               + jnp.einsum("bhcs,bhsv->bhcv", a_i, v_new))
        g_last = gc_i[..., -1]
        k_dec = k_i * jnp.exp(g_last[..., None] - gc_i)[..., None]
        state = state * jnp.exp(g_last)[..., None, None] + jnp.einsum(
            "bhck,bhcv->bhkv", k_dec, v_new)
        return state, o_i

    xs = tuple(jnp.moveaxis(a, 2, 0) for a in (q, k, u, w, gc, a_qk))
    s0 = jnp.zeros((b_, h_, dk, dv), jnp.float32)
    _, o = lax.scan(step, s0, xs)
    o = jnp.moveaxis(o, 0, 2).reshape(b_, h_, t_, dv)
    return jnp.moveaxis(o, 1, 2)


def mla_attention(q_nope, q_pe, k_nope, k_pe, v):
    b_, t_, h_, _ = q_nope.shape
    nqb = t_ // Q_BLOCK
    key_chunk = jnp.arange(t_) // CHUNK

    def block(args):
        i, qn, qp = args
        s = (jnp.einsum("bqhd,bkhd->bhqk", qn, k_nope)
             + jnp.einsum("bqhd,bkd->bhqk", qp, k_pe))
        s = s.astype(jnp.float32) * MLA_SCALE
        q_chunk = (i * Q_BLOCK + jnp.arange(Q_BLOCK)) // CHUNK
        s = jnp.where(key_chunk[None, :] <= q_chunk[:, None], s, -jnp.inf)
        p = jax.nn.softmax(s, axis=-1).astype(v.dtype)
        return jnp.einsum("bhqk,bkhd->bqhd", p, v)

    def to_blocks(a):
        return jnp.moveaxis(a.reshape(b_, nqb, Q_BLOCK, *a.shape[2:]), 1, 0)

    out = lax.map(block, (jnp.arange(nqb), to_blocks(q_nope), to_blocks(q_pe)))
    return jnp.moveaxis(out, 0, 1).reshape(b_, t_, h_, MLA_V)


def gdn_mla_mixer(h, cos, sin, w_in, conv_w, a_log, dt_bias, o_norm,
                  q_norm, kv_norm, w_uq, w_ukv, w_out):
    b_, t_, _ = h.shape
    f32 = jnp.float32
    cuts = np.cumsum(E_SECTIONS)[:-1].tolist()
    qkv, z, a, b, c_q, c_kv, k_rope = jnp.split(h @ w_in, cuts, axis=-1)

    qkv = jax.nn.silu(causal_conv(qkv, conv_w)).astype(f32)
    q, k, v = jnp.split(qkv, [GDN_QK, 2 * GDN_QK], axis=-1)
    q = l2norm(q.reshape(b_, t_, GDN_HEADS, GDN_DK)) * (GDN_DK ** -0.5)
    k = l2norm(k.reshape(b_, t_, GDN_HEADS, GDN_DK))
    v = v.reshape(b_, t_, GDN_HEADS, GDN_DV)
    beta = jax.nn.sigmoid(b.astype(f32))
    g = -jnp.exp(a_log.astype(f32)) * jax.nn.softplus(a.astype(f32) + dt_bias.astype(f32))
    o = gated_delta_rule(q, k, v, g, beta)
    o = rmsnorm(o, o_norm) * jax.nn.silu(z.reshape(b_, t_, GDN_HEADS, GDN_DV).astype(f32))
    out_a = o.reshape(b_, t_, GDN_VW).astype(h.dtype)

    qf = (rmsnorm(c_q, q_norm) @ w_uq).reshape(b_, t_, MLA_HEADS, MLA_NOPE + MLA_ROPE)
    q_nope, q_pe = qf[..., :MLA_NOPE], qf[..., MLA_NOPE:]
    kvf = (rmsnorm(c_kv, kv_norm) @ w_ukv).reshape(b_, t_, MLA_HEADS, MLA_NOPE + MLA_V)
    k_nope, v_b = kvf[..., :MLA_NOPE], kvf[..., MLA_NOPE:]
    q_pe = apply_rope(q_pe, cos[:, :, None, :], sin[:, :, None, :])
    k_pe = apply_rope(k_rope, cos, sin)
    out_b = mla_attention(q_nope, q_pe, k_nope, k_pe, v_b).reshape(b_, t_, MLA_VW)

    return jnp.concatenate([out_a, out_b], axis=-1) @ w_out


def _lru_combine(c1, c2):
    a1, b1 = c1
    a2, b2 = c2
    return a1 * a2, a2 * b1 + b2


def rglru_mixer(h, w_in, conv_w, conv_b, gate_a_w, gate_a_b, gate_x_w, gate_x_b,
                a_param, w_out):
    b_, t_, _ = h.shape
    f32 = jnp.float32
    xb, yb = jnp.split(h @ w_in, 2, axis=-1)
    gate = jax.nn.gelu(yb)
    xb = causal_conv(xb, conv_w) + conv_b
    xr = xb.reshape(b_, t_, LRU_BLOCKS, LRU_BW)
    r = jax.nn.sigmoid((jnp.einsum("btnd,nde->btne", xr, gate_a_w)
                        .reshape(b_, t_, LRU_WIDTH) + gate_a_b).astype(f32))
    i = jax.nn.sigmoid((jnp.einsum("btnd,nde->btne", xr, gate_x_w)
                        .reshape(b_, t_, LRU_WIDTH) + gate_x_b).astype(f32))
    log_a = -LRU_C * r * jax.nn.softplus(-a_param.astype(f32))
    a = jnp.exp(log_a)
    u = jnp.sqrt(-jnp.expm1(2.0 * log_a)) * (i * xb.astype(f32))
    _, hs = lax.associative_scan(_lru_combine, (a, u), axis=1)
    return (hs.astype(h.dtype) * gate) @ w_out


def memory_cross_attention(h, mem_n, wq, wkv, wo):
    b_, t_, _ = h.shape
    q = (h @ wq).reshape(b_, t_, XA_HEADS, XA_HD)
    k, v = jnp.split(mem_n @ wkv, 2, axis=-1)
    k = k.reshape(b_, N_MEM, XA_HEADS, XA_HD)
    v = v.reshape(b_, N_MEM, XA_HEADS, XA_HD)
    s = jnp.einsum("bthd,bmhd->bhtm", q, k).astype(jnp.float32) * (XA_HD ** -0.5)
    p = jax.nn.softmax(s, axis=-1).astype(v.dtype)
    o = jnp.einsum("bhtm,bmhd->bthd", p, v).reshape(b_, t_, D_MODEL)
    return o @ wo


def swiglu(h, w_in, w_out):
    gate, up = jnp.split(h @ w_in, 2, axis=-1)
    return (jax.nn.silu(gate) * up) @ w_out


def setup_inputs(seed: int = 0) -> dict:
    key = jax.random.key(seed)
    ks = list(jax.random.split(key, 40))
    f32 = jnp.float32

    def nrm(shape, fan_in):
        return jax.random.normal(ks.pop(), shape, f32) * (fan_in ** -0.5)

    def gain(shape):
        return 1.0 + 0.05 * jax.random.normal(ks.pop(), shape, f32)

    def small(shape):
        return 0.01 * jax.random.normal(ks.pop(), shape, f32)

    x = jax.random.normal(ks.pop(), (BATCH, SEQ, D_MODEL), f32)
    mem = jax.random.normal(ks.pop(), (BATCH, N_MEM, D_MODEL), f32)
    offset = jax.random.randint(ks.pop(), (BATCH, 1), 0, 64, dtype=jnp.int32) * CHUNK
    positions = (offset + jnp.arange(SEQ, dtype=jnp.int32)[None, :]).astype(jnp.int32)

    e_a_log = jnp.log(jax.random.uniform(ks.pop(), (N_EVEN, GDN_HEADS), f32, 1.0, 16.0))
    dt = jnp.exp(jax.random.uniform(ks.pop(), (N_EVEN, GDN_HEADS), f32,
                                    math.log(1e-3), math.log(1e-1)))
    e_dt_bias = dt + jnp.log(-jnp.expm1(-dt))
    a0 = jax.random.uniform(ks.pop(), (N_ODD, LRU_WIDTH), f32, 0.9, 0.999)
    o_a_param = jnp.log(a0) - jnp.log1p(-a0)

    return {
        "x": x,
        "mem": mem,
        "positions": positions,
        "norm_gains": gain((DEPTH, N_SUBLAYER_NORMS, D_MODEL)),
        "mem_norm": gain((D_MODEL,)),
        "e_w_in": nrm((N_EVEN, D_MODEL, E_IN), D_MODEL),
        "e_conv_w": nrm((N_EVEN, CONV_W, GDN_QKV), CONV_W),
        "e_a_log": e_a_log,
        "e_dt_bias": e_dt_bias,
        "e_o_norm": gain((N_EVEN, GDN_DV)),
        "e_q_norm": gain((N_EVEN, MLA_Q_RANK)),
        "e_kv_norm": gain((N_EVEN, MLA_KV_RANK)),
        "e_w_uq": nrm((N_EVEN, MLA_Q_RANK, MLA_HEADS * (MLA_NOPE + MLA_ROPE)), MLA_Q_RANK),
        "e_w_ukv": nrm((N_EVEN, MLA_KV_RANK, MLA_HEADS * (MLA_NOPE + MLA_V)), MLA_KV_RANK),
        "e_w_out": nrm((N_EVEN, E_MIX, D_MODEL), E_MIX),
        "o_w_in": nrm((N_ODD, D_MODEL, 2 * LRU_WIDTH), D_MODEL),
        "o_conv_w": nrm((N_ODD, CONV_W, LRU_WIDTH), CONV_W),
        "o_conv_b": small((N_ODD, LRU_WIDTH)),
        "o_gate_a_w": nrm((N_ODD, LRU_BLOCKS, LRU_BW, LRU_BW), LRU_BW),
        "o_gate_a_b": small((N_ODD, LRU_WIDTH)),
        "o_gate_x_w": nrm((N_ODD, LRU_BLOCKS, LRU_BW, LRU_BW), LRU_BW),
        "o_gate_x_b": small((N_ODD, LRU_WIDTH)),
        "o_a_param": o_a_param,
        "o_w_out": nrm((N_ODD, LRU_WIDTH, D_MODEL), LRU_WIDTH),
        "xa_wq": nrm((DEPTH, D_MODEL, D_MODEL), D_MODEL),
        "xa_wkv": nrm((DEPTH, D_MODEL, 2 * D_MODEL), D_MODEL),
        "xa_wo": nrm((DEPTH, D_MODEL, D_MODEL), D_MODEL),
        "ffn_w_in": nrm((DEPTH, D_MODEL, 2 * D_FF), D_MODEL),
        "ffn_w_out": nrm((DEPTH, D_FF, D_MODEL), D_FF),
    }


def reference(x, mem, positions, norm_gains, mem_norm,
              e_w_in, e_conv_w, e_a_log, e_dt_bias, e_o_norm, e_q_norm, e_kv_norm,
              e_w_uq, e_w_ukv, e_w_out,
              o_w_in, o_conv_w, o_conv_b, o_gate_a_w, o_gate_a_b, o_gate_x_w, o_gate_x_b,
              o_a_param, o_w_out,
              xa_wq, xa_wkv, xa_wo, ffn_w_in, ffn_w_out):
    cos, sin = rope_tables(positions)
    mem_n = rmsnorm(mem, mem_norm)
    for layer in range(DEPTH):
        g = norm_gains[layer]
        h = rmsnorm(x, g[0])
        if layer % 2 == 0:
            e = layer // 2
            y = gdn_mla_mixer(h, cos, sin, e_w_in[e], e_conv_w[e], e_a_log[e], e_dt_bias[e],
                              e_o_norm[e], e_q_norm[e], e_kv_norm[e], e_w_uq[e], e_w_ukv[e],
                              e_w_out[e])
        else:
            o = layer // 2
            y = rglru_mixer(h, o_w_in[o], o_conv_w[o], o_conv_b[o], o_gate_a_w[o],
                            o_gate_a_b[o], o_gate_x_w[o], o_gate_x_b[o], o_a_param[o],
                            o_w_out[o])
        x = x + rmsnorm(y, g[1])
        h = rmsnorm(x, g[2])
        x = x + rmsnorm(memory_cross_attention(h, mem_n, xa_wq[layer], xa_wkv[layer],
                                               xa_wo[layer]), g[3])
        h = rmsnorm(x, g[4])
        x = x + rmsnorm(swiglu(h, ffn_w_in[layer], ffn_w_out[layer]), g[5])
    return x
```

```python
import functools
import math

import jax
import jax.numpy as jnp
from jax import lax
from jax.experimental import pallas as pl
from jax.experimental.pallas import tpu as pltpu

F32 = jnp.float32
BF16 = jnp.bfloat16

EPS = 1e-6
CHUNK = 64
GDN_HEADS = 4
GDN_DK = 128
MLA_HEADS = 4
MLA_NOPE = 128
MLA_ROPE = 64
MLA_SCALE = (MLA_NOPE + MLA_ROPE) ** -0.5
ROPE_BASE = 10000.0
LRU_BLOCKS = 4
LRU_C = 8.0
XA_HEADS = 4
CONV_W = 4

LANES = 128
CARRY_ROWS = 8
V7X_VMEM_LIMIT = 56 * 1024 * 1024

NT_DIMS = (((1,), (1,)), ((), ()))
TN_DIMS = (((0,), (0,)), ((), ()))


def _cparams(n_axes):
    return pltpu.CompilerParams(
        dimension_semantics=("arbitrary",) * n_axes,
        vmem_limit_bytes=V7X_VMEM_LIMIT)


def _dot(a, b):
    return jnp.dot(a, b, preferred_element_type=F32)


def _dot_nt(a, b):
    return lax.dot_general(a, b, NT_DIMS, preferred_element_type=F32)


def _rms(xf, g):
    ms = jnp.mean(xf * xf, axis=-1, keepdims=True)
    return xf * lax.rsqrt(ms + EPS) * g


def _sigmoid(x):
    return 1.0 / (1.0 + jnp.exp(-x))


def _softplus(x):
    return jnp.maximum(x, 0.0) + jnp.log1p(jnp.exp(-jnp.abs(x)))


def _const_spec(shape):
    nd = len(shape)
    return pl.BlockSpec(shape, lambda *_: (0,) * nd)


def _rope_kernel(pos_ref, inv_ref, cos_ref, sin_ref):
    ang = pos_ref[...].astype(F32) * inv_ref[...]
    cos_ref[...] = jnp.cos(ang)
    sin_ref[...] = jnp.sin(ang)


def _rope_tables(positions):
    n = positions.size
    half = MLA_ROPE // 2
    per_row = LANES // half
    inv_freq = ROPE_BASE ** (-jnp.arange(0, MLA_ROPE, 2, dtype=F32) / MLA_ROPE)
    inv_row = jnp.tile(inv_freq, per_row).reshape(1, LANES)
    pos_rep = jnp.repeat(positions.reshape(n // per_row, per_row), half, axis=1)
    rows = n // per_row
    tr = min(rows, 1024)
    spec = pl.BlockSpec((tr, LANES), lambda i: (i, 0))
    cos_p, sin_p = pl.pallas_call(
        _rope_kernel,
        grid=(rows // tr,),
        in_specs=[spec, _const_spec((1, LANES))],
        out_specs=[spec, spec],
        out_shape=[jax.ShapeDtypeStruct((rows, LANES), F32)] * 2,
        compiler_params=_cparams(1),
        name="rope_tables",
    )(pos_rep, inv_row)
    cos_t = jnp.tile(cos_p.reshape(n, half), (1, per_row))
    sin_t = jnp.tile(sin_p.reshape(n, half), (1, per_row))
    return cos_t, sin_t


def _memkv_kernel(mem_ref, g_ref, w_ref, o_ref):
    mn = _rms(mem_ref[...], g_ref[...]).astype(BF16)
    o_ref[0] = _dot(mn, w_ref[0]).astype(BF16)


def _mem_kv(mem2d, mem_norm, wkv_bf16):
    depth, d, d2 = wkv_bf16.shape
    rows = mem2d.shape[0]
    tr = min(rows, 512)
    return pl.pallas_call(
        _memkv_kernel,
        grid=(depth, rows // tr),
        in_specs=[pl.BlockSpec((tr, d), lambda l, i: (i, 0)),
                  _const_spec((1, d)),
                  pl.BlockSpec((1, d, d2), lambda l, i: (l, 0, 0))],
        out_specs=pl.BlockSpec((1, tr, d2), lambda l, i: (l, i, 0)),
        out_shape=jax.ShapeDtypeStruct((depth, rows, d2), BF16),
        compiler_params=_cparams(2),
        name="mem_kv",
    )(mem2d, mem_norm.reshape(1, d), wkv_bf16)


def _rope_tile(x, cos, sin, lane):
    half = MLA_ROPE // 2
    up = pltpu.roll(x, LANES - half, axis=1)
    dn = pltpu.roll(x, half, axis=1)
    rot = jnp.where(lane < half, -up, dn)
    return jnp.where(lane < MLA_ROPE, x * cos + rot * sin, 0.0)


def _even_in_kernel(x_ref, g_ref, win_ref, cw_ref, alog_ref, dtb_ref, qn_ref, kvn_ref,
                    wuq_ref, wukv_ref, cos_ref, sin_ref,
                    q_out, k_out, v_out, z_out, gb_out, qf_out, kf_out, vb_out,
                    xpad):
    tm = x_ref.shape[0]
    qk_w = GDN_HEADS * GDN_DK
    qkv_w = 3 * qk_w

    @pl.when(pl.program_id(1) == 0)
    def _():
        xpad[0:CARRY_ROWS, :] = jnp.zeros((CARRY_ROWS, qkv_w), F32)

    h = _rms(x_ref[...], g_ref[...]).astype(BF16)
    proj = _dot(h, win_ref[...])

    qkv = proj[:, :qkv_w]
    xpad[CARRY_ROWS:CARRY_ROWS + tm, :] = qkv
    acc = cw_ref[CONV_W - 1:CONV_W, :] * qkv
    for j in range(CONV_W - 1):
        acc = acc + cw_ref[j:j + 1, :] * xpad[pl.ds(CARRY_ROWS - (CONV_W - 1) + j, tm), :]
    xpad[0:CARRY_ROWS, :] = qkv[tm - CARRY_ROWS:, :]
    act = acc * _sigmoid(acc)
    for hh in range(GDN_HEADS):
        sl = slice(hh * GDN_DK, (hh + 1) * GDN_DK)
        qh = act[:, sl]
        kh = act[:, qk_w + hh * GDN_DK: qk_w + (hh + 1) * GDN_DK]
        qn = qh * lax.rsqrt(jnp.sum(qh * qh, axis=-1, keepdims=True) + EPS) * (GDN_DK ** -0.5)
        kn = kh * lax.rsqrt(jnp.sum(kh * kh, axis=-1, keepdims=True) + EPS)
        q_out[:, sl] = qn.astype(BF16)
        k_out[:, sl] = kn.astype(BF16)
    v_out[...] = act[:, 2 * qk_w:].astype(BF16)
    z_out[...] = proj[:, qkv_w:qkv_w + qk_w].astype(BF16)

    misc = proj[:, 2560:2688]
    lane = lax.broadcasted_iota(jnp.int32, misc.shape, 1)
    gdec = -jnp.exp(alog_ref[...]) * _softplus(misc + dtb_ref[...])
    beta = _sigmoid(misc)
    a0 = MLA_ROPE
    gb_out[...] = jnp.where((lane >= a0) & (lane < a0 + GDN_HEADS), gdec,
                            jnp.where((lane >= a0 + GDN_HEADS) & (lane < a0 + 2 * GDN_HEADS),
                                      beta, 0.0))
    cos = cos_ref[...]
    sin = sin_ref[...]
    kpe = _rope_tile(misc, cos, sin, lane).astype(BF16)

    cq = _rms(proj[:, 2048:2304], qn_ref[...]).astype(BF16)
    qf = _dot(cq, wuq_ref[...])
    ckv = _rms(proj[:, 2304:2560], kvn_ref[...]).astype(BF16)
    kv = _dot(ckv, wukv_ref[...])
    nope_w = MLA_HEADS * MLA_NOPE
    for hh in range(MLA_HEADS):
        base = 2 * LANES * hh
        qf_out[:, base:base + LANES] = qf[:, hh * LANES:(hh + 1) * LANES].astype(BF16)
        pe = qf[:, nope_w + hh * LANES: nope_w + (hh + 1) * LANES]
        qf_out[:, base + LANES:base + 2 * LANES] = _rope_tile(pe, cos, sin, lane).astype(BF16)
        kf_out[:, base:base + LANES] = kv[:, hh * LANES:(hh + 1) * LANES].astype(BF16)
        kf_out[:, base + LANES:base + 2 * LANES] = kpe
    vb_out[...] = kv[:, nope_w:].astype(BF16)


def _even_inproj(xf, g0, w_in_re, conv_w, alog_row, dtb_row, q_norm, kv_norm, wuq_re, wukv_re,
                 cos_t, sin_t, batch, tm):
    n, d = xf.shape
    t = n // batch
    nt = t // tm
    e_in = w_in_re.shape[1]
    qkv_w = conv_w.shape[1]
    qk_w = qkv_w // 3
    row = lambda w: pl.BlockSpec((tm, w), lambda b, i: (b * nt + i, 0))
    outs = [(qk_w, BF16)] * 4 + [(LANES, F32), (2 * LANES * MLA_HEADS, BF16),
                                 (2 * LANES * MLA_HEADS, BF16), (LANES * MLA_HEADS, BF16)]
    return pl.pallas_call(
        _even_in_kernel,
        grid=(batch, nt),
        in_specs=[row(d), _const_spec((1, d)), _const_spec((d, e_in)), _const_spec((CONV_W, qkv_w)),
                  _const_spec((1, LANES)), _const_spec((1, LANES)),
                  _const_spec(q_norm.shape), _const_spec(kv_norm.shape),
                  _const_spec(wuq_re.shape), _const_spec(wukv_re.shape),
                  row(LANES), row(LANES)],
        out_specs=[row(w) for w, _ in outs],
        out_shape=[jax.ShapeDtypeStruct((n, w), dt) for w, dt in outs],
        scratch_shapes=[pltpu.VMEM((tm + CARRY_ROWS, qkv_w), F32)],
        compiler_params=_cparams(2),
        name="even_inproj",
    )(xf, g0, w_in_re, conv_w, alog_row, dtb_row, q_norm, kv_norm, wuq_re, wukv_re, cos_t, sin_t)


SUPER = 2 * CHUNK
G_LANE = MLA_ROPE
B_LANE = MLA_ROPE + GDN_HEADS


def _gdn_kernel(q_ref, k_ref, v_ref, z_ref, gb_ref, on_ref, out_ref,
                gc_scr, gct_scr, gl_scr, u_scr, w_scr, a_scr, qe_scr, kd_scr, o_scr, s_scr):
    t = q_ref.shape[0]
    n_super = t // SUPER
    n_chunk = t // CHUNK

    gb = gb_ref[...]
    rowi = lax.broadcasted_iota(jnp.int32, gb.shape, 0) & (CHUNK - 1)
    gc = gb
    s = 1
    while s < CHUNK:
        gc = jnp.where(rowi >= s, gc + pltpu.roll(gc, s, axis=0), gc)
        s *= 2
    gc_scr[...] = gc
    gct_scr[...] = gc.T
    g3 = gc.reshape(n_chunk, CHUNK, LANES)
    gl_scr[...] = jnp.broadcast_to(g3[:, CHUNK - 1:CHUNK, :], g3.shape).reshape(t, LANES)

    ri = lax.broadcasted_iota(jnp.int32, (SUPER, SUPER), 0)
    ci = lax.broadcasted_iota(jnp.int32, (SUPER, SUPER), 1)
    same = (ri >= CHUNK) == (ci >= CHUNK)
    causal = same & (ri >= ci)
    strict = same & (ri > ci)
    eye = (ri == ci).astype(F32)

    def solve_body(sc, _):
        r0 = pl.multiple_of(sc * SUPER, SUPER)
        rows = pl.ds(r0, SUPER)
        for hh in range(GDN_HEADS):
            hs = slice(hh * GDN_DK, (hh + 1) * GDN_DK)
            kc = k_ref[rows, hs]
            qc = q_ref[rows, hs]
            vc = v_ref[rows, hs].astype(F32)
            kf = kc.astype(F32)
            gcol = gc_scr[rows, G_LANE + hh:G_LANE + hh + 1]
            glast = gl_scr[rows, G_LANE + hh:G_LANE + hh + 1]
            bcol = gb_ref[rows, B_LANE + hh:B_LANE + hh + 1]
            grow = gct_scr[G_LANE + hh:G_LANE + hh + 1, rows]
            decay = jnp.where(causal, jnp.exp(jnp.where(causal, gcol - grow, 0.0)), 0.0)
            kk = _dot_nt(kc, kc)
            m = jnp.where(strict, -(kk * bcol * decay), 0.0)
            qacc = eye + m
            mb = m.astype(BF16)
            mj = _dot(mb, mb)
            lvl = 2
            while lvl < CHUNK // 2:
                mb = mj.astype(BF16)
                r = _dot(mb, jnp.concatenate([mb, qacc.astype(BF16)], axis=1))
                mj = r[:, :SUPER]
                qacc = qacc + r[:, SUPER:]
                lvl *= 2
            qacc = qacc + _dot(mj.astype(BF16), qacc.astype(BF16))
            eg = jnp.exp(gcol)
            rhs = jnp.concatenate([vc * bcol, kf * (bcol * eg)], axis=1).astype(BF16)
            sol = _dot(qacc.astype(BF16), rhs)
            u_scr[rows, hs] = sol[:, :GDN_DK]
            w_scr[rows, hs] = sol[:, GDN_DK:].astype(BF16)
            aqk = _dot_nt(qc, kc) * decay
            a_sh = pltpu.roll(aqk, CHUNK, axis=1)
            a_scr[rows, hs] = jnp.where(ri < CHUNK, aqk, a_sh).astype(BF16)
            qe_scr[rows, hs] = (qc.astype(F32) * eg).astype(BF16)
            kd_scr[rows, hs] = (kf * jnp.exp(glast - gcol)).astype(BF16)
        return 0

    lax.fori_loop(0, n_super, solve_body, 0)

    s_scr[...] = jnp.zeros(s_scr.shape, F32)

    def scan_body(c, _):
        r0 = pl.multiple_of(c * CHUNK, CHUNK)
        rows = pl.ds(r0, CHUNK)
        for hh in range(GDN_HEADS):
            hs = slice(hh * GDN_DK, (hh + 1) * GDN_DK)
            st = s_scr[hh]
            sb = st.astype(BF16)
            vnew = u_scr[rows, hs] - _dot(w_scr[rows, hs], sb)
            vb = vnew.astype(BF16)
            o_scr[rows, hs] = _dot(qe_scr[rows, hs], sb) + _dot(a_scr[rows, hs][:, :CHUNK], vb)
            gam = jnp.exp(gl_scr[pl.ds(r0, 1), G_LANE + hh:G_LANE + hh + 1])
            s_scr[hh] = st * gam + lax.dot_general(kd_scr[rows, hs], vb, TN_DIMS,
                                                   preferred_element_type=F32)
        return 0

    lax.fori_loop(0, n_chunk, scan_body, 0)

    on = on_ref[...]
    for hh in range(GDN_HEADS):
        hs = slice(hh * GDN_DK, (hh + 1) * GDN_DK)
        z = z_ref[:, hs].astype(F32)
        out_ref[:, hs] = (_rms(o_scr[:, hs], on) * (z * _sigmoid(z))).astype(BF16)


def _gdn(q, k, v, z, gb, o_norm, batch):
    n, w = q.shape
    t = n // batch
    row = lambda ww: pl.BlockSpec((t, ww), lambda b: (b, 0))
    return pl.pallas_call(
        _gdn_kernel,
        grid=(batch,),
        in_specs=[row(w), row(w), row(w), row(w), row(LANES), _const_spec((1, GDN_DK))],
        out_specs=row(w),
        out_shape=jax.ShapeDtypeStruct((n, w), BF16),
        scratch_shapes=[pltpu.VMEM((t, LANES), F32),
                        pltpu.VMEM((LANES, t), F32),
                        pltpu.VMEM((t, LANES), F32),
                        pltpu.VMEM((t, w), F32),
                        pltpu.VMEM((t, w), BF16),
                        pltpu.VMEM((t, w), BF16),
                        pltpu.VMEM((t, w), BF16),
                        pltpu.VMEM((t, w), BF16),
                        pltpu.VMEM((t, w), F32),
                        pltpu.VMEM((GDN_HEADS, GDN_DK, GDN_DK), F32)],
        compiler_params=_cparams(1),
        name="gdn",
    )(q, k, v, z, gb, o_norm)


MLA_TQ = 256


def _mla_kernel(q_ref, k_ref, v_ref, o_ref):
    t = q_ref.shape[0]
    tq = MLA_TQ
    nq = t // tq
    ri = lax.broadcasted_iota(jnp.int32, (tq, tq), 0) // CHUNK
    ci = lax.broadcasted_iota(jnp.int32, (tq, tq), 1) // CHUNK
    diag_mask = ci <= ri

    def kv_step(q, j, carry, masked):
        m, l, acc = carry
        k0 = pl.multiple_of(j * tq, tq)
        k = k_ref[pl.ds(k0, tq), :]
        v = v_ref[pl.ds(k0, tq), :]
        s = _dot_nt(q, k) * MLA_SCALE
        if masked:
            s = jnp.where(diag_mask, s, -jnp.inf)
        m_new = jnp.maximum(m, jnp.max(s, axis=-1, keepdims=True))
        alpha = jnp.exp(m - m_new)
        p = jnp.exp(s - m_new)
        l = alpha * l + jnp.sum(p, axis=-1, keepdims=True)
        acc = alpha * acc + _dot(p.astype(BF16), v)
        return m_new, l, acc

    def q_block(i, _):
        q0 = pl.multiple_of(i * tq, tq)
        q = q_ref[pl.ds(q0, tq), :]
        init = (jnp.full((tq, 1), -jnp.inf, F32), jnp.zeros((tq, 1), F32),
                jnp.zeros((tq, v_ref.shape[1]), F32))
        carry = lax.fori_loop(0, i, lambda j, c: kv_step(q, j, c, False), init)
        m, l, acc = kv_step(q, i, carry, True)
        o_ref[pl.ds(q0, tq), :] = (acc / l).astype(BF16)
        return 0

    lax.fori_loop(0, nq, q_block, 0)


def _mla(qf, kf, vb, batch):
    n = qf.shape[0]
    t = n // batch
    hw = 2 * LANES
    return pl.pallas_call(
        _mla_kernel,
        grid=(batch, MLA_HEADS),
        in_specs=[pl.BlockSpec((t, hw), lambda b, h: (b, h)),
                  pl.BlockSpec((t, hw), lambda b, h: (b, h)),
                  pl.BlockSpec((t, LANES), lambda b, h: (b, h))],
        out_specs=pl.BlockSpec((t, LANES), lambda b, h: (b, h)),
        out_shape=jax.ShapeDtypeStruct((n, MLA_HEADS * LANES), BF16),
        compiler_params=_cparams(2),
        name="mla_attn",
    )(qf, kf, vb)


def _out_kernel(x_ref, a_ref, b_ref, w_ref, g_ref, o_ref):
    ka = a_ref.shape[1]
    y = _dot(a_ref[...], w_ref[0:ka, :]) + _dot(b_ref[...], w_ref[ka:, :])
    o_ref[...] = x_ref[...] + _rms(y, g_ref[...])


def _out_proj(xf, a, b, w_out, g, tm):
    n, d = xf.shape
    row = lambda w: pl.BlockSpec((tm, w), lambda i: (i, 0))
    return pl.pallas_call(
        _out_kernel,
        grid=(n // tm,),
        in_specs=[row(d), row(a.shape[1]), row(b.shape[1]), _const_spec(w_out.shape),
                  _const_spec((1, d))],
        out_specs=row(d),
        out_shape=jax.ShapeDtypeStruct((n, d), F32),
        compiler_params=_cparams(1),
        name="even_outproj",
    )(xf, a, b, w_out, g)


def _gelu_tanh(x):
    c = math.sqrt(2.0 / math.pi)
    return 0.5 * x * (1.0 + jnp.tanh(c * (x + 0.044715 * (x * x * x))))


def _odd_kernel(x_ref, g0_ref, win_ref, cw_ref, cb_ref, gaw_ref, gab_ref, gxw_ref, gxb_ref,
                ap_ref, wout_ref, g1_ref, o_ref, xpad, hcar):
    tm, d = x_ref.shape
    bw = d // LRU_BLOCKS

    @pl.when(pl.program_id(1) == 0)
    def _():
        xpad[0:CARRY_ROWS, :] = jnp.zeros((CARRY_ROWS, d), F32)
        hcar[...] = jnp.zeros(hcar.shape, F32)

    x = x_ref[...]
    h = _rms(x, g0_ref[...]).astype(BF16)
    xy = _dot(h, win_ref[...])
    xb = xy[:, :d]
    gate = _gelu_tanh(xy[:, d:])

    xpad[CARRY_ROWS:CARRY_ROWS + tm, :] = xb
    xc = cw_ref[CONV_W - 1:CONV_W, :] * xb + cb_ref[...]
    for j in range(CONV_W - 1):
        xc = xc + cw_ref[j:j + 1, :] * xpad[pl.ds(CARRY_ROWS - (CONV_W - 1) + j, tm), :]
    xpad[0:CARRY_ROWS, :] = xb[tm - CARRY_ROWS:, :]

    xcb = xc.astype(BF16)
    ra = jnp.concatenate([_dot(xcb[:, n * bw:(n + 1) * bw], gaw_ref[n]) for n in range(LRU_BLOCKS)],
                         axis=1)
    ia = jnp.concatenate([_dot(xcb[:, n * bw:(n + 1) * bw], gxw_ref[n]) for n in range(LRU_BLOCKS)],
                         axis=1)
    r = _sigmoid(ra + gab_ref[...])
    ig = _sigmoid(ia + gxb_ref[...])
    log_a = (-LRU_C) * r * _softplus(-ap_ref[...])
    a = jnp.exp(log_a)
    th = jnp.tanh(log_a)
    u = jnp.sqrt(-2.0 * th / (1.0 - th)) * (ig * xc)

    rowi = lax.broadcasted_iota(jnp.int32, (tm, 1), 0)
    s = 1
    while s < tm:
        valid = rowi >= s
        a_sh = pltpu.roll(a, s, axis=0)
        u_sh = pltpu.roll(u, s, axis=0)
        u = jnp.where(valid, a * u_sh + u, u)
        a = jnp.where(valid, a * a_sh, a)
        s *= 2
    hs = a * hcar[...] + u
    hcar[...] = hs[tm - 1:tm, :]

    y = _dot((hs * gate).astype(BF16), wout_ref[...])
    o_ref[...] = x + _rms(y, g1_ref[...])


def _odd_mixer(xf, g0, g1, w_in, conv_w, conv_b, gaw, gab, gxw, gxb, a_param, w_out, batch, tm):
    n, d = xf.shape
    t = n // batch
    nt = t // tm
    row = pl.BlockSpec((tm, d), lambda b, i: (b * nt + i, 0))
    vec = _const_spec((1, d))
    return pl.pallas_call(
        _odd_kernel,
        grid=(batch, nt),
        in_specs=[row, vec, _const_spec(w_in.shape), _const_spec(conv_w.shape), vec,
                  _const_spec(gaw.shape), vec, _const_spec(gxw.shape), vec, vec,
                  _const_spec(w_out.shape), vec],
        out_specs=row,
        out_shape=jax.ShapeDtypeStruct((n, d), F32),
        scratch_shapes=[pltpu.VMEM((tm + CARRY_ROWS, d), F32), pltpu.VMEM((1, d), F32)],
        compiler_params=_cparams(2),
        name="odd_mixer",
    )(xf, g0, w_in, conv_w, conv_b, gaw, gab, gxw, gxb, a_param, w_out, g1)


def _xattn_kernel(x_ref, g2_ref, wq_ref, kv_ref, wo_ref, g3_ref, o_ref):
    tm, d = x_ref.shape
    hd = d // XA_HEADS
    x = x_ref[...]
    h = _rms(x, g2_ref[...]).astype(BF16)
    q = _dot(h, wq_ref[...]).astype(BF16)
    outs = []
    for hh in range(XA_HEADS):
        k = kv_ref[0, :, hh * hd:(hh + 1) * hd]
        v = kv_ref[0, :, d + hh * hd: d + (hh + 1) * hd]
        s = _dot_nt(q[:, hh * hd:(hh + 1) * hd], k) * (hd ** -0.5)
        m = jnp.max(s, axis=-1, keepdims=True)
        p = jnp.exp(s - m)
        p = p / jnp.sum(p, axis=-1, keepdims=True)
        outs.append(_dot(p.astype(BF16), v).astype(BF16))
    o = jnp.concatenate(outs, axis=1)
    y = _dot(o, wo_ref[...])
    o_ref[...] = x + _rms(y, g3_ref[...])


def _xattn(xf, g2, g3, wq, kv_mem, wo, batch, tm):
    n, d = xf.shape
    t = n // batch
    nt = t // tm
    n_mem = kv_mem.shape[0] // batch
    row = pl.BlockSpec((tm, d), lambda b, i: (b * nt + i, 0))
    vec = _const_spec((1, d))
    kv3 = kv_mem.reshape(batch, n_mem, 2 * d)
    return pl.pallas_call(
        _xattn_kernel,
        grid=(batch, nt),
        in_specs=[row, vec, _const_spec(wq.shape),
                  pl.BlockSpec((1, n_mem, 2 * d), lambda b, i: (b, 0, 0)),
                  _const_spec(wo.shape), vec],
        out_specs=row,
        out_shape=jax.ShapeDtypeStruct((n, d), F32),
        compiler_params=_cparams(2),
        name="xattn",
    )(xf, g2, wq, kv3, wo, g3)


FFN_TF = 256


def _ffn_kernel(x_ref, g4_ref, wg_ref, wu_ref, wo_ref, g5_ref, o_ref, h_scr, acc_scr):
    f = pl.program_id(1)

    @pl.when(f == 0)
    def _():
        h_scr[...] = _rms(x_ref[...], g4_ref[...]).astype(BF16)
        acc_scr[...] = jnp.zeros(acc_scr.shape, F32)

    h = h_scr[...]
    gt = _dot(h, wg_ref[...])
    up = _dot(h, wu_ref[...])
    act = (gt * _sigmoid(gt) * up).astype(BF16)
    acc_scr[...] += _dot(act, wo_ref[...])

    @pl.when(f == pl.num_programs(1) - 1)
    def _():
        o_ref[...] = x_ref[...] + _rms(acc_scr[...], g5_ref[...])


def _ffn(xf, g4, g5, w_in, w_out, tm):
    n, d = xf.shape
    dff = w_out.shape[0]
    nf = dff // FFN_TF
    row = pl.BlockSpec((tm, d), lambda i, f: (i, 0))
    vec = _const_spec((1, d))
    return pl.pallas_call(
        _ffn_kernel,
        grid=(n // tm, nf),
        in_specs=[row, vec,
                  pl.BlockSpec((d, FFN_TF), lambda i, f: (0, f)),
                  pl.BlockSpec((d, FFN_TF), lambda i, f: (0, nf + f)),
                  pl.BlockSpec((FFN_TF, d), lambda i, f: (f, 0)),
                  vec],
        out_specs=row,
        out_shape=jax.ShapeDtypeStruct((n, d), F32),
        scratch_shapes=[pltpu.VMEM((tm, d), BF16), pltpu.VMEM((tm, d), F32)],
        compiler_params=_cparams(2),
        name="ffn",
    )(xf, g4, w_in, w_in, w_out, g5)


def _relayout_even(w_in, w_uq, w_ukv, a_log, dt_bias):
    d = w_in.shape[0]
    qkvz = 3 * GDN_HEADS * GDN_DK + GDN_HEADS * GDN_DK
    ab = 2 * GDN_HEADS
    lat = w_in.shape[1] - qkvz - ab - MLA_ROPE
    pad = LANES - MLA_ROPE - ab
    w_in_re = jnp.concatenate(
        [w_in[:, :qkvz], w_in[:, qkvz + ab:qkvz + ab + lat], w_in[:, qkvz + ab + lat:],
         w_in[:, qkvz:qkvz + ab], jnp.zeros((d, pad), w_in.dtype)], axis=1).astype(BF16)
    r = w_uq.shape[0]
    uq = w_uq.reshape(r, MLA_HEADS, MLA_NOPE + MLA_ROPE)
    pe = jnp.pad(uq[:, :, MLA_NOPE:], ((0, 0), (0, 0), (0, LANES - MLA_ROPE)))
    wuq_re = jnp.concatenate([uq[:, :, :MLA_NOPE].reshape(r, -1), pe.reshape(r, -1)],
                             axis=1).astype(BF16)
    ukv = w_ukv.reshape(w_ukv.shape[0], MLA_HEADS, -1)
    wukv_re = jnp.concatenate([ukv[:, :, :MLA_NOPE].reshape(r, -1),
                               ukv[:, :, MLA_NOPE:].reshape(r, -1)], axis=1).astype(BF16)
    alog_row = jnp.zeros((1, LANES), F32).at[0, G_LANE:G_LANE + GDN_HEADS].set(a_log)
    dtb_row = jnp.zeros((1, LANES), F32).at[0, G_LANE:G_LANE + GDN_HEADS].set(dt_bias)
    return w_in_re, wuq_re, wukv_re, alog_row, dtb_row


def kernel(x, mem, positions, norm_gains, mem_norm, e_w_in, e_conv_w, e_a_log, e_dt_bias, e_o_norm,
           e_q_norm, e_kv_norm, e_w_uq, e_w_ukv, e_w_out, o_w_in, o_conv_w, o_conv_b, o_gate_a_w,
           o_gate_a_b, o_gate_x_w, o_gate_x_b, o_a_param, o_w_out, xa_wq, xa_wkv, xa_wo, ffn_w_in,
           ffn_w_out):
    batch, t, d = x.shape
    n = batch * t
    depth = norm_gains.shape[0]
    xf = x.reshape(n, d)
    gains = norm_gains.reshape(depth, -1, 1, d)

    cos_t, sin_t = _rope_tables(positions)
    kv_mem = _mem_kv(mem.reshape(-1, d), mem_norm, xa_wkv.astype(BF16))

    for layer in range(depth):
        g = gains[layer]
        if layer % 2 == 0:
            e = layer // 2
            w_in_re, wuq_re, wukv_re, alog_row, dtb_row = _relayout_even(
                e_w_in[e], e_w_uq[e], e_w_ukv[e], e_a_log[e], e_dt_bias[e])
            q, k, v, z, gb, qf, kf, vb = _even_inproj(
                xf, g[0], w_in_re, e_conv_w[e], alog_row, dtb_row,
                e_q_norm[e].reshape(1, -1), e_kv_norm[e].reshape(1, -1), wuq_re, wukv_re,
                cos_t, sin_t, batch, tm=512)
            out_a = _gdn(q, k, v, z, gb, e_o_norm[e].reshape(1, -1), batch)
            out_b = _mla(qf, kf, vb, batch)
            xf = _out_proj(xf, out_a, out_b, e_w_out[e].astype(BF16), g[1], tm=1024)
        else:
            o = layer // 2
            xf = _odd_mixer(xf, g[0], g[1], o_w_in[o].astype(BF16), o_conv_w[o],
                            o_conv_b[o].reshape(1, d), o_gate_a_w[o].astype(BF16),
                            o_gate_a_b[o].reshape(1, d), o_gate_x_w[o].astype(BF16),
                            o_gate_x_b[o].reshape(1, d), o_a_param[o].reshape(1, d),
                            o_w_out[o].astype(BF16), batch, tm=256)
        xf = _xattn(xf, g[2], g[3], xa_wq[layer].astype(BF16), kv_mem[layer],
                    xa_wo[layer].astype(BF16), batch, tm=512)
        xf = _ffn(xf, g[4], g[5], ffn_w_in[layer].astype(BF16), ffn_w_out[layer].astype(BF16),
                  tm=1024)
    return xf.reshape(batch, t, d)
```

```python
import functools
import math

import jax
import jax.numpy as jnp
from jax import lax
from jax.experimental import pallas as pl
from jax.experimental.pallas import tpu as pltpu

F32 = jnp.float32
BF16 = jnp.bfloat16

EPS = 1e-6
CHUNK = 64
GDN_HEADS = 4
GDN_DK = 128
MLA_HEADS = 4
MLA_NOPE = 128
MLA_ROPE = 64
MLA_SCALE = (MLA_NOPE + MLA_ROPE) ** -0.5
ROPE_BASE = 10000.0
LRU_BLOCKS = 4
LRU_C = 8.0
XA_HEADS = 4
CONV_W = 4

LANES = 128
CARRY_ROWS = 8
V7X_VMEM_LIMIT = 56 * 1024 * 1024

NT_DIMS = (((1,), (1,)), ((), ()))
TN_DIMS = (((0,), (0,)), ((), ()))


def _cparams(n_axes):
    return pltpu.CompilerParams(
        dimension_semantics=("arbitrary",) * n_axes,
        vmem_limit_bytes=V7X_VMEM_LIMIT)


def _dot(a, b):
    return jnp.dot(a, b, preferred_element_type=F32)


def _dot_nt(a, b):
    return lax.dot_general(a, b, NT_DIMS, preferred_element_type=F32)


def _rms(xf, g):
    ms = jnp.mean(xf * xf, axis=-1, keepdims=True)
    return xf * lax.rsqrt(ms + EPS) * g


def _sigmoid(x):
    return 1.0 / (1.0 + jnp.exp(-x))


def _softplus(x):
    return jnp.maximum(x, 0.0) + jnp.log1p(jnp.exp(-jnp.abs(x)))


def _const_spec(shape):
    nd = len(shape)
    return pl.BlockSpec(shape, lambda *_: (0,) * nd)


def _rope_kernel(pos_ref, inv_ref, cos_ref, sin_ref):
    ang = pos_ref[...].astype(F32) * inv_ref[...]
    cos_ref[...] = jnp.cos(ang)
    sin_ref[...] = jnp.sin(ang)


def _rope_tables(positions):
    n = positions.size
    half = MLA_ROPE // 2
    per_row = LANES // half
    inv_freq = ROPE_BASE ** (-jnp.arange(0, MLA_ROPE, 2, dtype=F32) / MLA_ROPE)
    inv_row = jnp.tile(inv_freq, per_row).reshape(1, LANES)
    pos_rep = jnp.repeat(positions.reshape(n // per_row, per_row), half, axis=1)
    rows = n // per_row
    tr = min(rows, 1024)
    spec = pl.BlockSpec((tr, LANES), lambda i: (i, 0))
    cos_p, sin_p = pl.pallas_call(
        _rope_kernel,
        grid=(rows // tr,),
        in_specs=[spec, _const_spec((1, LANES))],
        out_specs=[spec, spec],
        out_shape=[jax.ShapeDtypeStruct((rows, LANES), F32)] * 2,
        compiler_params=_cparams(1),
        name="rope_tables",
    )(pos_rep, inv_row)
    cos_t = jnp.tile(cos_p.reshape(n, half), (1, per_row))
    sin_t = jnp.tile(sin_p.reshape(n, half), (1, per_row))
    return cos_t, sin_t


def _memkv_kernel(mem_ref, g_ref, w_ref, o_ref):
    mn = _rms(mem_ref[...], g_ref[...]).astype(BF16)
    o_ref[0] = _dot(mn, w_ref[0]).astype(BF16)


def _mem_kv(mem2d, mem_norm, wkv_bf16):
    depth, d, d2 = wkv_bf16.shape
    rows = mem2d.shape[0]
    tr = min(rows, 512)
    return pl.pallas_call(
        _memkv_kernel,
        grid=(depth, rows // tr),
        in_specs=[pl.BlockSpec((tr, d), lambda l, i: (i, 0)),
                  _const_spec((1, d)),
                  pl.BlockSpec((1, d, d2), lambda l, i: (l, 0, 0))],
        out_specs=pl.BlockSpec((1, tr, d2), lambda l, i: (l, i, 0)),
        out_shape=jax.ShapeDtypeStruct((depth, rows, d2), BF16),
        compiler_params=_cparams(2),
        name="mem_kv",
    )(mem2d, mem_norm.reshape(1, d), wkv_bf16)


def _rope_tile(x, cos, sin, lane):
    half = MLA_ROPE // 2
    up = pltpu.roll(x, LANES - half, axis=1)
    dn = pltpu.roll(x, half, axis=1)
    rot = jnp.where(lane < half, -up, dn)
    return jnp.where(lane < MLA_ROPE, x * cos + rot * sin, 0.0)


def _even_in_kernel(x_ref, g_ref, win_ref, cw_ref, alog_ref, dtb_ref, qn_ref, kvn_ref,
                    wuq_ref, wukv_ref, cos_ref, sin_ref,
                    q_out, k_out, v_out, z_out, gb_out, qf_out, kf_out, vb_out,
                    xpad):
    tm = x_ref.shape[0]
    qk_w = GDN_HEADS * GDN_DK
    qkv_w = 3 * qk_w

    @pl.when(pl.program_id(1) == 0)
    def _():
        xpad[0:CARRY_ROWS, :] = jnp.zeros((CARRY_ROWS, qkv_w), F32)

    h = _rms(x_ref[...], g_ref[...]).astype(BF16)
    proj = _dot(h, win_ref[...])

    qkv = proj[:, :qkv_w]
    xpad[CARRY_ROWS:CARRY_ROWS + tm, :] = qkv
    acc = cw_ref[CONV_W - 1:CONV_W, :] * qkv
    for j in range(CONV_W - 1):
        acc = acc + cw_ref[j:j + 1, :] * xpad[pl.ds(CARRY_ROWS - (CONV_W - 1) + j, tm), :]
    xpad[0:CARRY_ROWS, :] = qkv[tm - CARRY_ROWS:, :]
    act = acc * _sigmoid(acc)
    for hh in range(GDN_HEADS):
        sl = slice(hh * GDN_DK, (hh + 1) * GDN_DK)
        qh = act[:, sl]
        kh = act[:, qk_w + hh * GDN_DK: qk_w + (hh + 1) * GDN_DK]
        qn = qh * lax.rsqrt(jnp.sum(qh * qh, axis=-1, keepdims=True) + EPS) * (GDN_DK ** -0.5)
        kn = kh * lax.rsqrt(jnp.sum(kh * kh, axis=-1, keepdims=True) + EPS)
        q_out[:, sl] = qn.astype(BF16)
        k_out[:, sl] = kn.astype(BF16)
    v_out[...] = act[:, 2 * qk_w:].astype(BF16)
    z_out[...] = proj[:, qkv_w:qkv_w + qk_w].astype(BF16)

    misc = proj[:, 2560:2688]
    lane = lax.broadcasted_iota(jnp.int32, misc.shape, 1)
    gdec = -jnp.exp(alog_ref[...]) * _softplus(misc + dtb_ref[...])
    beta = _sigmoid(misc)
    a0 = MLA_ROPE
    gb_out[...] = jnp.where((lane >= a0) & (lane < a0 + GDN_HEADS), gdec,
                            jnp.where((lane >= a0 + GDN_HEADS) & (lane < a0 + 2 * GDN_HEADS),
                                      beta, 0.0))
    cos = cos_ref[...]
    sin = sin_ref[...]
    kpe = _rope_tile(misc, cos, sin, lane).astype(BF16)

    cq = _rms(proj[:, 2048:2304], qn_ref[...]).astype(BF16)
    qf = _dot(cq, wuq_ref[...])
    ckv = _rms(proj[:, 2304:2560], kvn_ref[...]).astype(BF16)
    kv = _dot(ckv, wukv_ref[...])
    nope_w = MLA_HEADS * MLA_NOPE
    for hh in range(MLA_HEADS):
        base = 2 * LANES * hh
        qf_out[:, base:base + LANES] = qf[:, hh * LANES:(hh + 1) * LANES].astype(BF16)
        pe = qf[:, nope_w + hh * LANES: nope_w + (hh + 1) * LANES]
        qf_out[:, base + LANES:base + 2 * LANES] = _rope_tile(pe, cos, sin, lane).astype(BF16)
        kf_out[:, base:base + LANES] = kv[:, hh * LANES:(hh + 1) * LANES].astype(BF16)
        kf_out[:, base + LANES:base + 2 * LANES] = kpe
    vb_out[...] = kv[:, nope_w:].astype(BF16)


def _even_inproj(xf, g0, w_in_re, conv_w, alog_row, dtb_row, q_norm, kv_norm, wuq_re, wukv_re,
                 cos_t, sin_t, batch, tm):
    n, d = xf.shape
    t = n // batch
    nt = t // tm
    e_in = w_in_re.shape[1]
    qkv_w = conv_w.shape[1]
    qk_w = qkv_w // 3
    row = lambda w: pl.BlockSpec((tm, w), lambda b, i: (b * nt + i, 0))
    outs = [(qk_w, BF16)] * 4 + [(LANES, F32), (2 * LANES * MLA_HEADS, BF16),
                                 (2 * LANES * MLA_HEADS, BF16), (LANES * MLA_HEADS, BF16)]
    return pl.pallas_call(
        _even_in_kernel,
        grid=(batch, nt),
        in_specs=[row(d), _const_spec((1, d)), _const_spec((d, e_in)), _const_spec((CONV_W, qkv_w)),
                  _const_spec((1, LANES)), _const_spec((1, LANES)),
                  _const_spec(q_norm.shape), _const_spec(kv_norm.shape),
                  _const_spec(wuq_re.shape), _const_spec(wukv_re.shape),
                  row(LANES), row(LANES)],
        out_specs=[row(w) for w, _ in outs],
        out_shape=[jax.ShapeDtypeStruct((n, w), dt) for w, dt in outs],
        scratch_shapes=[pltpu.VMEM((tm + CARRY_ROWS, qkv_w), F32)],
        compiler_params=_cparams(2),
        name="even_inproj",
    )(xf, g0, w_in_re, conv_w, alog_row, dtb_row, q_norm, kv_norm, wuq_re, wukv_re, cos_t, sin_t)


SUPER = 2 * CHUNK
G_LANE = MLA_ROPE
B_LANE = MLA_ROPE + GDN_HEADS
SOLVE_UNROLL = 2


def _gdn_kernel(q_ref, k_ref, v_ref, z_ref, gb_ref, on_ref, out_ref,
                gc_scr, gct_scr, gl_scr, u_scr, w_scr, a_scr, qe_scr, kd_scr, o_scr, s_scr):
    t = q_ref.shape[0]
    n_super = t // SUPER
    n_chunk = t // CHUNK

    gb = gb_ref[...]
    rowi = lax.broadcasted_iota(jnp.int32, gb.shape, 0) & (CHUNK - 1)
    gc = gb
    s = 1
    while s < CHUNK:
        gc = jnp.where(rowi >= s, gc + pltpu.roll(gc, s, axis=0), gc)
        s *= 2
    gc_scr[...] = gc
    gct_scr[...] = gc.T
    g3 = gc.reshape(n_chunk, CHUNK, LANES)
    gl_scr[...] = jnp.broadcast_to(g3[:, CHUNK - 1:CHUNK, :], g3.shape).reshape(t, LANES)

    ri = lax.broadcasted_iota(jnp.int32, (SUPER, SUPER), 0)
    ci = lax.broadcasted_iota(jnp.int32, (SUPER, SUPER), 1)
    same = (ri >= CHUNK) == (ci >= CHUNK)
    causal = same & (ri >= ci)
    strict = same & (ri > ci)
    eye = (ri == ci).astype(F32)

    heads = range(GDN_HEADS)
    hsl = [slice(hh * GDN_DK, (hh + 1) * GDN_DK) for hh in heads]
    lane_g = [slice(G_LANE + hh, G_LANE + hh + 1) for hh in heads]
    lane_b = [slice(B_LANE + hh, B_LANE + hh + 1) for hh in heads]

    def solve_body(it, _):
        probs = []
        for sub in range(SOLVE_UNROLL):
            r0 = pl.multiple_of((it * SOLVE_UNROLL + sub) * SUPER, SUPER)
            probs += [(pl.ds(r0, SUPER), hh) for hh in heads]
        kc = [k_ref[rows, hsl[hh]] for rows, hh in probs]
        qc = [q_ref[rows, hsl[hh]] for rows, hh in probs]
        gcol = [gc_scr[rows, lane_g[hh]] for rows, hh in probs]
        bcol = [gb_ref[rows, lane_b[hh]] for rows, hh in probs]
        grow = [gct_scr[lane_g[hh], rows] for rows, hh in probs]
        kk = [_dot_nt(k, k) for k in kc]
        qk = [_dot_nt(q, k) for q, k in zip(qc, kc)]
        decay = [jnp.where(causal, jnp.exp(jnp.where(causal, gc - gr, 0.0)), 0.0)
                 for gc, gr in zip(gcol, grow)]
        m = [jnp.where(strict, -(x * b * dc), 0.0) for x, b, dc in zip(kk, bcol, decay)]
        qacc = [eye + x for x in m]
        mb = [x.astype(BF16) for x in m]
        mj = [_dot(x, x) for x in mb]
        lvl = 2
        while lvl < CHUNK // 2:
            mb = [x.astype(BF16) for x in mj]
            r = [_dot(x, jnp.concatenate([x, qa.astype(BF16)], axis=1)) for x, qa in zip(mb, qacc)]
            mj = [x[:, :SUPER] for x in r]
            qacc = [qa + x[:, SUPER:] for qa, x in zip(qacc, r)]
            lvl *= 2
        qacc = [qa + _dot(x.astype(BF16), qa.astype(BF16)) for qa, x in zip(qacc, mj)]
        eg = [jnp.exp(gc) for gc in gcol]
        sol = []
        for i, (rows, hh) in enumerate(probs):
            vc = v_ref[rows, hsl[hh]].astype(F32)
            kf = kc[i].astype(F32)
            rhs = jnp.concatenate([vc * bcol[i], kf * (bcol[i] * eg[i])], axis=1).astype(BF16)
            sol.append(_dot(qacc[i].astype(BF16), rhs))
        for i, (rows, hh) in enumerate(probs):
            u_scr[rows, hsl[hh]] = sol[i][:, :GDN_DK]
            w_scr[rows, hsl[hh]] = sol[i][:, GDN_DK:].astype(BF16)
            aqk = qk[i] * decay[i]
            a_sh = pltpu.roll(aqk, CHUNK, axis=1)
            a_scr[rows, hsl[hh]] = jnp.where(ri < CHUNK, aqk, a_sh).astype(BF16)
            qe_scr[rows, hsl[hh]] = (qc[i].astype(F32) * eg[i]).astype(BF16)
            glast = gl_scr[rows, lane_g[hh]]
            kd_scr[rows, hsl[hh]] = (kc[i].astype(F32) * jnp.exp(glast - gcol[i])).astype(BF16)
        return 0

    lax.fori_loop(0, n_super // SOLVE_UNROLL, solve_body, 0)

    s_scr[...] = jnp.zeros(s_scr.shape, F32)

    def scan_body(c, _):
        r0 = pl.multiple_of(c * CHUNK, CHUNK)
        rows = pl.ds(r0, CHUNK)
        st = [s_scr[hh] for hh in heads]
        sb = [x.astype(BF16) for x in st]
        t1 = [_dot(w_scr[rows, hsl[hh]], sb[hh]) for hh in heads]
        oq = [_dot(qe_scr[rows, hsl[hh]], sb[hh]) for hh in heads]
        vb = [(u_scr[rows, hsl[hh]] - t1[hh]).astype(BF16) for hh in heads]
        oa = [_dot(a_scr[rows, hsl[hh]][:, :CHUNK], vb[hh]) for hh in heads]
        kv = [lax.dot_general(kd_scr[rows, hsl[hh]], vb[hh], TN_DIMS, preferred_element_type=F32)
              for hh in heads]
        for hh in heads:
            gam = jnp.exp(gl_scr[pl.ds(r0, 1), lane_g[hh]])
            o_scr[rows, hsl[hh]] = oq[hh] + oa[hh]
            s_scr[hh] = st[hh] * gam + kv[hh]
        return 0

    lax.fori_loop(0, n_chunk, scan_body, 0)

    on = on_ref[...]
    for hh in range(GDN_HEADS):
        hs = slice(hh * GDN_DK, (hh + 1) * GDN_DK)
        z = z_ref[:, hs].astype(F32)
        out_ref[:, hs] = (_rms(o_scr[:, hs], on) * (z * _sigmoid(z))).astype(BF16)


def _gdn(q, k, v, z, gb, o_norm, batch):
    n, w = q.shape
    t = n // batch
    row = lambda ww: pl.BlockSpec((t, ww), lambda b: (b, 0))
    return pl.pallas_call(
        _gdn_kernel,
        grid=(batch,),
        in_specs=[row(w), row(w), row(w), row(w), row(LANES), _const_spec((1, GDN_DK))],
        out_specs=row(w),
        out_shape=jax.ShapeDtypeStruct((n, w), BF16),
        scratch_shapes=[pltpu.VMEM((t, LANES), F32),
                        pltpu.VMEM((LANES, t), F32),
                        pltpu.VMEM((t, LANES), F32),
                        pltpu.VMEM((t, w), F32),
                        pltpu.VMEM((t, w), BF16),
                        pltpu.VMEM((t, w), BF16),
                        pltpu.VMEM((t, w), BF16),
                        pltpu.VMEM((t, w), BF16),
                        pltpu.VMEM((t, w), F32),
                        pltpu.VMEM((GDN_HEADS, GDN_DK, GDN_DK), F32)],
        compiler_params=_cparams(1),
        name="gdn",
    )(q, k, v, z, gb, o_norm)


MLA_TQ = 256


def _mla_kernel(q_ref, k_ref, v_ref, o_ref):
    t = q_ref.shape[0]
    tq = MLA_TQ
    nq = t // tq
    hw = 2 * LANES
    heads = range(MLA_HEADS)
    ri = lax.broadcasted_iota(jnp.int32, (tq, tq), 0) // CHUNK
    ci = lax.broadcasted_iota(jnp.int32, (tq, tq), 1) // CHUNK
    diag_mask = ci <= ri

    def kv_step(qs, j, carry, masked):
        k0 = pl.multiple_of(j * tq, tq)
        rows = pl.ds(k0, tq)
        s = [_dot_nt(qs[h], k_ref[rows, h * hw:(h + 1) * hw]) * MLA_SCALE for h in heads]
        if masked:
            s = [jnp.where(diag_mask, x, -jnp.inf) for x in s]
        m_new = [jnp.maximum(carry[h][0], jnp.max(s[h], axis=-1, keepdims=True)) for h in heads]
        p = [jnp.exp(s[h] - m_new[h]) for h in heads]
        pv = [_dot(p[h].astype(BF16), v_ref[rows, h * LANES:(h + 1) * LANES]) for h in heads]
        out = []
        for h in heads:
            m, l, acc = carry[h]
            alpha = jnp.exp(m - m_new[h])
            out.append((m_new[h], alpha * l + jnp.sum(p[h], axis=-1, keepdims=True),
                        alpha * acc + pv[h]))
        return tuple(out)

    def q_block(i, _):
        q0 = pl.multiple_of(i * tq, tq)
        qs = [q_ref[pl.ds(q0, tq), h * hw:(h + 1) * hw] for h in heads]
        init = tuple((jnp.full((tq, 1), -jnp.inf, F32), jnp.zeros((tq, 1), F32),
                      jnp.zeros((tq, LANES), F32)) for _ in heads)
        carry = lax.fori_loop(0, i, lambda j, c: kv_step(qs, j, c, False), init)
        carry = kv_step(qs, i, carry, True)
        for h in heads:
            _, l, acc = carry[h]
            o_ref[pl.ds(q0, tq), h * LANES:(h + 1) * LANES] = (acc / l).astype(BF16)
        return 0

    lax.fori_loop(0, nq, q_block, 0)


def _mla(qf, kf, vb, batch):
    n = qf.shape[0]
    t = n // batch
    row = lambda w: pl.BlockSpec((t, w), lambda b: (b, 0))
    return pl.pallas_call(
        _mla_kernel,
        grid=(batch,),
        in_specs=[row(qf.shape[1]), row(kf.shape[1]), row(vb.shape[1])],
        out_specs=row(vb.shape[1]),
        out_shape=jax.ShapeDtypeStruct((n, vb.shape[1]), BF16),
        compiler_params=_cparams(1),
        name="mla_attn",
    )(qf, kf, vb)


def _out_kernel(x_ref, a_ref, b_ref, w_ref, g_ref, o_ref):
    ka = a_ref.shape[1]
    y = _dot(a_ref[...], w_ref[0:ka, :]) + _dot(b_ref[...], w_ref[ka:, :])
    o_ref[...] = x_ref[...] + _rms(y, g_ref[...])


def _out_proj(xf, a, b, w_out, g, tm):
    n, d = xf.shape
    row = lambda w: pl.BlockSpec((tm, w), lambda i: (i, 0))
    return pl.pallas_call(
        _out_kernel,
        grid=(n // tm,),
        in_specs=[row(d), row(a.shape[1]), row(b.shape[1]), _const_spec(w_out.shape),
                  _const_spec((1, d))],
        out_specs=row(d),
        out_shape=jax.ShapeDtypeStruct((n, d), F32),
        compiler_params=_cparams(1),
        name="even_outproj",
    )(xf, a, b, w_out, g)


SCAN_PHASES = 8


def _gelu_tanh(x):
    c = math.sqrt(2.0 / math.pi)
    return 0.5 * x * (1.0 + jnp.tanh(c * (x + 0.044715 * (x * x * x))))


def _odd_kernel(x_ref, g0_ref, win_ref, cw_ref, cb_ref, gaw_ref, gab_ref, gxw_ref, gxb_ref,
                ap_ref, wout_ref, g1_ref, o_ref, xpad, hcar, a_scr, u_scr):
    tm, d = x_ref.shape
    bw = d // LRU_BLOCKS

    @pl.when(pl.program_id(1) == 0)
    def _():
        xpad[0:CARRY_ROWS, :] = jnp.zeros((CARRY_ROWS, d), F32)
        hcar[...] = jnp.zeros(hcar.shape, F32)

    x = x_ref[...]
    h = _rms(x, g0_ref[...]).astype(BF16)
    xy = _dot(h, win_ref[...])
    xb = xy[:, :d]
    gate = _gelu_tanh(xy[:, d:])

    xpad[CARRY_ROWS:CARRY_ROWS + tm, :] = xb
    xc = cw_ref[CONV_W - 1:CONV_W, :] * xb + cb_ref[...]
    for j in range(CONV_W - 1):
        xc = xc + cw_ref[j:j + 1, :] * xpad[pl.ds(CARRY_ROWS - (CONV_W - 1) + j, tm), :]
    xpad[0:CARRY_ROWS, :] = xb[tm - CARRY_ROWS:, :]

    xcb = xc.astype(BF16)
    ra = jnp.concatenate([_dot(xcb[:, n * bw:(n + 1) * bw], gaw_ref[n]) for n in range(LRU_BLOCKS)],
                         axis=1)
    ia = jnp.concatenate([_dot(xcb[:, n * bw:(n + 1) * bw], gxw_ref[n]) for n in range(LRU_BLOCKS)],
                         axis=1)
    r = _sigmoid(ra + gab_ref[...])
    ig = _sigmoid(ia + gxb_ref[...])
    log_a = (-LRU_C) * r * _softplus(-ap_ref[...])
    a = jnp.exp(log_a)
    th = jnp.tanh(log_a)
    u = jnp.sqrt(-2.0 * th / (1.0 - th)) * (ig * xc)

    n_slab = d // LANES
    grp = tm // SCAN_PHASES
    for sl in range(n_slab):
        a_scr[sl] = a[:, sl * LANES:(sl + 1) * LANES]
        u_scr[sl] = u[:, sl * LANES:(sl + 1) * LANES]
    pa, pu = [], []
    for sl in range(n_slab):
        ca = a_scr[sl, pl.ds(0, grp, stride=SCAN_PHASES), :]
        cu = u_scr[sl, pl.ds(0, grp, stride=SCAN_PHASES), :]
        la, lu = [ca], [cu]
        for j in range(1, SCAN_PHASES):
            aj = a_scr[sl, pl.ds(j, grp, stride=SCAN_PHASES), :]
            uj = u_scr[sl, pl.ds(j, grp, stride=SCAN_PHASES), :]
            cu = aj * cu + uj
            ca = aj * ca
            la.append(ca)
            lu.append(cu)
        pa.append(la)
        pu.append(lu)
    ga = jnp.concatenate([pa[sl][-1] for sl in range(n_slab)], axis=1)
    gu = jnp.concatenate([pu[sl][-1] for sl in range(n_slab)], axis=1)
    rowg = lax.broadcasted_iota(jnp.int32, (grp, 1), 0)
    s = 1
    while s < grp:
        valid = rowg >= s
        gu = jnp.where(valid, ga * pltpu.roll(gu, s, axis=0) + gu, gu)
        ga = jnp.where(valid, ga * pltpu.roll(ga, s, axis=0), ga)
        s *= 2
    h_in = hcar[...]
    h_end = ga * h_in + gu
    hcar[...] = h_end[grp - 1:grp, :]
    h_prev = jnp.where(rowg == 0, h_in, pltpu.roll(h_end, 1, axis=0))
    for sl in range(n_slab):
        hp = h_prev[:, sl * LANES:(sl + 1) * LANES]
        for j in range(SCAN_PHASES):
            u_scr[sl, pl.ds(j, grp, stride=SCAN_PHASES), :] = pa[sl][j] * hp + pu[sl][j]
    hs = jnp.concatenate([u_scr[sl] for sl in range(n_slab)], axis=1)

    y = _dot((hs * gate).astype(BF16), wout_ref[...])
    o_ref[...] = x + _rms(y, g1_ref[...])


def _odd_mixer(xf, g0, g1, w_in, conv_w, conv_b, gaw, gab, gxw, gxb, a_param, w_out, batch, tm):
    n, d = xf.shape
    t = n // batch
    nt = t // tm
    row = pl.BlockSpec((tm, d), lambda b, i: (b * nt + i, 0))
    vec = _const_spec((1, d))
    return pl.pallas_call(
        _odd_kernel,
        grid=(batch, nt),
        in_specs=[row, vec, _const_spec(w_in.shape), _const_spec(conv_w.shape), vec,
                  _const_spec(gaw.shape), vec, _const_spec(gxw.shape), vec, vec,
                  _const_spec(w_out.shape), vec],
        out_specs=row,
        out_shape=jax.ShapeDtypeStruct((n, d), F32),
        scratch_shapes=[pltpu.VMEM((tm + CARRY_ROWS, d), F32), pltpu.VMEM((1, d), F32),
                        pltpu.VMEM((d // LANES, tm, LANES), F32),
                        pltpu.VMEM((d // LANES, tm, LANES), F32)],
        compiler_params=_cparams(2),
        name="odd_mixer",
    )(xf, g0, w_in, conv_w, conv_b, gaw, gab, gxw, gxb, a_param, w_out, g1)


def _xattn_kernel(x_ref, g2_ref, wq_ref, kv_ref, wo_ref, g3_ref, o_ref):
    tm, d = x_ref.shape
    hd = d // XA_HEADS
    x = x_ref[...]
    h = _rms(x, g2_ref[...]).astype(BF16)
    q = _dot(h, wq_ref[...]).astype(BF16)
    outs = []
    for hh in range(XA_HEADS):
        k = kv_ref[0, :, hh * hd:(hh + 1) * hd]
        v = kv_ref[0, :, d + hh * hd: d + (hh + 1) * hd]
        s = _dot_nt(q[:, hh * hd:(hh + 1) * hd], k) * (hd ** -0.5)
        m = jnp.max(s, axis=-1, keepdims=True)
        p = jnp.exp(s - m)
        p = p / jnp.sum(p, axis=-1, keepdims=True)
        outs.append(_dot(p.astype(BF16), v).astype(BF16))
    o = jnp.concatenate(outs, axis=1)
    y = _dot(o, wo_ref[...])
    o_ref[...] = x + _rms(y, g3_ref[...])


def _xattn(xf, g2, g3, wq, kv_mem, wo, batch, tm):
    n, d = xf.shape
    t = n // batch
    nt = t // tm
    n_mem = kv_mem.shape[0] // batch
    row = pl.BlockSpec((tm, d), lambda b, i: (b * nt + i, 0))
    vec = _const_spec((1, d))
    kv3 = kv_mem.reshape(batch, n_mem, 2 * d)
    return pl.pallas_call(
        _xattn_kernel,
        grid=(batch, nt),
        in_specs=[row, vec, _const_spec(wq.shape),
                  pl.BlockSpec((1, n_mem, 2 * d), lambda b, i: (b, 0, 0)),
                  _const_spec(wo.shape), vec],
        out_specs=row,
        out_shape=jax.ShapeDtypeStruct((n, d), F32),
        compiler_params=_cparams(2),
        name="xattn",
    )(xf, g2, wq, kv3, wo, g3)


FFN_TF = 256


def _ffn_kernel(x_ref, g4_ref, wg_ref, wu_ref, wo_ref, g5_ref, o_ref, h_scr, acc_scr):
    f = pl.program_id(1)

    @pl.when(f == 0)
    def _():
        h_scr[...] = _rms(x_ref[...], g4_ref[...]).astype(BF16)
        acc_scr[...] = jnp.zeros(acc_scr.shape, F32)

    h = h_scr[...]
    gt = _dot(h, wg_ref[...])
    up = _dot(h, wu_ref[...])
    act = (gt * _sigmoid(gt) * up).astype(BF16)
    acc_scr[...] += _dot(act, wo_ref[...])

    @pl.when(f == pl.num_programs(1) - 1)
    def _():
        o_ref[...] = x_ref[...] + _rms(acc_scr[...], g5_ref[...])


def _ffn(xf, g4, g5, w_in, w_out, tm):
    n, d = xf.shape
    dff = w_out.shape[0]
    nf = dff // FFN_TF
    row = pl.BlockSpec((tm, d), lambda i, f: (i, 0))
    vec = _const_spec((1, d))
    return pl.pallas_call(
        _ffn_kernel,
        grid=(n // tm, nf),
        in_specs=[row, vec,
                  pl.BlockSpec((d, FFN_TF), lambda i, f: (0, f)),
                  pl.BlockSpec((d, FFN_TF), lambda i, f: (0, nf + f)),
                  pl.BlockSpec((FFN_TF, d), lambda i, f: (f, 0)),
                  vec],
        out_specs=row,
        out_shape=jax.ShapeDtypeStruct((n, d), F32),
        scratch_shapes=[pltpu.VMEM((tm, d), BF16), pltpu.VMEM((tm, d), F32)],
        compiler_params=_cparams(2),
        name="ffn",
    )(xf, g4, w_in, w_in, w_out, g5)


def _relayout_even(w_in, w_uq, w_ukv, a_log, dt_bias):
    d = w_in.shape[0]
    qkvz = 3 * GDN_HEADS * GDN_DK + GDN_HEADS * GDN_DK
    ab = 2 * GDN_HEADS
    lat = w_in.shape[1] - qkvz - ab - MLA_ROPE
    pad = LANES - MLA_ROPE - ab
    w_in_re = jnp.concatenate(
        [w_in[:, :qkvz], w_in[:, qkvz + ab:qkvz + ab + lat], w_in[:, qkvz + ab + lat:],
         w_in[:, qkvz:qkvz + ab], jnp.zeros((d, pad), w_in.dtype)], axis=1).astype(BF16)
    r = w_uq.shape[0]
    uq = w_uq.reshape(r, MLA_HEADS, MLA_NOPE + MLA_ROPE)
    pe = jnp.pad(uq[:, :, MLA_NOPE:], ((0, 0), (0, 0), (0, LANES - MLA_ROPE)))
    wuq_re = jnp.concatenate([uq[:, :, :MLA_NOPE].reshape(r, -1), pe.reshape(r, -1)],
                             axis=1).astype(BF16)
    ukv = w_ukv.reshape(w_ukv.shape[0], MLA_HEADS, -1)
    wukv_re = jnp.concatenate([ukv[:, :, :MLA_NOPE].reshape(r, -1),
                               ukv[:, :, MLA_NOPE:].reshape(r, -1)], axis=1).astype(BF16)
    alog_row = jnp.zeros((1, LANES), F32).at[0, G_LANE:G_LANE + GDN_HEADS].set(a_log)
    dtb_row = jnp.zeros((1, LANES), F32).at[0, G_LANE:G_LANE + GDN_HEADS].set(dt_bias)
    return w_in_re, wuq_re, wukv_re, alog_row, dtb_row


def kernel(x, mem, positions, norm_gains, mem_norm, e_w_in, e_conv_w, e_a_log, e_dt_bias, e_o_norm,
           e_q_norm, e_kv_norm, e_w_uq, e_w_ukv, e_w_out, o_w_in, o_conv_w, o_conv_b, o_gate_a_w,
           o_gate_a_b, o_gate_x_w, o_gate_x_b, o_a_param, o_w_out, xa_wq, xa_wkv, xa_wo, ffn_w_in,
           ffn_w_out):
    batch, t, d = x.shape
    n = batch * t
    depth = norm_gains.shape[0]
    xf = x.reshape(n, d)
    gains = norm_gains.reshape(depth, -1, 1, d)

    cos_t, sin_t = _rope_tables(positions)
    kv_mem = _mem_kv(mem.reshape(-1, d), mem_norm, xa_wkv.astype(BF16))

    for layer in range(depth):
        g = gains[layer]
        if layer % 2 == 0:
            e = layer // 2
            w_in_re, wuq_re, wukv_re, alog_row, dtb_row = _relayout_even(
                e_w_in[e], e_w_uq[e], e_w_ukv[e], e_a_log[e], e_dt_bias[e])
            q, k, v, z, gb, qf, kf, vb = _even_inproj(
                xf, g[0], w_in_re, e_conv_w[e], alog_row, dtb_row,
                e_q_norm[e].reshape(1, -1), e_kv_norm[e].reshape(1, -1), wuq_re, wukv_re,
                cos_t, sin_t, batch, tm=512)
            out_a = _gdn(q, k, v, z, gb, e_o_norm[e].reshape(1, -1), batch)
            out_b = _mla(qf, kf, vb, batch)
            xf = _out_proj(xf, out_a, out_b, e_w_out[e].astype(BF16), g[1], tm=1024)
        else:
            o = layer // 2
            xf = _odd_mixer(xf, g[0], g[1], o_w_in[o].astype(BF16), o_conv_w[o],
                            o_conv_b[o].reshape(1, d), o_gate_a_w[o].astype(BF16),
                            o_gate_a_b[o].reshape(1, d), o_gate_x_w[o].astype(BF16),
                            o_gate_x_b[o].reshape(1, d), o_a_param[o].reshape(1, d),
                            o_w_out[o].astype(BF16), batch, tm=512)
        xf = _xattn(xf, g[2], g[3], xa_wq[layer].astype(BF16), kv_mem[layer],
                    xa_wo[layer].astype(BF16), batch, tm=512)
        xf = _ffn(xf, g[4], g[5], ffn_w_in[layer].astype(BF16), ffn_w_out[layer].astype(BF16),
                  tm=1024)
    return xf.reshape(batch, t, d)
```

```python
import functools
import math

import jax
import jax.numpy as jnp
from jax import lax
from jax.experimental import pallas as pl
from jax.experimental.pallas import tpu as pltpu

F32 = jnp.float32
BF16 = jnp.bfloat16

EPS = 1e-6
CHUNK = 64
GDN_HEADS = 4
GDN_DK = 128
MLA_HEADS = 4
MLA_NOPE = 128
MLA_ROPE = 64
MLA_SCALE = (MLA_NOPE + MLA_ROPE) ** -0.5
ROPE_BASE = 10000.0
LRU_BLOCKS = 4
LRU_C = 8.0
XA_HEADS = 4
CONV_W = 4

LANES = 128
CARRY_ROWS = 8
V7X_VMEM_LIMIT = 56 * 1024 * 1024

NT_DIMS = (((1,), (1,)), ((), ()))
TN_DIMS = (((0,), (0,)), ((), ()))


def _cparams(n_axes):
    return pltpu.CompilerParams(
        dimension_semantics=("arbitrary",) * n_axes,
        vmem_limit_bytes=V7X_VMEM_LIMIT)


def _dot(a, b):
    return jnp.dot(a, b, preferred_element_type=F32)


def _dot_nt(a, b):
    return lax.dot_general(a, b, NT_DIMS, preferred_element_type=F32)


def _rms(xf, g):
    ms = jnp.mean(xf * xf, axis=-1, keepdims=True)
    return xf * lax.rsqrt(ms + EPS) * g


def _sigmoid(x):
    return 1.0 / (1.0 + jnp.exp(-x))


def _softplus(x):
    return jnp.maximum(x, 0.0) + jnp.log1p(jnp.exp(-jnp.abs(x)))


def _const_spec(shape):
    nd = len(shape)
    return pl.BlockSpec(shape, lambda *_: (0,) * nd)


def _layer_spec(shape, layer):
    nd = len(shape) - 1
    return pl.BlockSpec((None,) + tuple(shape[1:]), lambda *_: (layer,) + (0,) * nd,
                        pipeline_mode=pl.Buffered(1))


def _rope_kernel(pos_ref, inv_ref, cos_ref, sin_ref):
    ang = pos_ref[...].astype(F32) * inv_ref[...]
    cos_ref[...] = jnp.cos(ang)
    sin_ref[...] = jnp.sin(ang)


def _rope_tables(positions):
    n = positions.size
    half = MLA_ROPE // 2
    per_row = LANES // half
    inv_freq = ROPE_BASE ** (-jnp.arange(0, MLA_ROPE, 2, dtype=F32) / MLA_ROPE)
    inv_row = jnp.tile(inv_freq, per_row).reshape(1, LANES)
    pos_rep = jnp.repeat(positions.reshape(n // per_row, per_row), half, axis=1)
    rows = n // per_row
    tr = min(rows, 1024)
    spec = pl.BlockSpec((tr, LANES), lambda i: (i, 0))
    cos_p, sin_p = pl.pallas_call(
        _rope_kernel,
        grid=(rows // tr,),
        in_specs=[spec, _const_spec((1, LANES))],
        out_specs=[spec, spec],
        out_shape=[jax.ShapeDtypeStruct((rows, LANES), F32)] * 2,
        compiler_params=_cparams(1),
        name="rope_tables",
    )(pos_rep, inv_row)
    cos_t = jnp.tile(cos_p.reshape(n, half), (1, per_row))
    sin_t = jnp.tile(sin_p.reshape(n, half), (1, per_row))
    return cos_t, sin_t


def _memkv_kernel(mem_ref, g_ref, w_ref, o_ref):
    mn = _rms(mem_ref[...], g_ref[...]).astype(BF16)
    o_ref[0] = _dot(mn, w_ref[0]).astype(BF16)


def _mem_kv(mem2d, mem_norm, wkv_bf16):
    depth, d, d2 = wkv_bf16.shape
    rows = mem2d.shape[0]
    tr = min(rows, 512)
    return pl.pallas_call(
        _memkv_kernel,
        grid=(depth, rows // tr),
        in_specs=[pl.BlockSpec((tr, d), lambda l, i: (i, 0)),
                  _const_spec((1, d)),
                  pl.BlockSpec((1, d, d2), lambda l, i: (l, 0, 0))],
        out_specs=pl.BlockSpec((1, tr, d2), lambda l, i: (l, i, 0)),
        out_shape=jax.ShapeDtypeStruct((depth, rows, d2), BF16),
        compiler_params=_cparams(2),
        name="mem_kv",
    )(mem2d, mem_norm.reshape(1, d), wkv_bf16)


def _rope_tile(x, cos, sin, lane):
    half = MLA_ROPE // 2
    up = pltpu.roll(x, LANES - half, axis=1)
    dn = pltpu.roll(x, half, axis=1)
    rot = jnp.where(lane < half, -up, dn)
    return jnp.where(lane < MLA_ROPE, x * cos + rot * sin, 0.0)


def _even_in_kernel(x_ref, g_ref, win_ref, cw_ref, alog_ref, dtb_ref, qn_ref, kvn_ref,
                    wuq_ref, wukv_ref, cos_ref, sin_ref,
                    q_out, k_out, v_out, z_out, gb_out, qf_out, kf_out, vb_out,
                    xpad):
    tm = x_ref.shape[0]
    qk_w = GDN_HEADS * GDN_DK
    qkv_w = 3 * qk_w

    @pl.when(pl.program_id(1) == 0)
    def _():
        xpad[0:CARRY_ROWS, :] = jnp.zeros((CARRY_ROWS, qkv_w), F32)

    h = _rms(x_ref[...], g_ref[...]).astype(BF16)
    proj = _dot(h, win_ref[...])

    qkv = proj[:, :qkv_w]
    xpad[CARRY_ROWS:CARRY_ROWS + tm, :] = qkv
    acc = cw_ref[CONV_W - 1:CONV_W, :] * qkv
    for j in range(CONV_W - 1):
        acc = acc + cw_ref[j:j + 1, :] * xpad[pl.ds(CARRY_ROWS - (CONV_W - 1) + j, tm), :]
    xpad[0:CARRY_ROWS, :] = qkv[tm - CARRY_ROWS:, :]
    act = acc * _sigmoid(acc)
    for hh in range(GDN_HEADS):
        sl = slice(hh * GDN_DK, (hh + 1) * GDN_DK)
        qh = act[:, sl]
        kh = act[:, qk_w + hh * GDN_DK: qk_w + (hh + 1) * GDN_DK]
        qn = qh * lax.rsqrt(jnp.sum(qh * qh, axis=-1, keepdims=True) + EPS) * (GDN_DK ** -0.5)
        kn = kh * lax.rsqrt(jnp.sum(kh * kh, axis=-1, keepdims=True) + EPS)
        q_out[:, sl] = qn.astype(BF16)
        k_out[:, sl] = kn.astype(BF16)
    v_out[...] = act[:, 2 * qk_w:].astype(BF16)
    z_out[...] = proj[:, qkv_w:qkv_w + qk_w].astype(BF16)

    misc = proj[:, 2560:2688]
    lane = lax.broadcasted_iota(jnp.int32, misc.shape, 1)
    gdec = -jnp.exp(alog_ref[...]) * _softplus(misc + dtb_ref[...])
    beta = _sigmoid(misc)
    a0 = MLA_ROPE
    gb_out[...] = jnp.where((lane >= a0) & (lane < a0 + GDN_HEADS), gdec,
                            jnp.where((lane >= a0 + GDN_HEADS) & (lane < a0 + 2 * GDN_HEADS),
                                      beta, 0.0))
    cos = cos_ref[...]
    sin = sin_ref[...]
    kpe = _rope_tile(misc, cos, sin, lane).astype(BF16)

    cq = _rms(proj[:, 2048:2304], qn_ref[...]).astype(BF16)
    qf = _dot(cq, wuq_ref[...])
    ckv = _rms(proj[:, 2304:2560], kvn_ref[...]).astype(BF16)
    kv = _dot(ckv, wukv_ref[...])
    nope_w = MLA_HEADS * MLA_NOPE
    for hh in range(MLA_HEADS):
        base = 2 * LANES * hh
        qf_out[:, base:base + LANES] = (qf[:, hh * LANES:(hh + 1) * LANES] * MLA_SCALE).astype(BF16)
        pe = qf[:, nope_w + hh * LANES: nope_w + (hh + 1) * LANES]
        qf_out[:, base + LANES:base + 2 * LANES] = (
            _rope_tile(pe, cos, sin, lane) * MLA_SCALE).astype(BF16)
        kf_out[:, base:base + LANES] = kv[:, hh * LANES:(hh + 1) * LANES].astype(BF16)
        kf_out[:, base + LANES:base + 2 * LANES] = kpe
    vb_out[...] = kv[:, nope_w:].astype(BF16)


def _even_inproj(xf, g0, w_in_re, conv_w, alog_row, dtb_row, q_norm, kv_norm, wuq_re, wukv_re,
                 cos_t, sin_t, batch, tm):
    n, d = xf.shape
    t = n // batch
    nt = t // tm
    e_in = w_in_re.shape[1]
    qkv_w = conv_w.shape[1]
    qk_w = qkv_w // 3
    row = lambda w: pl.BlockSpec((tm, w), lambda b, i: (b * nt + i, 0))
    outs = [(qk_w, BF16)] * 4 + [(LANES, F32), (2 * LANES * MLA_HEADS, BF16),
                                 (2 * LANES * MLA_HEADS, BF16), (LANES * MLA_HEADS, BF16)]
    return pl.pallas_call(
        _even_in_kernel,
        grid=(batch, nt),
        in_specs=[row(d), _const_spec((1, d)), _const_spec((d, e_in)), _const_spec((CONV_W, qkv_w)),
                  _const_spec((1, LANES)), _const_spec((1, LANES)),
                  _const_spec(q_norm.shape), _const_spec(kv_norm.shape),
                  _const_spec(wuq_re.shape), _const_spec(wukv_re.shape),
                  row(LANES), row(LANES)],
        out_specs=[row(w) for w, _ in outs],
        out_shape=[jax.ShapeDtypeStruct((n, w), dt) for w, dt in outs],
        scratch_shapes=[pltpu.VMEM((tm + CARRY_ROWS, qkv_w), F32)],
        compiler_params=_cparams(2),
        name="even_inproj",
    )(xf, g0, w_in_re, conv_w, alog_row, dtb_row, q_norm, kv_norm, wuq_re, wukv_re, cos_t, sin_t)


SUPER = 2 * CHUNK
G_LANE = MLA_ROPE
B_LANE = MLA_ROPE + GDN_HEADS
SOLVE_UNROLL = 2


def _gdn_kernel(q_ref, k_ref, v_ref, z_ref, gb_ref, on_ref, out_ref,
                gc_scr, gct_scr, gl_scr, u_scr, w_scr, a_scr, qe_scr, kd_scr, o_scr, s_scr):
    t = q_ref.shape[0]
    n_super = t // SUPER
    n_chunk = t // CHUNK

    gb = gb_ref[...]
    rowi = lax.broadcasted_iota(jnp.int32, gb.shape, 0) & (CHUNK - 1)
    gc = gb
    s = 1
    while s < CHUNK:
        gc = jnp.where(rowi >= s, gc + pltpu.roll(gc, s, axis=0), gc)
        s *= 2
    gc_scr[...] = gc
    gct_scr[...] = gc.T
    g3 = gc.reshape(n_chunk, CHUNK, LANES)
    gl_scr[...] = jnp.broadcast_to(g3[:, CHUNK - 1:CHUNK, :], g3.shape).reshape(t, LANES)

    ri = lax.broadcasted_iota(jnp.int32, (SUPER, SUPER), 0)
    ci = lax.broadcasted_iota(jnp.int32, (SUPER, SUPER), 1)
    same = (ri >= CHUNK) == (ci >= CHUNK)
    causal = same & (ri >= ci)
    strict = same & (ri > ci)
    eye = (ri == ci).astype(F32)

    heads = range(GDN_HEADS)
    hsl = [slice(hh * GDN_DK, (hh + 1) * GDN_DK) for hh in heads]
    lane_g = [slice(G_LANE + hh, G_LANE + hh + 1) for hh in heads]
    lane_b = [slice(B_LANE + hh, B_LANE + hh + 1) for hh in heads]

    def solve_body(it, _):
        probs = []
        for sub in range(SOLVE_UNROLL):
            r0 = pl.multiple_of((it * SOLVE_UNROLL + sub) * SUPER, SUPER)
            probs += [(pl.ds(r0, SUPER), hh) for hh in heads]
        kc = [k_ref[rows, hsl[hh]] for rows, hh in probs]
        qc = [q_ref[rows, hsl[hh]] for rows, hh in probs]
        gcol = [gc_scr[rows, lane_g[hh]] for rows, hh in probs]
        bcol = [gb_ref[rows, lane_b[hh]] for rows, hh in probs]
        grow = [gct_scr[lane_g[hh], rows] for rows, hh in probs]
        kk = [_dot_nt(k, k) for k in kc]
        qk = [_dot_nt(q, k) for q, k in zip(qc, kc)]
        decay = [jnp.where(causal, jnp.exp(jnp.where(causal, gc - gr, 0.0)), 0.0)
                 for gc, gr in zip(gcol, grow)]
        m = [jnp.where(strict, -(x * b * dc), 0.0) for x, b, dc in zip(kk, bcol, decay)]
        qacc = [eye + x for x in m]
        mb = [x.astype(BF16) for x in m]
        mj = [_dot(x, x) for x in mb]
        lvl = 2
        while lvl < CHUNK // 2:
            mb = [x.astype(BF16) for x in mj]
            r = [_dot(x, jnp.concatenate([x, qa.astype(BF16)], axis=1)) for x, qa in zip(mb, qacc)]
            mj = [x[:, :SUPER] for x in r]
            qacc = [qa + x[:, SUPER:] for qa, x in zip(qacc, r)]
            lvl *= 2
        qacc = [qa + _dot(x.astype(BF16), qa.astype(BF16)) for qa, x in zip(qacc, mj)]
        eg = [jnp.exp(gc) for gc in gcol]
        sol = []
        for i, (rows, hh) in enumerate(probs):
            vc = v_ref[rows, hsl[hh]].astype(F32)
            kf = kc[i].astype(F32)
            rhs = jnp.concatenate([vc * bcol[i], kf * (bcol[i] * eg[i])], axis=1).astype(BF16)
            sol.append(_dot(qacc[i].astype(BF16), rhs))
        for i, (rows, hh) in enumerate(probs):
            u_scr[rows, hsl[hh]] = sol[i][:, :GDN_DK]
            w_scr[rows, hsl[hh]] = sol[i][:, GDN_DK:].astype(BF16)
            aqk = qk[i] * decay[i]
            a_sh = pltpu.roll(aqk, CHUNK, axis=1)
            a_scr[rows, hsl[hh]] = jnp.where(ri < CHUNK, aqk, a_sh).astype(BF16)
            qe_scr[rows, hsl[hh]] = (qc[i].astype(F32) * eg[i]).astype(BF16)
            glast = gl_scr[rows, lane_g[hh]]
            kd_scr[rows, hsl[hh]] = (kc[i].astype(F32) * jnp.exp(glast - gcol[i])).astype(BF16)
        return 0

    lax.fori_loop(0, n_super // SOLVE_UNROLL, solve_body, 0)

    s_scr[...] = jnp.zeros(s_scr.shape, F32)

    def scan_body(c, _):
        r0 = pl.multiple_of(c * CHUNK, CHUNK)
        rows = pl.ds(r0, CHUNK)
        st = [s_scr[hh] for hh in heads]
        sb = [x.astype(BF16) for x in st]
        t1 = [_dot(w_scr[rows, hsl[hh]], sb[hh]) for hh in heads]
        oq = [_dot(qe_scr[rows, hsl[hh]], sb[hh]) for hh in heads]
        vb = [(u_scr[rows, hsl[hh]] - t1[hh]).astype(BF16) for hh in heads]
        oa = [_dot(a_scr[rows, hsl[hh]][:, :CHUNK], vb[hh]) for hh in heads]
        kv = [lax.dot_general(kd_scr[rows, hsl[hh]], vb[hh], TN_DIMS, preferred_element_type=F32)
              for hh in heads]
        for hh in heads:
            gam = jnp.exp(gl_scr[pl.ds(r0, 1), lane_g[hh]])
            o_scr[rows, hsl[hh]] = oq[hh] + oa[hh]
            s_scr[hh] = st[hh] * gam + kv[hh]
        return 0

    lax.fori_loop(0, n_chunk, scan_body, 0)

    on = on_ref[...]
    for hh in range(GDN_HEADS):
        hs = slice(hh * GDN_DK, (hh + 1) * GDN_DK)
        z = z_ref[:, hs].astype(F32)
        out_ref[:, hs] = (_rms(o_scr[:, hs], on) * (z * _sigmoid(z))).astype(BF16)


def _gdn(q, k, v, z, gb, o_norm, batch):
    n, w = q.shape
    t = n // batch
    row = lambda ww: pl.BlockSpec((t, ww), lambda b: (b, 0))
    return pl.pallas_call(
        _gdn_kernel,
        grid=(batch,),
        in_specs=[row(w), row(w), row(w), row(w), row(LANES), _const_spec((1, GDN_DK))],
        out_specs=row(w),
        out_shape=jax.ShapeDtypeStruct((n, w), BF16),
        scratch_shapes=[pltpu.VMEM((t, LANES), F32),
                        pltpu.VMEM((LANES, t), F32),
                        pltpu.VMEM((t, LANES), F32),
                        pltpu.VMEM((t, w), F32),
                        pltpu.VMEM((t, w), BF16),
                        pltpu.VMEM((t, w), BF16),
                        pltpu.VMEM((t, w), BF16),
                        pltpu.VMEM((t, w), BF16),
                        pltpu.VMEM((t, w), F32),
                        pltpu.VMEM((GDN_HEADS, GDN_DK, GDN_DK), F32)],
        compiler_params=_cparams(1),
        name="gdn",
    )(q, k, v, z, gb, o_norm)


MLA_TQ = 256


def _mla_kernel(q_ref, k_ref, v_ref, o_ref):
    t = q_ref.shape[0]
    tq = MLA_TQ
    nq = t // tq
    hw = 2 * LANES
    heads = range(MLA_HEADS)
    ri = lax.broadcasted_iota(jnp.int32, (tq, tq), 0) // CHUNK
    ci = lax.broadcasted_iota(jnp.int32, (tq, tq), 1) // CHUNK
    diag_mask = ci <= ri

    for i in range(nq):
        qrows = slice(i * tq, (i + 1) * tq)
        kw = (i + 1) * tq
        for h in heads:
            s = _dot_nt(q_ref[qrows, h * hw:(h + 1) * hw], k_ref[0:kw, h * hw:(h + 1) * hw])
            s_diag = jnp.where(diag_mask, s[:, kw - tq:], -jnp.inf)
            s = s_diag if i == 0 else jnp.concatenate([s[:, :kw - tq], s_diag], axis=1)
            m = jnp.max(s, axis=-1, keepdims=True)
            p = jnp.exp(s - m)
            l = jnp.sum(p, axis=-1, keepdims=True)
            pv = _dot(p.astype(BF16), v_ref[0:kw, h * LANES:(h + 1) * LANES])
            o_ref[qrows, h * LANES:(h + 1) * LANES] = (pv / l).astype(BF16)


def _mla(qf, kf, vb, batch):
    n = qf.shape[0]
    t = n // batch
    row = lambda w: pl.BlockSpec((t, w), lambda b: (b, 0))
    return pl.pallas_call(
        _mla_kernel,
        grid=(batch,),
        in_specs=[row(qf.shape[1]), row(kf.shape[1]), row(vb.shape[1])],
        out_specs=row(vb.shape[1]),
        out_shape=jax.ShapeDtypeStruct((n, vb.shape[1]), BF16),
        compiler_params=_cparams(1),
        name="mla_attn",
    )(qf, kf, vb)


def _out_kernel(x_ref, a_ref, b_ref, w_ref, g_ref, o_ref):
    ka = a_ref.shape[1]
    y = _dot(a_ref[...], w_ref[0:ka, :]) + _dot(b_ref[...], w_ref[ka:, :])
    o_ref[...] = x_ref[...] + _rms(y, g_ref[...])


def _out_proj(xf, a, b, w_out, g, tm):
    n, d = xf.shape
    row = lambda w: pl.BlockSpec((tm, w), lambda i: (i, 0))
    return pl.pallas_call(
        _out_kernel,
        grid=(n // tm,),
        in_specs=[row(d), row(a.shape[1]), row(b.shape[1]), _const_spec(w_out.shape),
                  _const_spec((1, d))],
        out_specs=row(d),
        out_shape=jax.ShapeDtypeStruct((n, d), F32),
        compiler_params=_cparams(1),
        name="even_outproj",
    )(xf, a, b, w_out, g)


SCAN_PHASES = 8


def _gelu_tanh(x):
    c = math.sqrt(2.0 / math.pi)
    return 0.5 * x * (1.0 + jnp.tanh(c * (x + 0.044715 * (x * x * x))))


def _odd_kernel(x_ref, g0_ref, win_ref, cw_ref, cb_ref, gaw_ref, gab_ref, gxw_ref, gxb_ref,
                ap_ref, wout_ref, g1_ref, o_ref, xpad, hcar, a_scr, u_scr):
    tm, d = x_ref.shape
    bw = d // LRU_BLOCKS

    @pl.when(pl.program_id(1) == 0)
    def _():
        xpad[0:CARRY_ROWS, :] = jnp.zeros((CARRY_ROWS, d), F32)
        hcar[...] = jnp.zeros(hcar.shape, F32)

    x = x_ref[...]
    h = _rms(x, g0_ref[...]).astype(BF16)
    xy = _dot(h, win_ref[...])
    xb = xy[:, :d]
    gate = _gelu_tanh(xy[:, d:])

    xpad[CARRY_ROWS:CARRY_ROWS + tm, :] = xb
    xc = cw_ref[CONV_W - 1:CONV_W, :] * xb + cb_ref[...]
    for j in range(CONV_W - 1):
        xc = xc + cw_ref[j:j + 1, :] * xpad[pl.ds(CARRY_ROWS - (CONV_W - 1) + j, tm), :]
    xpad[0:CARRY_ROWS, :] = xb[tm - CARRY_ROWS:, :]

    xcb = xc.astype(BF16)
    ra = jnp.concatenate([_dot(xcb[:, n * bw:(n + 1) * bw], gaw_ref[n]) for n in range(LRU_BLOCKS)],
                         axis=1)
    ia = jnp.concatenate([_dot(xcb[:, n * bw:(n + 1) * bw], gxw_ref[n]) for n in range(LRU_BLOCKS)],
                         axis=1)
    r = _sigmoid(ra + gab_ref[...])
    ig = _sigmoid(ia + gxb_ref[...])
    log_a = (-LRU_C) * r * _softplus(-ap_ref[...])
    a = jnp.exp(log_a)
    th = jnp.tanh(log_a)
    u = jnp.sqrt(-2.0 * th / (1.0 - th)) * (ig * xc)

    n_slab = d // LANES
    grp = tm // SCAN_PHASES
    for sl in range(n_slab):
        a_scr[sl] = a[:, sl * LANES:(sl + 1) * LANES]
        u_scr[sl] = u[:, sl * LANES:(sl + 1) * LANES]
    pa, pu = [], []
    for sl in range(n_slab):
        ca = a_scr[sl, pl.ds(0, grp, stride=SCAN_PHASES), :]
        cu = u_scr[sl, pl.ds(0, grp, stride=SCAN_PHASES), :]
        la, lu = [ca], [cu]
        for j in range(1, SCAN_PHASES):
            aj = a_scr[sl, pl.ds(j, grp, stride=SCAN_PHASES), :]
            uj = u_scr[sl, pl.ds(j, grp, stride=SCAN_PHASES), :]
            cu = aj * cu + uj
            ca = aj * ca
            la.append(ca)
            lu.append(cu)
        pa.append(la)
        pu.append(lu)
    ga = jnp.concatenate([pa[sl][-1] for sl in range(n_slab)], axis=1)
    gu = jnp.concatenate([pu[sl][-1] for sl in range(n_slab)], axis=1)
    rowg = lax.broadcasted_iota(jnp.int32, (grp, 1), 0)
    s = 1
    while s < grp:
        valid = rowg >= s
        gu = jnp.where(valid, ga * pltpu.roll(gu, s, axis=0) + gu, gu)
        ga = jnp.where(valid, ga * pltpu.roll(ga, s, axis=0), ga)
        s *= 2
    h_in = hcar[...]
    h_end = ga * h_in + gu
    hcar[...] = h_end[grp - 1:grp, :]
    h_prev = jnp.where(rowg == 0, h_in, pltpu.roll(h_end, 1, axis=0))
    for sl in range(n_slab):
        hp = h_prev[:, sl * LANES:(sl + 1) * LANES]
        for j in range(SCAN_PHASES):
            u_scr[sl, pl.ds(j, grp, stride=SCAN_PHASES), :] = pa[sl][j] * hp + pu[sl][j]
    hs = jnp.concatenate([u_scr[sl] for sl in range(n_slab)], axis=1)

    y = _dot((hs * gate).astype(BF16), wout_ref[...])
    o_ref[...] = x + _rms(y, g1_ref[...])


def _odd_mixer(xf, g0, g1, w_in, conv_w, conv_b, gaw, gab, gxw, gxb, a_param, w_out, batch, tm):
    n, d = xf.shape
    t = n // batch
    nt = t // tm
    row = pl.BlockSpec((tm, d), lambda b, i: (b * nt + i, 0))
    vec = _const_spec((1, d))
    return pl.pallas_call(
        _odd_kernel,
        grid=(batch, nt),
        in_specs=[row, vec, _const_spec(w_in.shape), _const_spec(conv_w.shape), vec,
                  _const_spec(gaw.shape), vec, _const_spec(gxw.shape), vec, vec,
                  _const_spec(w_out.shape), vec],
        out_specs=row,
        out_shape=jax.ShapeDtypeStruct((n, d), F32),
        scratch_shapes=[pltpu.VMEM((tm + CARRY_ROWS, d), F32), pltpu.VMEM((1, d), F32),
                        pltpu.VMEM((d // LANES, tm, LANES), F32),
                        pltpu.VMEM((d // LANES, tm, LANES), F32)],
        compiler_params=_cparams(2),
        name="odd_mixer",
    )(xf, g0, w_in, conv_w, conv_b, gaw, gab, gxw, gxb, a_param, w_out, g1)


def _xattn_kernel(x_ref, g2_ref, wq_ref, kv_ref, wo_ref, g3_ref, o_ref):
    tm, d = x_ref.shape
    hd = d // XA_HEADS
    x = x_ref[...]
    h = _rms(x, g2_ref[...]).astype(BF16)
    q = _dot(h, wq_ref[...]).astype(BF16)
    outs = []
    for hh in range(XA_HEADS):
        k = kv_ref[:, hh * hd:(hh + 1) * hd]
        v = kv_ref[:, d + hh * hd: d + (hh + 1) * hd]
        s = _dot_nt(q[:, hh * hd:(hh + 1) * hd], k) * (hd ** -0.5)
        m = jnp.max(s, axis=-1, keepdims=True)
        p = jnp.exp(s - m)
        p = p / jnp.sum(p, axis=-1, keepdims=True)
        outs.append(_dot(p.astype(BF16), v).astype(BF16))
    o = jnp.concatenate(outs, axis=1)
    y = _dot(o, wo_ref[...])
    o_ref[...] = x + _rms(y, g3_ref[...])


def _xattn(xf, g2, g3, wq, kv_mem, wo, layer, batch, tm):
    n, d = xf.shape
    t = n // batch
    nt = t // tm
    n_mem = kv_mem.shape[1] // batch
    row = pl.BlockSpec((tm, d), lambda b, i: (b * nt + i, 0))
    vec = _const_spec((1, d))
    return pl.pallas_call(
        _xattn_kernel,
        grid=(batch, nt),
        in_specs=[row, vec, _layer_spec(wq.shape, layer),
                  pl.BlockSpec((None, n_mem, 2 * d), lambda b, i: (layer, b, 0)),
                  _layer_spec(wo.shape, layer), vec],
        out_specs=row,
        out_shape=jax.ShapeDtypeStruct((n, d), F32),
        compiler_params=_cparams(2),
        name="xattn",
    )(xf, g2, wq, kv_mem, wo, g3)


FFN_TF = 256


def _ffn_kernel(x_ref, g4_ref, win_ref, wo_ref, g5_ref, o_ref, act_scr):
    dff = wo_ref.shape[0]
    x = x_ref[...]
    h = _rms(x, g4_ref[...]).astype(BF16)
    for c in range(dff // FFN_TF):
        cols = slice(c * FFN_TF, (c + 1) * FFN_TF)
        gt = _dot(h, win_ref[:, cols])
        up = _dot(h, win_ref[:, dff + c * FFN_TF: dff + (c + 1) * FFN_TF])
        act_scr[:, cols] = (gt * _sigmoid(gt) * up).astype(BF16)
    y = _dot(act_scr[...], wo_ref[...])
    o_ref[...] = x + _rms(y, g5_ref[...])


def _ffn(xf, g4, g5, w_in, w_out, layer, tm):
    n, d = xf.shape
    dff = w_out.shape[1]
    row = pl.BlockSpec((tm, d), lambda i: (i, 0))
    vec = _const_spec((1, d))
    return pl.pallas_call(
        _ffn_kernel,
        grid=(n // tm,),
        in_specs=[row, vec, _layer_spec(w_in.shape, layer), _layer_spec(w_out.shape, layer), vec],
        out_specs=row,
        out_shape=jax.ShapeDtypeStruct((n, d), F32),
        scratch_shapes=[pltpu.VMEM((tm, dff), BF16)],
        compiler_params=_cparams(1),
        name="ffn",
    )(xf, g4, w_in, w_out, g5)


def _relayout_even(w_in, w_uq, w_ukv, a_log, dt_bias):
    d = w_in.shape[0]
    qkvz = 3 * GDN_HEADS * GDN_DK + GDN_HEADS * GDN_DK
    ab = 2 * GDN_HEADS
    lat = w_in.shape[1] - qkvz - ab - MLA_ROPE
    pad = LANES - MLA_ROPE - ab
    w_in_re = jnp.concatenate(
        [w_in[:, :qkvz], w_in[:, qkvz + ab:qkvz + ab + lat], w_in[:, qkvz + ab + lat:],
         w_in[:, qkvz:qkvz + ab], jnp.zeros((d, pad), w_in.dtype)], axis=1).astype(BF16)
    r = w_uq.shape[0]
    uq = w_uq.reshape(r, MLA_HEADS, MLA_NOPE + MLA_ROPE)
    pe = jnp.pad(uq[:, :, MLA_NOPE:], ((0, 0), (0, 0), (0, LANES - MLA_ROPE)))
    wuq_re = jnp.concatenate([uq[:, :, :MLA_NOPE].reshape(r, -1), pe.reshape(r, -1)],
                             axis=1).astype(BF16)
    ukv = w_ukv.reshape(w_ukv.shape[0], MLA_HEADS, -1)
    wukv_re = jnp.concatenate([ukv[:, :, :MLA_NOPE].reshape(r, -1),
                               ukv[:, :, MLA_NOPE:].reshape(r, -1)], axis=1).astype(BF16)
    alog_row = jnp.zeros((1, LANES), F32).at[0, G_LANE:G_LANE + GDN_HEADS].set(a_log)
    dtb_row = jnp.zeros((1, LANES), F32).at[0, G_LANE:G_LANE + GDN_HEADS].set(dt_bias)
    return w_in_re, wuq_re, wukv_re, alog_row, dtb_row


def kernel(x, mem, positions, norm_gains, mem_norm, e_w_in, e_conv_w, e_a_log, e_dt_bias, e_o_norm,
           e_q_norm, e_kv_norm, e_w_uq, e_w_ukv, e_w_out, o_w_in, o_conv_w, o_conv_b, o_gate_a_w,
           o_gate_a_b, o_gate_x_w, o_gate_x_b, o_a_param, o_w_out, xa_wq, xa_wkv, xa_wo, ffn_w_in,
           ffn_w_out):
    batch, t, d = x.shape
    n = batch * t
    depth = norm_gains.shape[0]
    xf = x.reshape(n, d)
    gains = norm_gains.reshape(depth, -1, 1, d)

    cos_t, sin_t = _rope_tables(positions)
    kv_mem = _mem_kv(mem.reshape(-1, d), mem_norm, xa_wkv.astype(BF16))
    xa_wq_b, xa_wo_b = xa_wq.astype(BF16), xa_wo.astype(BF16)
    ffn_w_in_b, ffn_w_out_b = ffn_w_in.astype(BF16), ffn_w_out.astype(BF16)

    for layer in range(depth):
        g = gains[layer]
        if layer % 2 == 0:
            e = layer // 2
            w_in_re, wuq_re, wukv_re, alog_row, dtb_row = _relayout_even(
                e_w_in[e], e_w_uq[e], e_w_ukv[e], e_a_log[e], e_dt_bias[e])
            q, k, v, z, gb, qf, kf, vb = _even_inproj(
                xf, g[0], w_in_re, e_conv_w[e], alog_row, dtb_row,
                e_q_norm[e].reshape(1, -1), e_kv_norm[e].reshape(1, -1), wuq_re, wukv_re,
                cos_t, sin_t, batch, tm=512)
            out_a = _gdn(q, k, v, z, gb, e_o_norm[e].reshape(1, -1), batch)
            out_b = _mla(qf, kf, vb, batch)
            xf = _out_proj(xf, out_a, out_b, e_w_out[e].astype(BF16), g[1], tm=1024)
        else:
            o = layer // 2
            xf = _odd_mixer(xf, g[0], g[1], o_w_in[o].astype(BF16), o_conv_w[o],
                            o_conv_b[o].reshape(1, d), o_gate_a_w[o].astype(BF16),
                            o_gate_a_b[o].reshape(1, d), o_gate_x_w[o].astype(BF16),
                            o_gate_x_b[o].reshape(1, d), o_a_param[o].reshape(1, d),
                            o_w_out[o].astype(BF16), batch, tm=512)
        xf = _xattn(xf, g[2], g[3], xa_wq_b, kv_mem, xa_wo_b, layer, batch, tm=512)
        xf = _ffn(xf, g[4], g[5], ffn_w_in_b, ffn_w_out_b, layer, tm=512)
    return xf.reshape(batch, t, d)
```

```python
import functools
import math

import jax
import jax.numpy as jnp
from jax import lax
from jax.experimental import pallas as pl
from jax.experimental.pallas import tpu as pltpu

F32 = jnp.float32
BF16 = jnp.bfloat16

EPS = 1e-6
CHUNK = 64
GDN_HEADS = 4
GDN_DK = 128
MLA_HEADS = 4
MLA_NOPE = 128
MLA_ROPE = 64
MLA_SCALE = (MLA_NOPE + MLA_ROPE) ** -0.5
LOG2E = math.log2(math.e)
ROPE_BASE = 10000.0
LRU_BLOCKS = 4
LRU_C = 8.0
XA_HEADS = 4
CONV_W = 4

LANES = 128
CARRY_ROWS = 8
V7X_VMEM_LIMIT = 56 * 1024 * 1024

NT_DIMS = (((1,), (1,)), ((), ()))
TN_DIMS = (((0,), (0,)), ((), ()))


def _cparams(n_axes):
    return pltpu.CompilerParams(
        dimension_semantics=("arbitrary",) * n_axes,
        vmem_limit_bytes=V7X_VMEM_LIMIT)


def _dot(a, b):
    return jnp.dot(a, b, preferred_element_type=F32)


def _dot_nt(a, b):
    return lax.dot_general(a, b, NT_DIMS, preferred_element_type=F32)


def _rms(xf, g):
    ms = jnp.mean(xf * xf, axis=-1, keepdims=True)
    return xf * lax.rsqrt(ms + EPS) * g


def _sigmoid(x):
    return 1.0 / (1.0 + jnp.exp(-x))


def _softplus(x):
    return jnp.maximum(x, 0.0) + jnp.log1p(jnp.exp(-jnp.abs(x)))


def _const_spec(shape):
    nd = len(shape)
    return pl.BlockSpec(shape, lambda *_: (0,) * nd)


def _layer_spec(shape, layer):
    nd = len(shape) - 1
    return pl.BlockSpec((None,) + tuple(shape[1:]), lambda *_: (layer,) + (0,) * nd,
                        pipeline_mode=pl.Buffered(1))


def _rope_kernel(pos_ref, inv_ref, cos_ref, sin_ref):
    ang = pos_ref[...].astype(F32) * inv_ref[...]
    cos_ref[...] = jnp.cos(ang)
    sin_ref[...] = jnp.sin(ang)


def _rope_tables(positions):
    n = positions.size
    half = MLA_ROPE // 2
    per_row = LANES // half
    inv_freq = ROPE_BASE ** (-jnp.arange(0, MLA_ROPE, 2, dtype=F32) / MLA_ROPE)
    inv_row = jnp.tile(inv_freq, per_row).reshape(1, LANES)
    pos_rep = jnp.repeat(positions.reshape(n // per_row, per_row), half, axis=1)
    rows = n // per_row
    tr = min(rows, 1024)
    spec = pl.BlockSpec((tr, LANES), lambda i: (i, 0))
    cos_p, sin_p = pl.pallas_call(
        _rope_kernel,
        grid=(rows // tr,),
        in_specs=[spec, _const_spec((1, LANES))],
        out_specs=[spec, spec],
        out_shape=[jax.ShapeDtypeStruct((rows, LANES), F32)] * 2,
        compiler_params=_cparams(1),
        name="rope_tables",
    )(pos_rep, inv_row)
    cos_t = jnp.tile(cos_p.reshape(n, half), (1, per_row))
    sin_t = jnp.tile(sin_p.reshape(n, half), (1, per_row))
    return cos_t, sin_t


def _memkv_kernel(mem_ref, g_ref, w_ref, o_ref):
    mn = _rms(mem_ref[...], g_ref[...]).astype(BF16)
    o_ref[0] = _dot(mn, w_ref[0]).astype(BF16)


def _mem_kv(mem2d, mem_norm, wkv_bf16):
    depth, d, d2 = wkv_bf16.shape
    rows = mem2d.shape[0]
    tr = min(rows, 512)
    return pl.pallas_call(
        _memkv_kernel,
        grid=(depth, rows // tr),
        in_specs=[pl.BlockSpec((tr, d), lambda l, i: (i, 0)),
                  _const_spec((1, d)),
                  pl.BlockSpec((1, d, d2), lambda l, i: (l, 0, 0))],
        out_specs=pl.BlockSpec((1, tr, d2), lambda l, i: (l, i, 0)),
        out_shape=jax.ShapeDtypeStruct((depth, rows, d2), BF16),
        compiler_params=_cparams(2),
        name="mem_kv",
    )(mem2d, mem_norm.reshape(1, d), wkv_bf16)


def _rope_tile(x, cos, sin, lane):
    half = MLA_ROPE // 2
    up = pltpu.roll(x, LANES - half, axis=1)
    dn = pltpu.roll(x, half, axis=1)
    rot = jnp.where(lane < half, -up, dn)
    return jnp.where(lane < MLA_ROPE, x * cos + rot * sin, 0.0)


def _causal_conv4(x, cw_ref, cs, xpad, qpad):
    assert cw_ref.shape[0] == 4
    tm = x.shape[0]
    xpad[CARRY_ROWS:CARRY_ROWS + tm, cs] = x
    xm1 = xpad[pl.ds(CARRY_ROWS - 1, tm), cs]
    p = cw_ref[3:4, cs] * x + cw_ref[2:3, cs] * xm1
    q = cw_ref[1:2, cs] * x + cw_ref[0:1, cs] * xm1
    qpad[CARRY_ROWS:CARRY_ROWS + tm, cs] = q
    out = p + qpad[pl.ds(CARRY_ROWS - 2, tm), cs]
    xpad[0:CARRY_ROWS, cs] = x[tm - CARRY_ROWS:, :]
    qpad[0:CARRY_ROWS, cs] = q[tm - CARRY_ROWS:, :]
    return out


def _even_in_kernel(x_ref, g_ref, win_ref, cw_ref, alog_ref, dtb_ref, qn_ref, kvn_ref,
                    wuq_ref, wukv_ref, cos_ref, sin_ref,
                    q_out, k_out, v_out, z_out, gb_out, qf_out, kf_out, vb_out,
                    xpad, qpad):
    tm = x_ref.shape[0]
    qk_w = GDN_HEADS * GDN_DK
    qkv_w = 3 * qk_w

    @pl.when(pl.program_id(1) == 0)
    def _():
        xpad[0:CARRY_ROWS, :] = jnp.zeros((CARRY_ROWS, qkv_w), F32)
        qpad[0:CARRY_ROWS, :] = jnp.zeros((CARRY_ROWS, qkv_w), F32)

    h = _rms(x_ref[...], g_ref[...]).astype(BF16)
    lane = lax.broadcasted_iota(jnp.int32, (tm, LANES), 1)
    cos = cos_ref[...]
    sin = sin_ref[...]
    nope_w = MLA_HEADS * MLA_NOPE
    z_off = qkv_w
    cq_off = z_off + qk_w
    ckv_off = cq_off + qn_ref.shape[1]
    misc_off = ckv_off + kvn_ref.shape[1]
    gw = 2 * GDN_DK
    st = {}

    def proj_cols(lo, width):
        return _dot(h, win_ref[:, lo:lo + width])

    def mm_cq():
        st["cq"] = proj_cols(cq_off, qn_ref.shape[1])

    def ep_cq():
        st["cqb"] = _rms(st.pop("cq"), qn_ref[...] * (MLA_SCALE * LOG2E)).astype(BF16)

    def mm_ckv():
        st["ckv"] = proj_cols(ckv_off, kvn_ref.shape[1])

    def ep_ckv():
        st["ckvb"] = _rms(st.pop("ckv"), kvn_ref[...]).astype(BF16)

    def mm_misc():
        st["misc"] = proj_cols(misc_off, LANES)

    def ep_misc():
        misc = st.pop("misc")
        gdec = -jnp.exp(alog_ref[...]) * _softplus(misc + dtb_ref[...])
        beta = _sigmoid(misc)
        gb_out[...] = jnp.where((lane >= G_LANE) & (lane < G_LANE + GDN_HEADS), gdec,
                                jnp.where((lane >= B_LANE) & (lane < B_LANE + GDN_HEADS),
                                          beta, 0.0))
        st["kpe"] = _rope_tile(misc, cos, sin, lane).astype(BF16)

    def mm_uq():
        st["qf"] = _dot(st.pop("cqb"), wuq_ref[...])

    def ep_uq():
        qf = st.pop("qf")
        for hh in range(MLA_HEADS):
            base = 2 * LANES * hh
            qf_out[:, base:base + LANES] = qf[:, hh * LANES:(hh + 1) * LANES].astype(BF16)
            pe = qf[:, nope_w + hh * LANES: nope_w + (hh + 1) * LANES]
            qf_out[:, base + LANES:base + 2 * LANES] = _rope_tile(pe, cos, sin, lane).astype(BF16)

    def mm_ukv():
        st["kv"] = _dot(st.pop("ckvb"), wukv_ref[...])

    def ep_ukv():
        kv = st.pop("kv")
        for hh in range(MLA_HEADS):
            base = 2 * LANES * hh
            kf_out[:, base:base + LANES] = kv[:, hh * LANES:(hh + 1) * LANES].astype(BF16)
            kf_out[:, base + LANES:base + 2 * LANES] = st["kpe"]
        vb_out[...] = kv[:, nope_w:].astype(BF16)

    def mm_group(lo):
        def run():
            st[lo] = proj_cols(lo, gw)
        return run

    def ep_qkv(lo):
        def run():
            cs = slice(lo, lo + gw)
            acc = _causal_conv4(st.pop(lo), cw_ref, cs, xpad, qpad)
            act = acc * _sigmoid(acc)
            which, off = divmod(lo, qk_w)
            if which == 2:
                v_out[:, off:off + gw] = act.astype(BF16)
                return
            out, scale = (q_out, GDN_DK ** -0.5) if which == 0 else (k_out, 1.0)
            for sub in range(gw // GDN_DK):
                a = act[:, sub * GDN_DK:(sub + 1) * GDN_DK]
                nrm = a * (lax.rsqrt(jnp.sum(a * a, axis=-1, keepdims=True) + EPS) * scale)
                out[:, off + sub * GDN_DK: off + (sub + 1) * GDN_DK] = nrm.astype(BF16)
        return run

    def ep_z(lo):
        def run():
            z_out[:, lo - z_off: lo - z_off + gw] = st.pop(lo).astype(BF16)
        return run

    pairs = [(mm_cq, ep_cq), (mm_ckv, ep_ckv), (mm_misc, ep_misc), (mm_uq, ep_uq), (mm_ukv, ep_ukv)]
    pairs += [(mm_group(lo), ep_qkv(lo)) for lo in range(0, qkv_w, gw)]
    pairs += [(mm_group(lo), ep_z(lo)) for lo in range(z_off, cq_off, gw)]
    pairs[0][0]()
    for i, (_, ep) in enumerate(pairs):
        if i + 1 < len(pairs):
            pairs[i + 1][0]()
        ep()


def _even_inproj(xf, g0, w_in_re, conv_w, alog_row, dtb_row, q_norm, kv_norm, wuq_re, wukv_re,
                 cos_t, sin_t, batch, tm):
    n, d = xf.shape
    t = n // batch
    nt = t // tm
    e_in = w_in_re.shape[1]
    qkv_w = conv_w.shape[1]
    qk_w = qkv_w // 3
    row = lambda w: pl.BlockSpec((tm, w), lambda b, i: (b * nt + i, 0))
    outs = [(qk_w, BF16)] * 4 + [(LANES, F32), (2 * LANES * MLA_HEADS, BF16),
                                 (2 * LANES * MLA_HEADS, BF16), (LANES * MLA_HEADS, BF16)]
    return pl.pallas_call(
        _even_in_kernel,
        grid=(batch, nt),
        in_specs=[row(d), _const_spec((1, d)), _const_spec((d, e_in)), _const_spec((CONV_W, qkv_w)),
                  _const_spec((1, LANES)), _const_spec((1, LANES)),
                  _const_spec(q_norm.shape), _const_spec(kv_norm.shape),
                  _const_spec(wuq_re.shape), _const_spec(wukv_re.shape),
                  row(LANES), row(LANES)],
        out_specs=[row(w) for w, _ in outs],
        out_shape=[jax.ShapeDtypeStruct((n, w), dt) for w, dt in outs],
        scratch_shapes=[pltpu.VMEM((tm + CARRY_ROWS, qkv_w), F32),
                        pltpu.VMEM((tm + CARRY_ROWS, qkv_w), F32)],
        compiler_params=_cparams(2),
        name="even_inproj",
    )(xf, g0, w_in_re, conv_w, alog_row, dtb_row, q_norm, kv_norm, wuq_re, wukv_re, cos_t, sin_t)


SUPER = 2 * CHUNK
G_LANE = MLA_ROPE
B_LANE = MLA_ROPE + GDN_HEADS
SOLVE_UNROLL = 2


def _gdn_kernel(q_ref, k_ref, v_ref, z_ref, gb_ref, on_ref, out_ref,
                gc_scr, gct_scr, gl_scr, u_scr, w_scr, a_scr, qe_scr, kd_scr, o_scr, s_scr):
    t = q_ref.shape[0]
    n_super = t // SUPER
    n_chunk = t // CHUNK

    gb = gb_ref[...]
    rowi = lax.broadcasted_iota(jnp.int32, gb.shape, 0) & (CHUNK - 1)
    gc = gb
    s = 1
    while s < CHUNK:
        gc = jnp.where(rowi >= s, gc + pltpu.roll(gc, s, axis=0), gc)
        s *= 2
    gc_scr[...] = gc
    gct_scr[...] = gc.T
    g3 = gc.reshape(n_chunk, CHUNK, LANES)
    gl_scr[...] = jnp.broadcast_to(g3[:, CHUNK - 1:CHUNK, :], g3.shape).reshape(t, LANES)

    ri = lax.broadcasted_iota(jnp.int32, (SUPER, SUPER), 0)
    ci = lax.broadcasted_iota(jnp.int32, (SUPER, SUPER), 1)
    same = (ri >= CHUNK) == (ci >= CHUNK)
    causal = same & (ri >= ci)
    strict = same & (ri > ci)
    eye = (ri == ci).astype(F32)

    heads = range(GDN_HEADS)
    hsl = [slice(hh * GDN_DK, (hh + 1) * GDN_DK) for hh in heads]
    lane_g = [slice(G_LANE + hh, G_LANE + hh + 1) for hh in heads]
    lane_b = [slice(B_LANE + hh, B_LANE + hh + 1) for hh in heads]

    def solve_body(it, _):
        probs = []
        for sub in range(SOLVE_UNROLL):
            r0 = pl.multiple_of((it * SOLVE_UNROLL + sub) * SUPER, SUPER)
            probs += [(pl.ds(r0, SUPER), hh) for hh in heads]
        kc = [k_ref[rows, hsl[hh]] for rows, hh in probs]
        qc = [q_ref[rows, hsl[hh]] for rows, hh in probs]
        gcol = [gc_scr[rows, lane_g[hh]] for rows, hh in probs]
        bcol = [gb_ref[rows, lane_b[hh]] for rows, hh in probs]
        grow = [gct_scr[lane_g[hh], rows] for rows, hh in probs]
        kk = [_dot_nt(k, k) for k in kc]
        qk = [_dot_nt(q, k) for q, k in zip(qc, kc)]
        decay = [jnp.where(causal, jnp.exp(jnp.where(causal, gc - gr, 0.0)), 0.0)
                 for gc, gr in zip(gcol, grow)]
        m = [jnp.where(strict, -(x * b * dc), 0.0) for x, b, dc in zip(kk, bcol, decay)]
        qacc = [eye + x for x in m]
        mb = [x.astype(BF16) for x in m]
        mj = [_dot(x, x) for x in mb]
        lvl = 2
        while lvl < CHUNK // 2:
            mb = [x.astype(BF16) for x in mj]
            r = [_dot(x, jnp.concatenate([x, qa.astype(BF16)], axis=1)) for x, qa in zip(mb, qacc)]
            mj = [x[:, :SUPER] for x in r]
            qacc = [qa + x[:, SUPER:] for qa, x in zip(qacc, r)]
            lvl *= 2
        qacc = [qa + _dot(x.astype(BF16), qa.astype(BF16)) for qa, x in zip(qacc, mj)]
        eg = [jnp.exp(gc) for gc in gcol]
        sol = []
        for i, (rows, hh) in enumerate(probs):
            vc = v_ref[rows, hsl[hh]].astype(F32)
            kf = kc[i].astype(F32)
            rhs = jnp.concatenate([vc * bcol[i], kf * (bcol[i] * eg[i])], axis=1).astype(BF16)
            sol.append(_dot(qacc[i].astype(BF16), rhs))
        for i, (rows, hh) in enumerate(probs):
            u_scr[rows, hsl[hh]] = sol[i][:, :GDN_DK]
            w_scr[rows, hsl[hh]] = sol[i][:, GDN_DK:].astype(BF16)
            aqk = qk[i] * decay[i]
            a_sh = pltpu.roll(aqk, CHUNK, axis=1)
            a_scr[rows, hsl[hh]] = jnp.where(ri < CHUNK, aqk, a_sh).astype(BF16)
            qe_scr[rows, hsl[hh]] = (qc[i].astype(F32) * eg[i]).astype(BF16)
            glast = gl_scr[rows, lane_g[hh]]
            kd_scr[rows, hsl[hh]] = (kc[i].astype(F32) * jnp.exp(glast - gcol[i])).astype(BF16)
        return 0

    lax.fori_loop(0, n_super // SOLVE_UNROLL, solve_body, 0)

    s_scr[...] = jnp.zeros(s_scr.shape, F32)

    def scan_body(c, _):
        r0 = pl.multiple_of(c * CHUNK, CHUNK)
        rows = pl.ds(r0, CHUNK)
        st = [s_scr[hh] for hh in heads]
        sb = [x.astype(BF16) for x in st]
        t1 = [_dot(w_scr[rows, hsl[hh]], sb[hh]) for hh in heads]
        oq = [_dot(qe_scr[rows, hsl[hh]], sb[hh]) for hh in heads]
        vb = [(u_scr[rows, hsl[hh]] - t1[hh]).astype(BF16) for hh in heads]
        oa = [_dot(a_scr[rows, hsl[hh]][:, :CHUNK], vb[hh]) for hh in heads]
        kv = [lax.dot_general(kd_scr[rows, hsl[hh]], vb[hh], TN_DIMS, preferred_element_type=F32)
              for hh in heads]
        for hh in heads:
            gam = jnp.exp(gl_scr[pl.ds(r0, 1), lane_g[hh]])
            o_scr[rows, hsl[hh]] = oq[hh] + oa[hh]
            s_scr[hh] = st[hh] * gam + kv[hh]
        return 0

    lax.fori_loop(0, n_chunk, scan_body, 0)

    on = on_ref[...]
    for hh in range(GDN_HEADS):
        hs = slice(hh * GDN_DK, (hh + 1) * GDN_DK)
        z = z_ref[:, hs].astype(F32)
        out_ref[:, hs] = (_rms(o_scr[:, hs], on) * (z * _sigmoid(z))).astype(BF16)


def _gdn(q, k, v, z, gb, o_norm, batch):
    n, w = q.shape
    t = n // batch
    row = lambda ww: pl.BlockSpec((t, ww), lambda b: (b, 0))
    return pl.pallas_call(
        _gdn_kernel,
        grid=(batch,),
        in_specs=[row(w), row(w), row(w), row(w), row(LANES), _const_spec((1, GDN_DK))],
        out_specs=row(w),
        out_shape=jax.ShapeDtypeStruct((n, w), BF16),
        scratch_shapes=[pltpu.VMEM((t, LANES), F32),
                        pltpu.VMEM((LANES, t), F32),
                        pltpu.VMEM((t, LANES), F32),
                        pltpu.VMEM((t, w), F32),
                        pltpu.VMEM((t, w), BF16),
                        pltpu.VMEM((t, w), BF16),
                        pltpu.VMEM((t, w), BF16),
                        pltpu.VMEM((t, w), BF16),
                        pltpu.VMEM((t, w), F32),
                        pltpu.VMEM((GDN_HEADS, GDN_DK, GDN_DK), F32)],
        compiler_params=_cparams(1),
        name="gdn",
    )(q, k, v, z, gb, o_norm)


MLA_TQ = 256


def _mla_kernel(q_ref, k_ref, v_ref, o_ref):
    t = q_ref.shape[0]
    tq = MLA_TQ
    nq = t // tq
    hw = 2 * LANES
    heads = range(q_ref.shape[1] // hw)
    ri = lax.broadcasted_iota(jnp.int32, (tq, tq), 0) // CHUNK
    ci = lax.broadcasted_iota(jnp.int32, (tq, tq), 1) // CHUNK
    diag_mask = ci <= ri

    for i in range(nq):
        qrows = slice(i * tq, (i + 1) * tq)
        kw = (i + 1) * tq
        for h in heads:
            s = _dot_nt(q_ref[qrows, h * hw:(h + 1) * hw], k_ref[0:kw, h * hw:(h + 1) * hw])
            s_diag = jnp.where(diag_mask, s[:, kw - tq:], -jnp.inf)
            s = s_diag if i == 0 else jnp.concatenate([s[:, :kw - tq], s_diag], axis=1)
            m = jnp.max(s, axis=-1, keepdims=True)
            p = jnp.exp2(s - m)
            l = jnp.sum(p, axis=-1, keepdims=True)
            pv = _dot(p.astype(BF16), v_ref[0:kw, h * LANES:(h + 1) * LANES])
            o_ref[qrows, h * LANES:(h + 1) * LANES] = (pv / l).astype(BF16)


MLA_HEADS_PER_STEP = 2


def _mla(qf, kf, vb, batch):
    n = qf.shape[0]
    t = n // batch
    hps = MLA_HEADS_PER_STEP
    spec = lambda w: pl.BlockSpec((t, w * hps), lambda b, g: (b, g))
    return pl.pallas_call(
        _mla_kernel,
        grid=(batch, MLA_HEADS // hps),
        in_specs=[spec(2 * LANES), spec(2 * LANES), spec(LANES)],
        out_specs=spec(LANES),
        out_shape=jax.ShapeDtypeStruct((n, vb.shape[1]), BF16),
        compiler_params=_cparams(2),
        name="mla_attn",
    )(qf, kf, vb)


def _out_kernel(x_ref, a_ref, b_ref, w_ref, g_ref, o_ref):
    ka = a_ref.shape[1]
    y = _dot(a_ref[...], w_ref[0:ka, :]) + _dot(b_ref[...], w_ref[ka:, :])
    o_ref[...] = x_ref[...] + _rms(y, g_ref[...])


def _out_proj(xf, a, b, w_out, g, tm):
    n, d = xf.shape
    row = lambda w: pl.BlockSpec((tm, w), lambda i: (i, 0))
    return pl.pallas_call(
        _out_kernel,
        grid=(n // tm,),
        in_specs=[row(d), row(a.shape[1]), row(b.shape[1]), _const_spec(w_out.shape),
                  _const_spec((1, d))],
        out_specs=row(d),
        out_shape=jax.ShapeDtypeStruct((n, d), F32),
        compiler_params=_cparams(1),
        name="even_outproj",
    )(xf, a, b, w_out, g)


SCAN_PHASES = 8


def _gelu_tanh(x):
    c2 = 2.0 * math.sqrt(2.0 / math.pi) * LOG2E
    return x / (1.0 + jnp.exp2(x * (-c2 - (c2 * 0.044715) * (x * x))))


def _odd_kernel(x_ref, g0_ref, win_ref, cw_ref, cb_ref, gaw_ref, gab_ref, gxw_ref, gxb_ref,
                ap_ref, wout_ref, g1_ref, o_ref, xpad, qpad, hcar, a_scr, u_scr):
    tm, d = x_ref.shape
    bw = d // LRU_BLOCKS

    @pl.when(pl.program_id(1) == 0)
    def _():
        xpad[0:CARRY_ROWS, :] = jnp.zeros((CARRY_ROWS, d), F32)
        qpad[0:CARRY_ROWS, :] = jnp.zeros((CARRY_ROWS, d), F32)
        hcar[...] = jnp.zeros(hcar.shape, F32)

    x = x_ref[...]
    h = _rms(x, g0_ref[...]).astype(BF16)
    grp = tm // SCAN_PHASES
    slabs_per_block = bw // LANES
    rowg = lax.broadcasted_iota(jnp.int32, (grp, 1), 0)

    blocks = range(LRU_BLOCKS)
    csl = [slice(n * bw, (n + 1) * bw) for n in blocks]
    st = [dict() for _ in blocks]

    def projx(n):
        st[n]["xb"] = _dot(h, win_ref[:, csl[n]])

    def projy(n):
        st[n]["yb"] = _dot(h, win_ref[:, d + n * bw: d + (n + 1) * bw])

    def conv(n):
        cs, xb = csl[n], st[n].pop("xb")
        st[n]["xc"] = _causal_conv4(xb, cw_ref, cs, xpad, qpad) + cb_ref[:, cs]

    def gates(n):
        xcb = st[n]["xc"].astype(BF16)
        st[n]["ra"] = _dot(xcb, gaw_ref[n])
        st[n]["ia"] = _dot(xcb, gxw_ref[n])

    def recur(n):
        cs, xc = csl[n], st[n].pop("xc")
        r = _sigmoid(st[n].pop("ra") + gab_ref[:, cs])
        ig = _sigmoid(st[n].pop("ia") + gxb_ref[:, cs])
        log_a = (-LRU_C) * r * _softplus(-ap_ref[:, cs])
        a = jnp.exp(log_a)
        om = 1.0 - a * a
        u = jnp.where(om > 0.0, om * lax.rsqrt(om), 0.0) * (ig * xc)

        slabs = range(n * slabs_per_block, (n + 1) * slabs_per_block)
        for k, sl in enumerate(slabs):
            a_scr[sl] = a[:, k * LANES:(k + 1) * LANES]
            u_scr[sl] = u[:, k * LANES:(k + 1) * LANES]
        pa, pu = [], []
        for sl in slabs:
            ca = a_scr[sl, pl.ds(0, grp, stride=SCAN_PHASES), :]
            cu = u_scr[sl, pl.ds(0, grp, stride=SCAN_PHASES), :]
            la, lu = [ca], [cu]
            for j in range(1, SCAN_PHASES):
                aj = a_scr[sl, pl.ds(j, grp, stride=SCAN_PHASES), :]
                uj = u_scr[sl, pl.ds(j, grp, stride=SCAN_PHASES), :]
                cu = aj * cu + uj
                ca = aj * ca
                la.append(ca)
                lu.append(cu)
            pa.append(la)
            pu.append(lu)
        ga = jnp.concatenate([p[-1] for p in pa], axis=1)
        gu = jnp.concatenate([p[-1] for p in pu], axis=1)
        s = 1
        while s < grp:
            valid = rowg >= s
            gu = jnp.where(valid, ga * pltpu.roll(gu, s, axis=0) + gu, gu)
            ga = jnp.where(valid, ga * pltpu.roll(ga, s, axis=0), ga)
            s *= 2
        h_in = hcar[:, cs]
        h_end = ga * h_in + gu
        hcar[:, cs] = h_end[grp - 1:grp, :]
        h_prev = jnp.where(rowg == 0, h_in, pltpu.roll(h_end, 1, axis=0))
        for k, sl in enumerate(slabs):
            hp = h_prev[:, k * LANES:(k + 1) * LANES]
            for j in range(SCAN_PHASES):
                u_scr[sl, pl.ds(j, grp, stride=SCAN_PHASES), :] = pa[k][j] * hp + pu[k][j]
        st[n]["hs"] = jnp.concatenate([u_scr[sl] for sl in slabs], axis=1)

    def gate(n):
        st[n]["hg"] = (st[n].pop("hs") * _gelu_tanh(st[n].pop("yb"))).astype(BF16)

    def outp(n):
        st[n]["y"] = _dot(st[n].pop("hg"), wout_ref[csl[n], :])

    chain = (projx, conv, gates, recur, projy, gate, outp)
    for k in range(LRU_BLOCKS + len(chain) - 1):
        for s, stage in enumerate(chain):
            if 0 <= k - s < LRU_BLOCKS:
                stage(k - s)
    y = functools.reduce(lambda p, q: p + q, [st[n]["y"] for n in blocks])
    o_ref[...] = x + _rms(y, g1_ref[...])


def _odd_mixer(xf, g0, g1, w_in, conv_w, conv_b, gaw, gab, gxw, gxb, a_param, w_out, batch, tm):
    n, d = xf.shape
    t = n // batch
    nt = t // tm
    row = pl.BlockSpec((tm, d), lambda b, i: (b * nt + i, 0))
    vec = _const_spec((1, d))
    return pl.pallas_call(
        _odd_kernel,
        grid=(batch, nt),
        in_specs=[row, vec, _const_spec(w_in.shape), _const_spec(conv_w.shape), vec,
                  _const_spec(gaw.shape), vec, _const_spec(gxw.shape), vec, vec,
                  _const_spec(w_out.shape), vec],
        out_specs=row,
        out_shape=jax.ShapeDtypeStruct((n, d), F32),
        scratch_shapes=[pltpu.VMEM((tm + CARRY_ROWS, d), F32), pltpu.VMEM((tm + CARRY_ROWS, d), F32),
                        pltpu.VMEM((1, d), F32),
                        pltpu.VMEM((d // LANES, tm, LANES), F32),
                        pltpu.VMEM((d // LANES, tm, LANES), F32)],
        compiler_params=_cparams(2),
        name="odd_mixer",
    )(xf, g0, w_in, conv_w, conv_b, gaw, gab, gxw, gxb, a_param, w_out, g1)


def _xattn_kernel(x_ref, g2_ref, wq_ref, kv_ref, wo_ref, g3_ref, o_ref):
    tm, d = x_ref.shape
    hd = d // XA_HEADS
    x = x_ref[...]
    h = _rms(x, g2_ref[...]).astype(BF16)
    q = (_dot(h, wq_ref[...]) * (hd ** -0.5 * LOG2E)).astype(BF16)
    outs = []
    for hh in range(XA_HEADS):
        k = kv_ref[:, hh * hd:(hh + 1) * hd]
        v = kv_ref[:, d + hh * hd: d + (hh + 1) * hd]
        s = _dot_nt(q[:, hh * hd:(hh + 1) * hd], k)
        m = jnp.max(s, axis=-1, keepdims=True)
        p = jnp.exp2(s - m)
        l = jnp.sum(p, axis=-1, keepdims=True)
        outs.append((_dot(p.astype(BF16), v) / l).astype(BF16))
    o = jnp.concatenate(outs, axis=1)
    y = _dot(o, wo_ref[...])
    o_ref[...] = x + _rms(y, g3_ref[...])


def _xattn(xf, g2, g3, wq, kv_mem, wo, layer, batch, tm):
    n, d = xf.shape
    t = n // batch
    nt = t // tm
    n_mem = kv_mem.shape[1] // batch
    row = pl.BlockSpec((tm, d), lambda b, i: (b * nt + i, 0))
    vec = _const_spec((1, d))
    return pl.pallas_call(
        _xattn_kernel,
        grid=(batch, nt),
        in_specs=[row, vec, _layer_spec(wq.shape, layer),
                  pl.BlockSpec((None, n_mem, 2 * d), lambda b, i: (layer, b, 0)),
                  _layer_spec(wo.shape, layer), vec],
        out_specs=row,
        out_shape=jax.ShapeDtypeStruct((n, d), F32),
        compiler_params=_cparams(2),
        name="xattn",
    )(xf, g2, wq, kv_mem, wo, g3)


FFN_TF = 256


def _ffn_kernel(x_ref, g4_ref, win_ref, wo_ref, g5_ref, o_ref, act_scr):
    dff = wo_ref.shape[0]
    x = x_ref[...]
    h = _rms(x, g4_ref[...]).astype(BF16)
    for c in range(dff // FFN_TF):
        cols = slice(c * FFN_TF, (c + 1) * FFN_TF)
        gt = _dot(h, win_ref[:, cols])
        up = _dot(h, win_ref[:, dff + c * FFN_TF: dff + (c + 1) * FFN_TF])
        act_scr[:, cols] = (gt * _sigmoid(gt) * up).astype(BF16)
    y = _dot(act_scr[...], wo_ref[...])
    o_ref[...] = x + _rms(y, g5_ref[...])


def _ffn(xf, g4, g5, w_in, w_out, layer, tm):
    n, d = xf.shape
    dff = w_out.shape[1]
    row = pl.BlockSpec((tm, d), lambda i: (i, 0))
    vec = _const_spec((1, d))
    return pl.pallas_call(
        _ffn_kernel,
        grid=(n // tm,),
        in_specs=[row, vec, _layer_spec(w_in.shape, layer), _layer_spec(w_out.shape, layer), vec],
        out_specs=row,
        out_shape=jax.ShapeDtypeStruct((n, d), F32),
        scratch_shapes=[pltpu.VMEM((tm, dff), BF16)],
        compiler_params=_cparams(1),
        name="ffn",
    )(xf, g4, w_in, w_out, g5)


def _relayout_even(w_in, w_uq, w_ukv, a_log, dt_bias):
    d = w_in.shape[0]
    qkvz = 3 * GDN_HEADS * GDN_DK + GDN_HEADS * GDN_DK
    ab = 2 * GDN_HEADS
    lat = w_in.shape[1] - qkvz - ab - MLA_ROPE
    pad = LANES - MLA_ROPE - ab
    w_in_re = jnp.concatenate(
        [w_in[:, :qkvz], w_in[:, qkvz + ab:qkvz + ab + lat], w_in[:, qkvz + ab + lat:],
         w_in[:, qkvz:qkvz + ab], jnp.zeros((d, pad), w_in.dtype)], axis=1).astype(BF16)
    r = w_uq.shape[0]
    uq = w_uq.reshape(r, MLA_HEADS, MLA_NOPE + MLA_ROPE)
    pe = jnp.pad(uq[:, :, MLA_NOPE:], ((0, 0), (0, 0), (0, LANES - MLA_ROPE)))
    wuq_re = jnp.concatenate([uq[:, :, :MLA_NOPE].reshape(r, -1), pe.reshape(r, -1)],
                             axis=1).astype(BF16)
    ukv = w_ukv.reshape(w_ukv.shape[0], MLA_HEADS, -1)
    wukv_re = jnp.concatenate([ukv[:, :, :MLA_NOPE].reshape(r, -1),
                               ukv[:, :, MLA_NOPE:].reshape(r, -1)], axis=1).astype(BF16)
    alog_row = jnp.zeros((1, LANES), F32).at[0, G_LANE:G_LANE + GDN_HEADS].set(a_log)
    dtb_row = jnp.zeros((1, LANES), F32).at[0, G_LANE:G_LANE + GDN_HEADS].set(dt_bias)
    return w_in_re, wuq_re, wukv_re, alog_row, dtb_row


def kernel(x, mem, positions, norm_gains, mem_norm, e_w_in, e_conv_w, e_a_log, e_dt_bias, e_o_norm,
           e_q_norm, e_kv_norm, e_w_uq, e_w_ukv, e_w_out, o_w_in, o_conv_w, o_conv_b, o_gate_a_w,
           o_gate_a_b, o_gate_x_w, o_gate_x_b, o_a_param, o_w_out, xa_wq, xa_wkv, xa_wo, ffn_w_in,
           ffn_w_out):
    batch, t, d = x.shape
    n = batch * t
    depth = norm_gains.shape[0]
    xf = x.reshape(n, d)
    gains = norm_gains.reshape(depth, -1, 1, d)

    cos_t, sin_t = _rope_tables(positions)
    kv_mem = _mem_kv(mem.reshape(-1, d), mem_norm, xa_wkv.astype(BF16))
    xa_wq_b, xa_wo_b = xa_wq.astype(BF16), xa_wo.astype(BF16)
    ffn_w_in_b, ffn_w_out_b = ffn_w_in.astype(BF16), ffn_w_out.astype(BF16)

    for layer in range(depth):
        g = gains[layer]
        if layer % 2 == 0:
            e = layer // 2
            w_in_re, wuq_re, wukv_re, alog_row, dtb_row = _relayout_even(
                e_w_in[e], e_w_uq[e], e_w_ukv[e], e_a_log[e], e_dt_bias[e])
            q, k, v, z, gb, qf, kf, vb = _even_inproj(
                xf, g[0], w_in_re, e_conv_w[e], alog_row, dtb_row,
                e_q_norm[e].reshape(1, -1), e_kv_norm[e].reshape(1, -1), wuq_re, wukv_re,
                cos_t, sin_t, batch, tm=512)
            out_a = _gdn(q, k, v, z, gb, e_o_norm[e].reshape(1, -1), batch)
            out_b = _mla(qf, kf, vb, batch)
            xf = _out_proj(xf, out_a, out_b, e_w_out[e].astype(BF16), g[1], tm=1024)
        else:
            o = layer // 2
            xf = _odd_mixer(xf, g[0], g[1], o_w_in[o].astype(BF16), o_conv_w[o],
                            o_conv_b[o].reshape(1, d), o_gate_a_w[o].astype(BF16),
                            o_gate_a_b[o].reshape(1, d), o_gate_x_w[o].astype(BF16),
                            o_gate_x_b[o].reshape(1, d), o_a_param[o].reshape(1, d),
                            o_w_out[o].astype(BF16), batch, tm=512)
        xf = _xattn(xf, g[2], g[3], xa_wq_b, kv_mem, xa_wo_b, layer, batch, tm=512)
        xf = _ffn(xf, g[4], g[5], ffn_w_in_b, ffn_w_out_b, layer, tm=512)
    return xf.reshape(batch, t, d)
```

```python
import functools
import math

import jax
import jax.numpy as jnp
from jax import lax
from jax.experimental import pallas as pl
from jax.experimental.pallas import tpu as pltpu

F32 = jnp.float32
BF16 = jnp.bfloat16

EPS = 1e-6
CHUNK = 64
GDN_HEADS = 4
GDN_DK = 128
MLA_HEADS = 4
MLA_NOPE = 128
MLA_ROPE = 64
MLA_SCALE = (MLA_NOPE + MLA_ROPE) ** -0.5
LOG2E = math.log2(math.e)
ROPE_BASE = 10000.0
LRU_BLOCKS = 4
LRU_C = 8.0
XA_HEADS = 4
CONV_W = 4

LANES = 128
CARRY_ROWS = 8
V7X_VMEM_LIMIT = 56 * 1024 * 1024

NT_DIMS = (((1,), (1,)), ((), ()))
TN_DIMS = (((0,), (0,)), ((), ()))


def _cparams(n_axes):
    return pltpu.CompilerParams(
        dimension_semantics=("arbitrary",) * n_axes,
        vmem_limit_bytes=V7X_VMEM_LIMIT)


def _dot(a, b):
    return jnp.dot(a, b, preferred_element_type=F32)


def _dot_nt(a, b):
    return lax.dot_general(a, b, NT_DIMS, preferred_element_type=F32)


def _rms(xf, g):
    ms = jnp.mean(xf * xf, axis=-1, keepdims=True)
    return xf * lax.rsqrt(ms + EPS) * g


def _sigmoid(x):
    return 1.0 / (1.0 + jnp.exp(-x))


def _softplus(x):
    return jnp.maximum(x, 0.0) + jnp.log1p(jnp.exp(-jnp.abs(x)))


def _const_spec(shape):
    nd = len(shape)
    return pl.BlockSpec(shape, lambda *_: (0,) * nd)


def _layer_spec(shape, layer):
    nd = len(shape) - 1
    return pl.BlockSpec((None,) + tuple(shape[1:]), lambda *_: (layer,) + (0,) * nd,
                        pipeline_mode=pl.Buffered(1))


def _rope_kernel(pos_ref, inv_ref, cos_ref, sin_ref):
    half = MLA_ROPE // 2
    per_row = LANES // half
    tr = pos_ref.shape[0]
    ang = pos_ref[...].astype(F32) * inv_ref[...]
    lane = lax.broadcasted_iota(jnp.int32, ang.shape, 1)
    for tbl, ref in ((jnp.cos(ang), cos_ref), (jnp.sin(ang), sin_ref)):
        for j in range(per_row):
            own = jnp.where((lane >= half * j) & (lane < half * (j + 1)), tbl, 0.0)
            rep = own
            for k in range(1, per_row):
                rep = rep + pltpu.roll(own, half * k, axis=1)
            ref[pl.ds(j, tr, stride=per_row), :] = rep


def _rope_tables(positions):
    n = positions.size
    half = MLA_ROPE // 2
    per_row = LANES // half
    inv_freq = ROPE_BASE ** (-jnp.arange(0, MLA_ROPE, 2, dtype=F32) / MLA_ROPE)
    inv_row = jnp.tile(inv_freq, per_row).reshape(1, LANES)
    pos_rep = jnp.repeat(positions.reshape(n // per_row, per_row), half, axis=1)
    rows = n // per_row
    tr = min(rows, 512)
    out_spec = pl.BlockSpec((tr * per_row, LANES), lambda i: (i, 0))
    return pl.pallas_call(
        _rope_kernel,
        grid=(rows // tr,),
        in_specs=[pl.BlockSpec((tr, LANES), lambda i: (i, 0)), _const_spec((1, LANES))],
        out_specs=[out_spec, out_spec],
        out_shape=[jax.ShapeDtypeStruct((n, LANES), F32)] * 2,
        compiler_params=_cparams(1),
        name="rope_tables",
    )(pos_rep, inv_row)


def _memkv_kernel(mem_ref, g_ref, w_ref, o_ref):
    mn = _rms(mem_ref[...], g_ref[...]).astype(BF16)
    o_ref[0] = _dot(mn, w_ref[0]).astype(BF16)


def _mem_kv(mem2d, mem_norm, wkv_bf16):
    depth, d, d2 = wkv_bf16.shape
    rows = mem2d.shape[0]
    tr = min(rows, 512)
    return pl.pallas_call(
        _memkv_kernel,
        grid=(depth, rows // tr),
        in_specs=[pl.BlockSpec((tr, d), lambda l, i: (i, 0)),
                  _const_spec((1, d)),
                  pl.BlockSpec((1, d, d2), lambda l, i: (l, 0, 0))],
        out_specs=pl.BlockSpec((1, tr, d2), lambda l, i: (l, i, 0)),
        out_shape=jax.ShapeDtypeStruct((depth, rows, d2), BF16),
        compiler_params=_cparams(2),
        name="mem_kv",
    )(mem2d, mem_norm.reshape(1, d), wkv_bf16)


def _rope_tile(x, cos, sin, lane):
    half = MLA_ROPE // 2
    up = pltpu.roll(x, LANES - half, axis=1)
    dn = pltpu.roll(x, half, axis=1)
    rot = jnp.where(lane < half, -up, dn)
    return jnp.where(lane < MLA_ROPE, x * cos + rot * sin, 0.0)


def _causal_conv4(x, cw_ref, cs, xpad, qpad):
    assert cw_ref.shape[0] == 4
    tm = x.shape[0]
    xpad[CARRY_ROWS:CARRY_ROWS + tm, cs] = x
    xm1 = xpad[pl.ds(CARRY_ROWS - 1, tm), cs]
    p = cw_ref[3:4, cs] * x + cw_ref[2:3, cs] * xm1
    q = cw_ref[1:2, cs] * x + cw_ref[0:1, cs] * xm1
    qpad[CARRY_ROWS:CARRY_ROWS + tm, cs] = q
    out = p + qpad[pl.ds(CARRY_ROWS - 2, tm), cs]
    xpad[0:CARRY_ROWS, cs] = x[tm - CARRY_ROWS:, :]
    qpad[0:CARRY_ROWS, cs] = q[tm - CARRY_ROWS:, :]
    return out


def _even_in_kernel(x_ref, g_ref, win_ref, cw_ref, alog_ref, dtb_ref, qn_ref, kvn_ref,
                    wuq_ref, wukv_ref, cos_ref, sin_ref,
                    q_out, k_out, v_out, z_out, gb_out, qf_out, kf_out, vb_out,
                    xpad, qpad):
    tm = x_ref.shape[0]
    qk_w = GDN_HEADS * GDN_DK
    qkv_w = 3 * qk_w

    @pl.when(pl.program_id(1) == 0)
    def _():
        xpad[0:CARRY_ROWS, :] = jnp.zeros((CARRY_ROWS, qkv_w), F32)
        qpad[0:CARRY_ROWS, :] = jnp.zeros((CARRY_ROWS, qkv_w), F32)

    h = _rms(x_ref[...], g_ref[...]).astype(BF16)
    lane = lax.broadcasted_iota(jnp.int32, (tm, LANES), 1)
    cos = cos_ref[...]
    sin = sin_ref[...]
    nope_w = MLA_HEADS * MLA_NOPE
    z_off = qkv_w
    cq_off = z_off + qk_w
    ckv_off = cq_off + qn_ref.shape[1]
    misc_off = ckv_off + kvn_ref.shape[1]
    gw = 2 * GDN_DK
    st = {}

    def proj_cols(lo, width):
        return _dot(h, win_ref[:, lo:lo + width])

    def mm_cq():
        st["cq"] = proj_cols(cq_off, qn_ref.shape[1])

    def ep_cq():
        st["cqb"] = _rms(st.pop("cq"), qn_ref[...] * (MLA_SCALE * LOG2E)).astype(BF16)

    def mm_ckv():
        st["ckv"] = proj_cols(ckv_off, kvn_ref.shape[1])

    def ep_ckv():
        st["ckvb"] = _rms(st.pop("ckv"), kvn_ref[...]).astype(BF16)

    def mm_misc():
        st["misc"] = proj_cols(misc_off, LANES)

    def ep_misc():
        misc = st.pop("misc")
        gdec = -jnp.exp(alog_ref[...]) * _softplus(misc + dtb_ref[...])
        beta = _sigmoid(misc)
        gb_out[...] = jnp.where((lane >= G_LANE) & (lane < G_LANE + GDN_HEADS), gdec,
                                jnp.where((lane >= B_LANE) & (lane < B_LANE + GDN_HEADS),
                                          beta, 0.0))
        st["kpe"] = _rope_tile(misc, cos, sin, lane).astype(BF16)

    def mm_uq():
        st["qf"] = _dot(st.pop("cqb"), wuq_ref[...])

    def ep_uq():
        qf = st.pop("qf")
        for hh in range(MLA_HEADS):
            base = 2 * LANES * hh
            qf_out[:, base:base + LANES] = qf[:, hh * LANES:(hh + 1) * LANES].astype(BF16)
            pe = qf[:, nope_w + hh * LANES: nope_w + (hh + 1) * LANES]
            qf_out[:, base + LANES:base + 2 * LANES] = _rope_tile(pe, cos, sin, lane).astype(BF16)

    def mm_ukv():
        st["kv"] = _dot(st.pop("ckvb"), wukv_ref[...])

    def ep_ukv():
        kv = st.pop("kv")
        for hh in range(MLA_HEADS):
            base = 2 * LANES * hh
            kf_out[:, base:base + LANES] = kv[:, hh * LANES:(hh + 1) * LANES].astype(BF16)
            kf_out[:, base + LANES:base + 2 * LANES] = st["kpe"]
        vb_out[...] = kv[:, nope_w:].astype(BF16)

    def mm_group(lo):
        def run():
            st[lo] = proj_cols(lo, gw)
        return run

    def ep_qkv(lo):
        def run():
            cs = slice(lo, lo + gw)
            acc = _causal_conv4(st.pop(lo), cw_ref, cs, xpad, qpad)
            act = acc * _sigmoid(acc)
            which, off = divmod(lo, qk_w)
            if which == 2:
                v_out[:, off:off + gw] = act.astype(BF16)
                return
            out, scale = (q_out, GDN_DK ** -0.5) if which == 0 else (k_out, 1.0)
            for sub in range(gw // GDN_DK):
                a = act[:, sub * GDN_DK:(sub + 1) * GDN_DK]
                nrm = a * (lax.rsqrt(jnp.sum(a * a, axis=-1, keepdims=True) + EPS) * scale)
                out[:, off + sub * GDN_DK: off + (sub + 1) * GDN_DK] = nrm.astype(BF16)
        return run

    def ep_z(lo):
        def run():
            z_out[:, lo - z_off: lo - z_off + gw] = st.pop(lo).astype(BF16)
        return run

    pairs = [(mm_cq, ep_cq), (mm_ckv, ep_ckv), (mm_misc, ep_misc), (mm_uq, ep_uq), (mm_ukv, ep_ukv)]
    pairs += [(mm_group(lo), ep_qkv(lo)) for lo in range(0, qkv_w, gw)]
    pairs += [(mm_group(lo), ep_z(lo)) for lo in range(z_off, cq_off, gw)]
    pairs[0][0]()
    for i, (_, ep) in enumerate(pairs):
        if i + 1 < len(pairs):
            pairs[i + 1][0]()
        ep()


def _even_inproj(xf, g0, w_in_re, conv_w, alog_row, dtb_row, q_norm, kv_norm, wuq_re, wukv_re,
                 cos_t, sin_t, batch, tm):
    n, d = xf.shape
    t = n // batch
    nt = t // tm
    e_in = w_in_re.shape[1]
    qkv_w = conv_w.shape[1]
    qk_w = qkv_w // 3
    row = lambda w: pl.BlockSpec((tm, w), lambda b, i: (b * nt + i, 0))
    outs = [(qk_w, BF16)] * 4 + [(LANES, F32), (2 * LANES * MLA_HEADS, BF16),
                                 (2 * LANES * MLA_HEADS, BF16), (LANES * MLA_HEADS, BF16)]
    return pl.pallas_call(
        _even_in_kernel,
        grid=(batch, nt),
        in_specs=[row(d), _const_spec((1, d)), _const_spec((d, e_in)), _const_spec((CONV_W, qkv_w)),
                  _const_spec((1, LANES)), _const_spec((1, LANES)),
                  _const_spec(q_norm.shape), _const_spec(kv_norm.shape),
                  _const_spec(wuq_re.shape), _const_spec(wukv_re.shape),
                  row(LANES), row(LANES)],
        out_specs=[row(w) for w, _ in outs],
        out_shape=[jax.ShapeDtypeStruct((n, w), dt) for w, dt in outs],
        scratch_shapes=[pltpu.VMEM((tm + CARRY_ROWS, qkv_w), F32),
                        pltpu.VMEM((tm + CARRY_ROWS, qkv_w), F32)],
        compiler_params=_cparams(2),
        name="even_inproj",
    )(xf, g0, w_in_re, conv_w, alog_row, dtb_row, q_norm, kv_norm, wuq_re, wukv_re, cos_t, sin_t)


SUPER = 2 * CHUNK
G_LANE = MLA_ROPE
B_LANE = MLA_ROPE + GDN_HEADS
GDN_PIPE = 2


def _gdn_kernel(q_ref, k_ref, v_ref, z_ref, gb_ref, on_ref, out_ref,
                gc_scr, gct_scr, gl_scr, u_scr, w_scr, a_scr, qe_scr, kd_scr, o_scr, s_scr):
    t = q_ref.shape[0]
    n_super = t // SUPER
    n_chunk = t // CHUNK

    gb = gb_ref[...]
    rowi = lax.broadcasted_iota(jnp.int32, gb.shape, 0) & (CHUNK - 1)
    gc = gb
    s = 1
    while s < CHUNK:
        gc = jnp.where(rowi >= s, gc + pltpu.roll(gc, s, axis=0), gc)
        s *= 2
    gc_scr[...] = gc
    gct_scr[...] = gc.T
    g3 = gc.reshape(n_chunk, CHUNK, LANES)
    gl_scr[...] = jnp.broadcast_to(g3[:, CHUNK - 1:CHUNK, :], g3.shape).reshape(t, LANES)

    ri = lax.broadcasted_iota(jnp.int32, (SUPER, SUPER), 0)
    ci = lax.broadcasted_iota(jnp.int32, (SUPER, SUPER), 1)
    same = (ri >= CHUNK) == (ci >= CHUNK)
    causal = same & (ri >= ci)
    strict = same & (ri > ci)
    eye = (ri == ci).astype(F32)

    heads = range(GDN_HEADS)
    hsl = [slice(hh * GDN_DK, (hh + 1) * GDN_DK) for hh in heads]
    lane_g = [slice(G_LANE + hh, G_LANE + hh + 1) for hh in heads]
    lane_b = [slice(B_LANE + hh, B_LANE + hh + 1) for hh in heads]

    def solve_stages(sc0):
        probs = []
        for sub in range(GDN_PIPE):
            rows = pl.ds(pl.multiple_of((sc0 + sub) * SUPER, SUPER), SUPER)
            probs += [(rows, hh) for hh in heads]
        kc = [k_ref[rows, hsl[hh]] for rows, hh in probs]
        qc = [q_ref[rows, hsl[hh]] for rows, hh in probs]
        gcol = [gc_scr[rows, lane_g[hh]] for rows, hh in probs]
        bcol = [gb_ref[rows, lane_b[hh]] for rows, hh in probs]
        grow = [gct_scr[lane_g[hh], rows] for rows, hh in probs]
        kk = [_dot_nt(k, k) for k in kc]
        qk = [_dot_nt(q, k) for q, k in zip(qc, kc)]
        yield
        decay = [jnp.where(causal, jnp.exp(jnp.where(causal, gc - gr, 0.0)), 0.0)
                 for gc, gr in zip(gcol, grow)]
        m = [jnp.where(strict, -(x * b * dc), 0.0) for x, b, dc in zip(kk, bcol, decay)]
        qacc = [eye + x for x in m]
        mb = [x.astype(BF16) for x in m]
        mj = [_dot(x, x) for x in mb]
        yield
        lvl = 2
        while lvl < CHUNK // 2:
            mb = [x.astype(BF16) for x in mj]
            r = [_dot(x, jnp.concatenate([x, qa.astype(BF16)], axis=1)) for x, qa in zip(mb, qacc)]
            yield
            mj = [x[:, :SUPER] for x in r]
            qacc = [qa + x[:, SUPER:] for qa, x in zip(qacc, r)]
            lvl *= 2
        last = [_dot(x.astype(BF16), qa.astype(BF16)) for qa, x in zip(qacc, mj)]
        yield
        qacc = [qa + x for qa, x in zip(qacc, last)]
        eg = [jnp.exp(gc) for gc in gcol]
        sol = []
        for i, (rows, hh) in enumerate(probs):
            vc = v_ref[rows, hsl[hh]].astype(F32)
            kf = kc[i].astype(F32)
            rhs = jnp.concatenate([vc * bcol[i], kf * (bcol[i] * eg[i])], axis=1).astype(BF16)
            sol.append(_dot(qacc[i].astype(BF16), rhs))
        yield
        for i, (rows, hh) in enumerate(probs):
            u_scr[rows, hsl[hh]] = sol[i][:, :GDN_DK]
            w_scr[rows, hsl[hh]] = sol[i][:, GDN_DK:].astype(BF16)
            aqk = qk[i] * decay[i]
            a_sh = pltpu.roll(aqk, CHUNK, axis=1)
            a_scr[rows, hsl[hh]] = jnp.where(ri < CHUNK, aqk, a_sh).astype(BF16)
            qe_scr[rows, hsl[hh]] = (qc[i].astype(F32) * eg[i]).astype(BF16)
            glast = gl_scr[rows, lane_g[hh]]
            kd_scr[rows, hsl[hh]] = (kc[i].astype(F32) * jnp.exp(glast - gcol[i])).astype(BF16)

    def scan_stages(sc0, st):
        for half in range(GDN_PIPE * SUPER // CHUNK):
            r0 = pl.multiple_of(sc0 * SUPER + half * CHUNK, CHUNK)
            rows = pl.ds(r0, CHUNK)
            sb = [x.astype(BF16) for x in st]
            t1 = [_dot(w_scr[rows, hsl[hh]], sb[hh]) for hh in heads]
            oq = [_dot(qe_scr[rows, hsl[hh]], sb[hh]) for hh in heads]
            yield
            vb = [(u_scr[rows, hsl[hh]] - t1[hh]).astype(BF16) for hh in heads]
            oa = [_dot(a_scr[rows, hsl[hh]][:, :CHUNK], vb[hh]) for hh in heads]
            kv = [lax.dot_general(kd_scr[rows, hsl[hh]], vb[hh], TN_DIMS,
                                  preferred_element_type=F32) for hh in heads]
            yield
            for hh in heads:
                gam = jnp.exp(gl_scr[pl.ds(r0, 1), lane_g[hh]])
                o_scr[rows, hsl[hh]] = oq[hh] + oa[hh]
                st[hh] = st[hh] * gam + kv[hh]

    def run_alternating(*gens):
        live = list(gens)
        while live:
            live = [g for g in live if next(g, StopIteration) is not StopIteration]

    run_alternating(solve_stages(0))
    s_scr[...] = jnp.zeros(s_scr.shape, F32)
    n_steps = n_super // GDN_PIPE

    def pipe_body(it, _):
        st = [s_scr[hh] for hh in heads]
        run_alternating(solve_stages((it + 1) * GDN_PIPE), scan_stages(it * GDN_PIPE, st))
        for hh in heads:
            s_scr[hh] = st[hh]
        return 0

    lax.fori_loop(0, n_steps - 1, pipe_body, 0)
    st = [s_scr[hh] for hh in heads]
    run_alternating(scan_stages((n_steps - 1) * GDN_PIPE, st))

    on = on_ref[...]
    for hh in range(GDN_HEADS):
        hs = slice(hh * GDN_DK, (hh + 1) * GDN_DK)
        z = z_ref[:, hs].astype(F32)
        out_ref[:, hs] = (_rms(o_scr[:, hs], on) * (z * _sigmoid(z))).astype(BF16)


def _gdn(q, k, v, z, gb, o_norm, batch):
    n, w = q.shape
    t = n // batch
    row = lambda ww: pl.BlockSpec((t, ww), lambda b: (b, 0))
    return pl.pallas_call(
        _gdn_kernel,
        grid=(batch,),
        in_specs=[row(w), row(w), row(w), row(w), row(LANES), _const_spec((1, GDN_DK))],
        out_specs=row(w),
        out_shape=jax.ShapeDtypeStruct((n, w), BF16),
        scratch_shapes=[pltpu.VMEM((t, LANES), F32),
                        pltpu.VMEM((LANES, t), F32),
                        pltpu.VMEM((t, LANES), F32),
                        pltpu.VMEM((t, w), F32),
                        pltpu.VMEM((t, w), BF16),
                        pltpu.VMEM((t, w), BF16),
                        pltpu.VMEM((t, w), BF16),
                        pltpu.VMEM((t, w), BF16),
                        pltpu.VMEM((t, w), F32),
                        pltpu.VMEM((GDN_HEADS, GDN_DK, GDN_DK), F32)],
        compiler_params=_cparams(1),
        name="gdn",
    )(q, k, v, z, gb, o_norm)


MLA_TQ = 256


def _mla_kernel(q_ref, k_ref, v_ref, o_ref):
    t = q_ref.shape[0]
    tq = MLA_TQ
    nq = t // tq
    hw = 2 * LANES
    heads = range(q_ref.shape[1] // hw)
    ri = lax.broadcasted_iota(jnp.int32, (tq, tq), 0) // CHUNK
    ci = lax.broadcasted_iota(jnp.int32, (tq, tq), 1) // CHUNK
    diag_mask = ci <= ri

    for i in range(nq):
        qrows = slice(i * tq, (i + 1) * tq)
        kw = (i + 1) * tq
        for h in heads:
            s = _dot_nt(q_ref[qrows, h * hw:(h + 1) * hw], k_ref[0:kw, h * hw:(h + 1) * hw])
            s_diag = jnp.where(diag_mask, s[:, kw - tq:], -jnp.inf)
            s = s_diag if i == 0 else jnp.concatenate([s[:, :kw - tq], s_diag], axis=1)
            m = jnp.max(s, axis=-1, keepdims=True)
            p = jnp.exp2(s - m)
            l = jnp.sum(p, axis=-1, keepdims=True)
            pv = _dot(p.astype(BF16), v_ref[0:kw, h * LANES:(h + 1) * LANES])
            o_ref[qrows, h * LANES:(h + 1) * LANES] = (pv / l).astype(BF16)


MLA_HEADS_PER_STEP = 2


def _mla(qf, kf, vb, batch):
    n = qf.shape[0]
    t = n // batch
    hps = MLA_HEADS_PER_STEP
    spec = lambda w: pl.BlockSpec((t, w * hps), lambda b, g: (b, g))
    return pl.pallas_call(
        _mla_kernel,
        grid=(batch, MLA_HEADS // hps),
        in_specs=[spec(2 * LANES), spec(2 * LANES), spec(LANES)],
        out_specs=spec(LANES),
        out_shape=jax.ShapeDtypeStruct((n, vb.shape[1]), BF16),
        compiler_params=_cparams(2),
        name="mla_attn",
    )(qf, kf, vb)


SCAN_PHASES = 8


def _gelu_tanh(x):
    c2 = 2.0 * math.sqrt(2.0 / math.pi) * LOG2E
    return x / (1.0 + jnp.exp2(x * (-c2 - (c2 * 0.044715) * (x * x))))


def _odd_kernel(x_ref, g0_ref, win_ref, cw_ref, cb_ref, gaw_ref, gab_ref, gxw_ref, gxb_ref,
                ap_ref, wout_ref, g1_ref, o_ref, xpad, qpad, hcar, a_scr, u_scr):
    tm, d = x_ref.shape
    bw = d // LRU_BLOCKS

    @pl.when(pl.program_id(1) == 0)
    def _():
        xpad[0:CARRY_ROWS, :] = jnp.zeros((CARRY_ROWS, d), F32)
        qpad[0:CARRY_ROWS, :] = jnp.zeros((CARRY_ROWS, d), F32)
        hcar[...] = jnp.zeros(hcar.shape, F32)

    x = x_ref[...]
    h = _rms(x, g0_ref[...]).astype(BF16)
    grp = tm // SCAN_PHASES
    slabs_per_block = bw // LANES
    rowg = lax.broadcasted_iota(jnp.int32, (grp, 1), 0)

    blocks = range(LRU_BLOCKS)
    csl = [slice(n * bw, (n + 1) * bw) for n in blocks]
    st = [dict() for _ in blocks]

    def projx(n):
        st[n]["xb"] = _dot(h, win_ref[:, csl[n]])

    def projy(n):
        st[n]["yb"] = _dot(h, win_ref[:, d + n * bw: d + (n + 1) * bw])

    def conv(n):
        cs, xb = csl[n], st[n].pop("xb")
        st[n]["xc"] = _causal_conv4(xb, cw_ref, cs, xpad, qpad) + cb_ref[:, cs]

    def gates(n):
        xcb = st[n]["xc"].astype(BF16)
        st[n]["ra"] = _dot(xcb, gaw_ref[n])
        st[n]["ia"] = _dot(xcb, gxw_ref[n])

    def recur(n):
        cs, xc = csl[n], st[n].pop("xc")
        r = _sigmoid(st[n].pop("ra") + gab_ref[:, cs])
        ig = _sigmoid(st[n].pop("ia") + gxb_ref[:, cs])
        log_a = (-LRU_C) * r * _softplus(-ap_ref[:, cs])
        a = jnp.exp(log_a)
        om = 1.0 - a * a
        u = jnp.where(om > 0.0, om * lax.rsqrt(om), 0.0) * (ig * xc)

        slabs = range(n * slabs_per_block, (n + 1) * slabs_per_block)
        for k, sl in enumerate(slabs):
            a_scr[sl] = a[:, k * LANES:(k + 1) * LANES]
            u_scr[sl] = u[:, k * LANES:(k + 1) * LANES]
        pa, pu = [], []
        for sl in slabs:
            ca = a_scr[sl, pl.ds(0, grp, stride=SCAN_PHASES), :]
            cu = u_scr[sl, pl.ds(0, grp, stride=SCAN_PHASES), :]
            la, lu = [ca], [cu]
            for j in range(1, SCAN_PHASES):
                aj = a_scr[sl, pl.ds(j, grp, stride=SCAN_PHASES), :]
                uj = u_scr[sl, pl.ds(j, grp, stride=SCAN_PHASES), :]
                cu = aj * cu + uj
                ca = aj * ca
                la.append(ca)
                lu.append(cu)
            pa.append(la)
            pu.append(lu)
        ga = jnp.concatenate([p[-1] for p in pa], axis=1)
        gu = jnp.concatenate([p[-1] for p in pu], axis=1)
        s = 1
        while s < grp:
            valid = rowg >= s
            gu = jnp.where(valid, ga * pltpu.roll(gu, s, axis=0) + gu, gu)
            ga = jnp.where(valid, ga * pltpu.roll(ga, s, axis=0), ga)
            s *= 2
        h_in = hcar[:, cs]
        h_end = ga * h_in + gu
        hcar[:, cs] = h_end[grp - 1:grp, :]
        h_prev = jnp.where(rowg == 0, h_in, pltpu.roll(h_end, 1, axis=0))
        for k, sl in enumerate(slabs):
            hp = h_prev[:, k * LANES:(k + 1) * LANES]
            for j in range(SCAN_PHASES):
                u_scr[sl, pl.ds(j, grp, stride=SCAN_PHASES), :] = pa[k][j] * hp + pu[k][j]
        st[n]["hs"] = jnp.concatenate([u_scr[sl] for sl in slabs], axis=1)

    def gate(n):
        st[n]["hg"] = (st[n].pop("hs") * _gelu_tanh(st[n].pop("yb"))).astype(BF16)

    def outp(n):
        st[n]["y"] = _dot(st[n].pop("hg"), wout_ref[csl[n], :])

    chain = (projx, conv, gates, recur, projy, gate, outp)
    for k in range(LRU_BLOCKS + len(chain) - 1):
        for s, stage in enumerate(chain):
            if 0 <= k - s < LRU_BLOCKS:
                stage(k - s)
    y = functools.reduce(lambda p, q: p + q, [st[n]["y"] for n in blocks])
    o_ref[...] = x + _rms(y, g1_ref[...])


def _odd_mixer(xf, g0, g1, w_in, conv_w, conv_b, gaw, gab, gxw, gxb, a_param, w_out, batch, tm):
    n, d = xf.shape
    t = n // batch
    nt = t // tm
    row = pl.BlockSpec((tm, d), lambda b, i: (b * nt + i, 0))
    vec = _const_spec((1, d))
    return pl.pallas_call(
        _odd_kernel,
        grid=(batch, nt),
        in_specs=[row, vec, _const_spec(w_in.shape), _const_spec(conv_w.shape), vec,
                  _const_spec(gaw.shape), vec, _const_spec(gxw.shape), vec, vec,
                  _const_spec(w_out.shape), vec],
        out_specs=row,
        out_shape=jax.ShapeDtypeStruct((n, d), F32),
        scratch_shapes=[pltpu.VMEM((tm + CARRY_ROWS, d), F32), pltpu.VMEM((tm + CARRY_ROWS, d), F32),
                        pltpu.VMEM((1, d), F32),
                        pltpu.VMEM((d // LANES, tm, LANES), F32),
                        pltpu.VMEM((d // LANES, tm, LANES), F32)],
        compiler_params=_cparams(2),
        name="odd_mixer",
    )(xf, g0, w_in, conv_w, conv_b, gaw, gab, gxw, gxb, a_param, w_out, g1)


def _xattn_kernel(x_ref, g2_ref, wq_ref, kv_ref, wo_ref, g3_ref, o_ref):
    o_ref[...] = _xattn_tile(x_ref[...], g2_ref, wq_ref, kv_ref, wo_ref, g3_ref)


def _outproj_xattn_kernel(x_ref, a_ref, b_ref, w_ref, g1_ref,
                          g2_ref, wq_ref, kv_ref, wo_ref, g3_ref, o_ref):
    ka = a_ref.shape[1]
    y = _dot(a_ref[...], w_ref[0:ka, :]) + _dot(b_ref[...], w_ref[ka:, :])
    x1 = x_ref[...] + _rms(y, g1_ref[...])
    o_ref[...] = _xattn_tile(x1, g2_ref, wq_ref, kv_ref, wo_ref, g3_ref)


def _xattn_tile(x, g2_ref, wq_ref, kv_ref, wo_ref, g3_ref):
    d = x.shape[1]
    hd = d // XA_HEADS
    h = _rms(x, g2_ref[...]).astype(BF16)
    q = (_dot(h, wq_ref[...]) * (hd ** -0.5 * LOG2E)).astype(BF16)
    outs = []
    for hh in range(XA_HEADS):
        k = kv_ref[:, hh * hd:(hh + 1) * hd]
        v = kv_ref[:, d + hh * hd: d + (hh + 1) * hd]
        s = _dot_nt(q[:, hh * hd:(hh + 1) * hd], k)
        m = jnp.max(s, axis=-1, keepdims=True)
        p = jnp.exp2(s - m)
        l = jnp.sum(p, axis=-1, keepdims=True)
        outs.append((_dot(p.astype(BF16), v) / l).astype(BF16))
    o = jnp.concatenate(outs, axis=1)
    y = _dot(o, wo_ref[...])
    return x + _rms(y, g3_ref[...])


def _xattn(xf, g2, g3, wq, kv_mem, wo, layer, batch, tm, mix=None):
    n, d = xf.shape
    t = n // batch
    nt = t // tm
    n_mem = kv_mem.shape[1] // batch
    row = lambda w: pl.BlockSpec((tm, w), lambda b, i: (b * nt + i, 0))
    vec = _const_spec((1, d))
    xa_specs = [vec, _layer_spec(wq.shape, layer),
                pl.BlockSpec((None, n_mem, 2 * d), lambda b, i: (layer, b, 0)),
                _layer_spec(wo.shape, layer), vec]
    xa_args = (g2, wq, kv_mem, wo, g3)
    if mix is None:
        body, specs, args, name = _xattn_kernel, [row(d)] + xa_specs, (xf,) + xa_args, "xattn"
    else:
        a, b, w_out, g1 = mix
        body, name = _outproj_xattn_kernel, "outproj_xattn"
        specs = [row(d), row(a.shape[1]), row(b.shape[1]),
                 pl.BlockSpec(w_out.shape, lambda *_: (0, 0), pipeline_mode=pl.Buffered(1)),
                 vec] + xa_specs
        args = (xf, a, b, w_out, g1) + xa_args
    return pl.pallas_call(
        body,
        grid=(batch, nt),
        in_specs=specs,
        out_specs=row(d),
        out_shape=jax.ShapeDtypeStruct((n, d), F32),
        compiler_params=_cparams(2),
        name=name,
    )(*args)


FFN_TF = 256


def _ffn_kernel(x_ref, g4_ref, win_ref, wo_ref, g5_ref, o_ref, act_scr):
    dff = wo_ref.shape[0]
    x = x_ref[...]
    h = _rms(x, g4_ref[...]).astype(BF16)
    for c in range(dff // FFN_TF):
        cols = slice(c * FFN_TF, (c + 1) * FFN_TF)
        gt = _dot(h, win_ref[:, cols])
        up = _dot(h, win_ref[:, dff + c * FFN_TF: dff + (c + 1) * FFN_TF])
        act_scr[:, cols] = (gt * _sigmoid(gt) * up).astype(BF16)
    y = _dot(act_scr[...], wo_ref[...])
    o_ref[...] = x + _rms(y, g5_ref[...])


def _ffn(xf, g4, g5, w_in, w_out, layer, tm):
    n, d = xf.shape
    dff = w_out.shape[1]
    row = pl.BlockSpec((tm, d), lambda i: (i, 0))
    vec = _const_spec((1, d))
    return pl.pallas_call(
        _ffn_kernel,
        grid=(n // tm,),
        in_specs=[row, vec, _layer_spec(w_in.shape, layer), _layer_spec(w_out.shape, layer), vec],
        out_specs=row,
        out_shape=jax.ShapeDtypeStruct((n, d), F32),
        scratch_shapes=[pltpu.VMEM((tm, dff), BF16)],
        compiler_params=_cparams(1),
        name="ffn",
    )(xf, g4, w_in, w_out, g5)


def _relayout_even(w_in, w_uq, w_ukv, a_log, dt_bias):
    d = w_in.shape[0]
    qkvz = 3 * GDN_HEADS * GDN_DK + GDN_HEADS * GDN_DK
    ab = 2 * GDN_HEADS
    lat = w_in.shape[1] - qkvz - ab - MLA_ROPE
    pad = LANES - MLA_ROPE - ab
    w_in_re = jnp.concatenate(
        [w_in[:, :qkvz], w_in[:, qkvz + ab:qkvz + ab + lat], w_in[:, qkvz + ab + lat:],
         w_in[:, qkvz:qkvz + ab], jnp.zeros((d, pad), w_in.dtype)], axis=1).astype(BF16)
    r = w_uq.shape[0]
    uq = w_uq.reshape(r, MLA_HEADS, MLA_NOPE + MLA_ROPE)
    pe = jnp.pad(uq[:, :, MLA_NOPE:], ((0, 0), (0, 0), (0, LANES - MLA_ROPE)))
    wuq_re = jnp.concatenate([uq[:, :, :MLA_NOPE].reshape(r, -1), pe.reshape(r, -1)],
                             axis=1).astype(BF16)
    ukv = w_ukv.reshape(w_ukv.shape[0], MLA_HEADS, -1)
    wukv_re = jnp.concatenate([ukv[:, :, :MLA_NOPE].reshape(r, -1),
                               ukv[:, :, MLA_NOPE:].reshape(r, -1)], axis=1).astype(BF16)
    alog_row = jnp.zeros((1, LANES), F32).at[0, G_LANE:G_LANE + GDN_HEADS].set(a_log)
    dtb_row = jnp.zeros((1, LANES), F32).at[0, G_LANE:G_LANE + GDN_HEADS].set(dt_bias)
    return w_in_re, wuq_re, wukv_re, alog_row, dtb_row


def kernel(x, mem, positions, norm_gains, mem_norm, e_w_in, e_conv_w, e_a_log, e_dt_bias, e_o_norm,
           e_q_norm, e_kv_norm, e_w_uq, e_w_ukv, e_w_out, o_w_in, o_conv_w, o_conv_b, o_gate_a_w,
           o_gate_a_b, o_gate_x_w, o_gate_x_b, o_a_param, o_w_out, xa_wq, xa_wkv, xa_wo, ffn_w_in,
           ffn_w_out):
    batch, t, d = x.shape
    n = batch * t
    depth = norm_gains.shape[0]
    xf = x.reshape(n, d)
    gains = norm_gains.reshape(depth, -1, 1, d)

    cos_t, sin_t = _rope_tables(positions)
    kv_mem = _mem_kv(mem.reshape(-1, d), mem_norm, xa_wkv.astype(BF16))
    xa_wq_b, xa_wo_b = xa_wq.astype(BF16), xa_wo.astype(BF16)
    ffn_w_in_b, ffn_w_out_b = ffn_w_in.astype(BF16), ffn_w_out.astype(BF16)

    for layer in range(depth):
        g = gains[layer]
        mix = None
        if layer % 2 == 0:
            e = layer // 2
            w_in_re, wuq_re, wukv_re, alog_row, dtb_row = _relayout_even(
                e_w_in[e], e_w_uq[e], e_w_ukv[e], e_a_log[e], e_dt_bias[e])
            q, k, v, z, gb, qf, kf, vb = _even_inproj(
                xf, g[0], w_in_re, e_conv_w[e], alog_row, dtb_row,
                e_q_norm[e].reshape(1, -1), e_kv_norm[e].reshape(1, -1), wuq_re, wukv_re,
                cos_t, sin_t, batch, tm=512)
            out_a = _gdn(q, k, v, z, gb, e_o_norm[e].reshape(1, -1), batch)
            out_b = _mla(qf, kf, vb, batch)
            mix = (out_a, out_b, e_w_out[e].astype(BF16), g[1])
        else:
            o = layer // 2
            xf = _odd_mixer(xf, g[0], g[1], o_w_in[o].astype(BF16), o_conv_w[o],
                            o_conv_b[o].reshape(1, d), o_gate_a_w[o].astype(BF16),
                            o_gate_a_b[o].reshape(1, d), o_gate_x_w[o].astype(BF16),
                            o_gate_x_b[o].reshape(1, d), o_a_param[o].reshape(1, d),
                            o_w_out[o].astype(BF16), batch, tm=512)
        xf = _xattn(xf, g[2], g[3], xa_wq_b, kv_mem, xa_wo_b, layer, batch, tm=512, mix=mix)
        xf = _ffn(xf, g[4], g[5], ffn_w_in_b, ffn_w_out_b, layer, tm=512)
    return xf.reshape(batch, t, d)
```

```python
import functools
import math

import jax
import jax.numpy as jnp
from jax import lax
from jax.experimental import pallas as pl
from jax.experimental.pallas import tpu as pltpu

F32 = jnp.float32
BF16 = jnp.bfloat16

EPS = 1e-6
CHUNK = 64
GDN_HEADS = 4
GDN_DK = 128
MLA_HEADS = 4
MLA_NOPE = 128
MLA_ROPE = 64
MLA_SCALE = (MLA_NOPE + MLA_ROPE) ** -0.5
LOG2E = math.log2(math.e)
ROPE_BASE = 10000.0
LRU_BLOCKS = 4
LRU_C = 8.0
XA_HEADS = 4
CONV_W = 4

LANES = 128
CARRY_ROWS = 8
V7X_VMEM_LIMIT = 56 * 1024 * 1024

NT_DIMS = (((1,), (1,)), ((), ()))
TN_DIMS = (((0,), (0,)), ((), ()))


def _cparams(n_axes):
    return pltpu.CompilerParams(
        dimension_semantics=("arbitrary",) * n_axes,
        vmem_limit_bytes=V7X_VMEM_LIMIT)


def _dot(a, b):
    return jnp.dot(a, b, preferred_element_type=F32)


def _dot_nt(a, b):
    return lax.dot_general(a, b, NT_DIMS, preferred_element_type=F32)


def _rms(xf, g):
    ms = jnp.mean(xf * xf, axis=-1, keepdims=True)
    return xf * lax.rsqrt(ms + EPS) * g


def _sigmoid(x):
    return 1.0 / (1.0 + jnp.exp(-x))


def _softplus(x):
    return jnp.maximum(x, 0.0) + jnp.log1p(jnp.exp(-jnp.abs(x)))


def _const_spec(shape):
    nd = len(shape)
    return pl.BlockSpec(shape, lambda *_: (0,) * nd)


def _layer_spec(shape, layer):
    nd = len(shape) - 1
    return pl.BlockSpec((None,) + tuple(shape[1:]), lambda *_: (layer,) + (0,) * nd,
                        pipeline_mode=pl.Buffered(1))


def _rope_kernel(pos_ref, inv_ref, cos_ref, sin_ref):
    half = MLA_ROPE // 2
    per_row = LANES // half
    tr = pos_ref.shape[0]
    ang = pos_ref[...].astype(F32) * inv_ref[...]
    lane = lax.broadcasted_iota(jnp.int32, ang.shape, 1)
    for tbl, ref in ((jnp.cos(ang), cos_ref), (jnp.sin(ang), sin_ref)):
        for j in range(per_row):
            rep = jnp.where((lane >= half * j) & (lane < half * (j + 1)), tbl, 0.0)
            width = half
            while width < LANES:
                rep = rep + pltpu.roll(rep, width, axis=1)
                width *= 2
            ref[pl.ds(j, tr, stride=per_row), :] = rep


def _rope_tables(positions):
    n = positions.size
    half = MLA_ROPE // 2
    per_row = LANES // half
    inv_freq = ROPE_BASE ** (-jnp.arange(0, MLA_ROPE, 2, dtype=F32) / MLA_ROPE)
    inv_row = jnp.tile(inv_freq, per_row).reshape(1, LANES)
    pos_rep = jnp.repeat(positions.reshape(n // per_row, per_row), half, axis=1)
    rows = n // per_row
    tr = min(rows, 512)
    out_spec = pl.BlockSpec((tr * per_row, LANES), lambda i: (i, 0))
    return pl.pallas_call(
        _rope_kernel,
        grid=(rows // tr,),
        in_specs=[pl.BlockSpec((tr, LANES), lambda i: (i, 0)), _const_spec((1, LANES))],
        out_specs=[out_spec, out_spec],
        out_shape=[jax.ShapeDtypeStruct((n, LANES), F32)] * 2,
        compiler_params=_cparams(1),
        name="rope_tables",
    )(pos_rep, inv_row)


def _memkv_kernel(mem_ref, g_ref, w_ref, o_ref):
    mn = _rms(mem_ref[...], g_ref[...]).astype(BF16)
    o_ref[0] = _dot(mn, w_ref[0]).astype(BF16)


def _mem_kv(mem2d, mem_norm, wkv_bf16):
    depth, d, d2 = wkv_bf16.shape
    rows = mem2d.shape[0]
    tr = min(rows, 512)
    return pl.pallas_call(
        _memkv_kernel,
        grid=(depth, rows // tr),
        in_specs=[pl.BlockSpec((tr, d), lambda l, i: (i, 0)),
                  _const_spec((1, d)),
                  pl.BlockSpec((1, d, d2), lambda l, i: (l, 0, 0))],
        out_specs=pl.BlockSpec((1, tr, d2), lambda l, i: (l, i, 0)),
        out_shape=jax.ShapeDtypeStruct((depth, rows, d2), BF16),
        compiler_params=_cparams(2),
        name="mem_kv",
    )(mem2d, mem_norm.reshape(1, d), wkv_bf16)


def _rope_tile(x, cos, sin, lane):
    half = MLA_ROPE // 2
    up = pltpu.roll(x, LANES - half, axis=1)
    dn = pltpu.roll(x, half, axis=1)
    rot = jnp.where(lane < half, -up, dn)
    return jnp.where(lane < MLA_ROPE, x * cos + rot * sin, 0.0)


def _causal_conv4(x, cw_ref, cs, xpad, qpad):
    assert cw_ref.shape[0] == 4
    tm = x.shape[0]
    xpad[CARRY_ROWS:CARRY_ROWS + tm, cs] = x
    xm1 = xpad[pl.ds(CARRY_ROWS - 1, tm), cs]
    p = cw_ref[3:4, cs] * x + cw_ref[2:3, cs] * xm1
    q = cw_ref[1:2, cs] * x + cw_ref[0:1, cs] * xm1
    qpad[CARRY_ROWS:CARRY_ROWS + tm, cs] = q
    out = p + qpad[pl.ds(CARRY_ROWS - 2, tm), cs]
    xpad[0:CARRY_ROWS, cs] = x[tm - CARRY_ROWS:, :]
    qpad[0:CARRY_ROWS, cs] = q[tm - CARRY_ROWS:, :]
    return out


def _even_in_kernel(x_ref, g_ref, win_ref, cw_ref, alog_ref, dtb_ref, qn_ref, kvn_ref,
                    wuq_ref, wukv_ref, cos_ref, sin_ref,
                    q_out, k_out, v_out, z_out, gb_out, qf_out, kf_out, vb_out,
                    xpad, qpad):
    tm = x_ref.shape[0]
    qk_w = GDN_HEADS * GDN_DK
    qkv_w = 3 * qk_w

    @pl.when(pl.program_id(1) == 0)
    def _():
        xpad[0:CARRY_ROWS, :] = jnp.zeros((CARRY_ROWS, qkv_w), F32)
        qpad[0:CARRY_ROWS, :] = jnp.zeros((CARRY_ROWS, qkv_w), F32)

    h = _rms(x_ref[...], g_ref[...]).astype(BF16)
    lane = lax.broadcasted_iota(jnp.int32, (tm, LANES), 1)
    cos = cos_ref[...]
    sin = sin_ref[...]
    nope_w = MLA_HEADS * MLA_NOPE
    z_off = qkv_w
    cq_off = z_off + qk_w
    ckv_off = cq_off + qn_ref.shape[1]
    misc_off = ckv_off + kvn_ref.shape[1]
    gw = 2 * GDN_DK
    st = {}

    def proj_cols(lo, width):
        return _dot(h, win_ref[:, lo:lo + width])

    def mm_cq():
        st["cq"] = proj_cols(cq_off, qn_ref.shape[1])

    def ep_cq():
        st["cqb"] = _rms(st.pop("cq"), qn_ref[...] * (MLA_SCALE * LOG2E)).astype(BF16)

    def mm_ckv():
        st["ckv"] = proj_cols(ckv_off, kvn_ref.shape[1])

    def ep_ckv():
        st["ckvb"] = _rms(st.pop("ckv"), kvn_ref[...]).astype(BF16)

    def mm_misc():
        st["misc"] = proj_cols(misc_off, LANES)

    def ep_misc():
        misc = st.pop("misc")
        gdec = -jnp.exp(alog_ref[...]) * _softplus(misc + dtb_ref[...])
        beta = _sigmoid(misc)
        gb_out[...] = jnp.where((lane >= G_LANE) & (lane < G_LANE + GDN_HEADS), gdec,
                                jnp.where((lane >= B_LANE) & (lane < B_LANE + GDN_HEADS),
                                          beta, 0.0))
        st["kpe"] = _rope_tile(misc, cos, sin, lane).astype(BF16)

    def mm_uq():
        st["qf"] = _dot(st.pop("cqb"), wuq_ref[...])

    def ep_uq():
        qf = st.pop("qf")
        for hh in range(MLA_HEADS):
            base = 2 * LANES * hh
            qf_out[:, base:base + LANES] = qf[:, hh * LANES:(hh + 1) * LANES].astype(BF16)
            pe = qf[:, nope_w + hh * LANES: nope_w + (hh + 1) * LANES]
            qf_out[:, base + LANES:base + 2 * LANES] = _rope_tile(pe, cos, sin, lane).astype(BF16)

    def mm_ukv():
        st["kv"] = _dot(st.pop("ckvb"), wukv_ref[...])

    def ep_ukv():
        kv = st.pop("kv")
        for hh in range(MLA_HEADS):
            base = 2 * LANES * hh
            kf_out[:, base:base + LANES] = kv[:, hh * LANES:(hh + 1) * LANES].astype(BF16)
            kf_out[:, base + LANES:base + 2 * LANES] = st["kpe"]
        vb_out[...] = kv[:, nope_w:].astype(BF16)

    def mm_group(lo):
        def run():
            st[lo] = proj_cols(lo, gw)
        return run

    def ep_qkv(lo):
        def run():
            cs = slice(lo, lo + gw)
            acc = _causal_conv4(st.pop(lo), cw_ref, cs, xpad, qpad)
            act = acc * _sigmoid(acc)
            which, off = divmod(lo, qk_w)
            if which == 2:
                v_out[:, off:off + gw] = act.astype(BF16)
                return
            out, scale = (q_out, GDN_DK ** -0.5) if which == 0 else (k_out, 1.0)
            for sub in range(gw // GDN_DK):
                a = act[:, sub * GDN_DK:(sub + 1) * GDN_DK]
                nrm = a * (lax.rsqrt(jnp.sum(a * a, axis=-1, keepdims=True) + EPS) * scale)
                out[:, off + sub * GDN_DK: off + (sub + 1) * GDN_DK] = nrm.astype(BF16)
        return run

    def ep_z(lo):
        def run():
            z_out[:, lo - z_off: lo - z_off + gw] = st.pop(lo).astype(BF16)
        return run

    pairs = [(mm_cq, ep_cq), (mm_ckv, ep_ckv), (mm_misc, ep_misc), (mm_uq, ep_uq), (mm_ukv, ep_ukv)]
    pairs += [(mm_group(lo), ep_qkv(lo)) for lo in range(0, qkv_w, gw)]
    pairs += [(mm_group(lo), ep_z(lo)) for lo in range(z_off, cq_off, gw)]
    pairs[0][0]()
    for i, (_, ep) in enumerate(pairs):
        if i + 1 < len(pairs):
            pairs[i + 1][0]()
        ep()


def _even_inproj(xf, g0, w_in_re, conv_w, alog_row, dtb_row, q_norm, kv_norm, wuq_re, wukv_re,
                 cos_t, sin_t, batch, tm):
    n, d = xf.shape
    t = n // batch
    nt = t // tm
    e_in = w_in_re.shape[1]
    qkv_w = conv_w.shape[1]
    qk_w = qkv_w // 3
    row = lambda w: pl.BlockSpec((tm, w), lambda b, i: (b * nt + i, 0))
    outs = [(qk_w, BF16)] * 4 + [(LANES, F32), (2 * LANES * MLA_HEADS, BF16),
                                 (2 * LANES * MLA_HEADS, BF16), (LANES * MLA_HEADS, BF16)]
    return pl.pallas_call(
        _even_in_kernel,
        grid=(batch, nt),
        in_specs=[row(d), _const_spec((1, d)), _const_spec((d, e_in)), _const_spec((CONV_W, qkv_w)),
                  _const_spec((1, LANES)), _const_spec((1, LANES)),
                  _const_spec(q_norm.shape), _const_spec(kv_norm.shape),
                  _const_spec(wuq_re.shape), _const_spec(wukv_re.shape),
                  row(LANES), row(LANES)],
        out_specs=[row(w) for w, _ in outs],
        out_shape=[jax.ShapeDtypeStruct((n, w), dt) for w, dt in outs],
        scratch_shapes=[pltpu.VMEM((tm + CARRY_ROWS, qkv_w), F32),
                        pltpu.VMEM((tm + CARRY_ROWS, qkv_w), F32)],
        compiler_params=_cparams(2),
        name="even_inproj",
    )(xf, g0, w_in_re, conv_w, alog_row, dtb_row, q_norm, kv_norm, wuq_re, wukv_re, cos_t, sin_t)


SUPER = 2 * CHUNK
G_LANE = MLA_ROPE
B_LANE = MLA_ROPE + GDN_HEADS
GDN_PIPE = 2


def _gdn_kernel(q_ref, k_ref, v_ref, z_ref, gb_ref, on_ref, out_ref,
                gc_scr, gct_scr, gl_scr, u_scr, w_scr, a_scr, qe_scr, kd_scr, o_scr, s_scr):
    t = q_ref.shape[0]
    n_super = t // SUPER
    n_chunk = t // CHUNK

    gb = gb_ref[...]
    rowi = lax.broadcasted_iota(jnp.int32, gb.shape, 0) & (CHUNK - 1)
    gc = gb
    s = 1
    while s < CHUNK:
        gc = jnp.where(rowi >= s, gc + pltpu.roll(gc, s, axis=0), gc)
        s *= 2
    gc_scr[...] = gc
    gct_scr[...] = gc.T
    g3 = gc.reshape(n_chunk, CHUNK, LANES)
    gl_scr[...] = jnp.broadcast_to(g3[:, CHUNK - 1:CHUNK, :], g3.shape).reshape(t, LANES)

    ri = lax.broadcasted_iota(jnp.int32, (SUPER, SUPER), 0)
    ci = lax.broadcasted_iota(jnp.int32, (SUPER, SUPER), 1)
    same = (ri >= CHUNK) == (ci >= CHUNK)
    causal = same & (ri >= ci)
    strict = same & (ri > ci)
    eye = (ri == ci).astype(F32)

    heads = range(GDN_HEADS)
    hsl = [slice(hh * GDN_DK, (hh + 1) * GDN_DK) for hh in heads]
    lane_g = [slice(G_LANE + hh, G_LANE + hh + 1) for hh in heads]
    lane_b = [slice(B_LANE + hh, B_LANE + hh + 1) for hh in heads]

    def solve_stages(sc0):
        probs = []
        for sub in range(GDN_PIPE):
            rows = pl.ds(pl.multiple_of((sc0 + sub) * SUPER, SUPER), SUPER)
            probs += [(rows, hh) for hh in heads]
        kc = [k_ref[rows, hsl[hh]] for rows, hh in probs]
        qc = [q_ref[rows, hsl[hh]] for rows, hh in probs]
        gcol = [gc_scr[rows, lane_g[hh]] for rows, hh in probs]
        bcol = [gb_ref[rows, lane_b[hh]] for rows, hh in probs]
        grow = [gct_scr[lane_g[hh], rows] for rows, hh in probs]
        kk = [_dot_nt(k, k) for k in kc]
        qk = [_dot_nt(q, k) for q, k in zip(qc, kc)]
        yield
        decay = [jnp.where(causal, jnp.exp(jnp.where(causal, gc - gr, 0.0)), 0.0)
                 for gc, gr in zip(gcol, grow)]
        m = [jnp.where(strict, -(x * b * dc), 0.0) for x, b, dc in zip(kk, bcol, decay)]
        qacc = [eye + x for x in m]
        mb = [x.astype(BF16) for x in m]
        mj = [_dot(x, x) for x in mb]
        yield
        lvl = 2
        while lvl < CHUNK // 2:
            mb = [x.astype(BF16) for x in mj]
            r = [_dot(x, jnp.concatenate([x, qa.astype(BF16)], axis=1)) for x, qa in zip(mb, qacc)]
            yield
            mj = [x[:, :SUPER] for x in r]
            qacc = [qa + x[:, SUPER:] for qa, x in zip(qacc, r)]
            lvl *= 2
        last = [_dot(x.astype(BF16), qa.astype(BF16)) for qa, x in zip(qacc, mj)]
        yield
        qacc = [qa + x for qa, x in zip(qacc, last)]
        eg = [jnp.exp(gc) for gc in gcol]
        sol = []
        for i, (rows, hh) in enumerate(probs):
            vc = v_ref[rows, hsl[hh]].astype(F32)
            kf = kc[i].astype(F32)
            rhs = jnp.concatenate([vc * bcol[i], kf * (bcol[i] * eg[i])], axis=1).astype(BF16)
            sol.append(_dot(qacc[i].astype(BF16), rhs))
        yield
        for i, (rows, hh) in enumerate(probs):
            u_scr[rows, hsl[hh]] = sol[i][:, :GDN_DK]
            w_scr[rows, hsl[hh]] = sol[i][:, GDN_DK:].astype(BF16)
            aqk = qk[i] * decay[i]
            a_sh = pltpu.roll(aqk, CHUNK, axis=1)
            a_scr[rows, hsl[hh]] = jnp.where(ri < CHUNK, aqk, a_sh).astype(BF16)
            qe_scr[rows, hsl[hh]] = (qc[i].astype(F32) * eg[i]).astype(BF16)
            glast = gl_scr[rows, lane_g[hh]]
            kd_scr[rows, hsl[hh]] = (kc[i].astype(F32) * jnp.exp(glast - gcol[i])).astype(BF16)

    def scan_stages(sc0, st):
        for half in range(GDN_PIPE * SUPER // CHUNK):
            r0 = pl.multiple_of(sc0 * SUPER + half * CHUNK, CHUNK)
            rows = pl.ds(r0, CHUNK)
            sb = [x.astype(BF16) for x in st]
            t1 = [_dot(w_scr[rows, hsl[hh]], sb[hh]) for hh in heads]
            oq = [_dot(qe_scr[rows, hsl[hh]], sb[hh]) for hh in heads]
            yield
            vb = [(u_scr[rows, hsl[hh]] - t1[hh]).astype(BF16) for hh in heads]
            oa = [_dot(a_scr[rows, hsl[hh]][:, :CHUNK], vb[hh]) for hh in heads]
            kv = [lax.dot_general(kd_scr[rows, hsl[hh]], vb[hh], TN_DIMS,
                                  preferred_element_type=F32) for hh in heads]
            yield
            for hh in heads:
                gam = jnp.exp(gl_scr[pl.ds(r0, 1), lane_g[hh]])
                o_scr[rows, hsl[hh]] = oq[hh] + oa[hh]
                st[hh] = st[hh] * gam + kv[hh]

    def run_alternating(*gens):
        live = list(gens)
        while live:
            live = [g for g in live if next(g, StopIteration) is not StopIteration]

    run_alternating(solve_stages(0))
    s_scr[...] = jnp.zeros(s_scr.shape, F32)
    n_steps = n_super // GDN_PIPE

    def pipe_body(it, _):
        st = [s_scr[hh] for hh in heads]
        run_alternating(solve_stages((it + 1) * GDN_PIPE), scan_stages(it * GDN_PIPE, st))
        for hh in heads:
            s_scr[hh] = st[hh]
        return 0

    lax.fori_loop(0, n_steps - 1, pipe_body, 0)
    st = [s_scr[hh] for hh in heads]
    run_alternating(scan_stages((n_steps - 1) * GDN_PIPE, st))

    on = on_ref[...]
    for hh in range(GDN_HEADS):
        hs = slice(hh * GDN_DK, (hh + 1) * GDN_DK)
        z = z_ref[:, hs].astype(F32)
        out_ref[:, hs] = (_rms(o_scr[:, hs], on) * (z * _sigmoid(z))).astype(BF16)


def _gdn(q, k, v, z, gb, o_norm, batch):
    n, w = q.shape
    t = n // batch
    row = lambda ww: pl.BlockSpec((t, ww), lambda b: (b, 0))
    return pl.pallas_call(
        _gdn_kernel,
        grid=(batch,),
        in_specs=[row(w), row(w), row(w), row(w), row(LANES), _const_spec((1, GDN_DK))],
        out_specs=row(w),
        out_shape=jax.ShapeDtypeStruct((n, w), BF16),
        scratch_shapes=[pltpu.VMEM((t, LANES), F32),
                        pltpu.VMEM((LANES, t), F32),
                        pltpu.VMEM((t, LANES), F32),
                        pltpu.VMEM((t, w), F32),
                        pltpu.VMEM((t, w), BF16),
                        pltpu.VMEM((t, w), BF16),
                        pltpu.VMEM((t, w), BF16),
                        pltpu.VMEM((t, w), BF16),
                        pltpu.VMEM((t, w), F32),
                        pltpu.VMEM((GDN_HEADS, GDN_DK, GDN_DK), F32)],
        compiler_params=_cparams(1),
        name="gdn",
    )(q, k, v, z, gb, o_norm)


MLA_TQ = 256


def _mla_kernel(q_ref, k_ref, v_ref, o_ref):
    t = q_ref.shape[0]
    tq = MLA_TQ
    nq = t // tq
    hw = 2 * LANES
    heads = range(q_ref.shape[1] // hw)
    ri = lax.broadcasted_iota(jnp.int32, (tq, tq), 0) // CHUNK
    ci = lax.broadcasted_iota(jnp.int32, (tq, tq), 1) // CHUNK
    diag_mask = ci <= ri

    groups = [(i, h) for i in range(nq) for h in heads]

    def scores(i, h):
        kw = (i + 1) * tq
        return _dot_nt(q_ref[i * tq:(i + 1) * tq, h * hw:(h + 1) * hw],
                       k_ref[0:kw, h * hw:(h + 1) * hw])

    s_next = scores(*groups[0])
    for g, (i, h) in enumerate(groups):
        s = s_next
        if g + 1 < len(groups):
            s_next = scores(*groups[g + 1])
        kw = (i + 1) * tq
        s_diag = jnp.where(diag_mask, s[:, kw - tq:], -jnp.inf)
        s = s_diag if i == 0 else jnp.concatenate([s[:, :kw - tq], s_diag], axis=1)
        m = jnp.max(s, axis=-1, keepdims=True)
        p = jnp.exp2(s - m)
        l = jnp.sum(p, axis=-1, keepdims=True)
        pv = _dot(p.astype(BF16), v_ref[0:kw, h * LANES:(h + 1) * LANES])
        o_ref[i * tq:(i + 1) * tq, h * LANES:(h + 1) * LANES] = (pv / l).astype(BF16)


MLA_HEADS_PER_STEP = 2


def _mla(qf, kf, vb, batch):
    n = qf.shape[0]
    t = n // batch
    hps = MLA_HEADS_PER_STEP
    spec = lambda w: pl.BlockSpec((t, w * hps), lambda b, g: (b, g))
    return pl.pallas_call(
        _mla_kernel,
        grid=(batch, MLA_HEADS // hps),
        in_specs=[spec(2 * LANES), spec(2 * LANES), spec(LANES)],
        out_specs=spec(LANES),
        out_shape=jax.ShapeDtypeStruct((n, vb.shape[1]), BF16),
        compiler_params=_cparams(2),
        name="mla_attn",
    )(qf, kf, vb)


SCAN_PHASES = 8


def _gelu_tanh(x):
    c2 = 2.0 * math.sqrt(2.0 / math.pi) * LOG2E
    return x / (1.0 + jnp.exp2(x * (-c2 - (c2 * 0.044715) * (x * x))))


def _odd_kernel(x_ref, g0_ref, win_ref, cw_ref, cb_ref, gaw_ref, gab_ref, gxw_ref, gxb_ref,
                ap_ref, wout_ref, g1_ref, o_ref, xpad, qpad, hcar, a_scr, u_scr):
    tm, d = x_ref.shape
    bw = d // LRU_BLOCKS

    @pl.when(pl.program_id(1) == 0)
    def _():
        xpad[0:CARRY_ROWS, :] = jnp.zeros((CARRY_ROWS, d), F32)
        qpad[0:CARRY_ROWS, :] = jnp.zeros((CARRY_ROWS, d), F32)
        hcar[...] = jnp.zeros(hcar.shape, F32)

    x = x_ref[...]
    h = _rms(x, g0_ref[...]).astype(BF16)
    grp = tm // SCAN_PHASES
    slabs_per_block = bw // LANES
    rowg = lax.broadcasted_iota(jnp.int32, (grp, 1), 0)

    blocks = range(LRU_BLOCKS)
    csl = [slice(n * bw, (n + 1) * bw) for n in blocks]
    st = [dict() for _ in blocks]

    def projx(n):
        st[n]["xb"] = _dot(h, win_ref[:, csl[n]])

    def projy(n):
        st[n]["yb"] = _dot(h, win_ref[:, d + n * bw: d + (n + 1) * bw])

    def conv(n):
        cs, xb = csl[n], st[n].pop("xb")
        st[n]["xc"] = _causal_conv4(xb, cw_ref, cs, xpad, qpad) + cb_ref[:, cs]

    def gates(n):
        xcb = st[n]["xc"].astype(BF16)
        st[n]["ra"] = _dot(xcb, gaw_ref[n])
        st[n]["ia"] = _dot(xcb, gxw_ref[n])

    def recur(n):
        cs, xc = csl[n], st[n].pop("xc")
        r = _sigmoid(st[n].pop("ra") + gab_ref[:, cs])
        ig = _sigmoid(st[n].pop("ia") + gxb_ref[:, cs])
        log_a = (-LRU_C) * r * _softplus(-ap_ref[:, cs])
        a = jnp.exp(log_a)
        om = 1.0 - a * a
        u = jnp.where(om > 0.0, om * lax.rsqrt(om), 0.0) * (ig * xc)

        slabs = range(n * slabs_per_block, (n + 1) * slabs_per_block)
        for k, sl in enumerate(slabs):
            a_scr[sl] = a[:, k * LANES:(k + 1) * LANES]
            u_scr[sl] = u[:, k * LANES:(k + 1) * LANES]
        pa, pu = [], []
        for sl in slabs:
            ca = a_scr[sl, pl.ds(0, grp, stride=SCAN_PHASES), :]
            cu = u_scr[sl, pl.ds(0, grp, stride=SCAN_PHASES), :]
            la, lu = [ca], [cu]
            for j in range(1, SCAN_PHASES):
                aj = a_scr[sl, pl.ds(j, grp, stride=SCAN_PHASES), :]
                uj = u_scr[sl, pl.ds(j, grp, stride=SCAN_PHASES), :]
                cu = aj * cu + uj
                ca = aj * ca
                la.append(ca)
                lu.append(cu)
            pa.append(la)
            pu.append(lu)
        ga = jnp.concatenate([p[-1] for p in pa], axis=1)
        gu = jnp.concatenate([p[-1] for p in pu], axis=1)
        s = 1
        while s < grp:
            valid = rowg >= s
            gu = jnp.where(valid, ga * pltpu.roll(gu, s, axis=0) + gu, gu)
            ga = jnp.where(valid, ga * pltpu.roll(ga, s, axis=0), ga)
            s *= 2
        h_in = hcar[:, cs]
        h_end = ga * h_in + gu
        hcar[:, cs] = h_end[grp - 1:grp, :]
        h_prev = jnp.where(rowg == 0, h_in, pltpu.roll(h_end, 1, axis=0))
        for k, sl in enumerate(slabs):
            hp = h_prev[:, k * LANES:(k + 1) * LANES]
            for j in range(SCAN_PHASES):
                u_scr[sl, pl.ds(j, grp, stride=SCAN_PHASES), :] = pa[k][j] * hp + pu[k][j]
        st[n]["hs"] = jnp.concatenate([u_scr[sl] for sl in slabs], axis=1)

    def gate(n):
        st[n]["hg"] = (st[n].pop("hs") * _gelu_tanh(st[n].pop("yb"))).astype(BF16)

    def outp(n):
        st[n]["y"] = _dot(st[n].pop("hg"), wout_ref[csl[n], :])

    chain = (projx, conv, gates, recur, projy, gate, outp)
    for k in range(LRU_BLOCKS + len(chain) - 1):
        for s, stage in enumerate(chain):
            if 0 <= k - s < LRU_BLOCKS:
                stage(k - s)
    y = functools.reduce(lambda p, q: p + q, [st[n]["y"] for n in blocks])
    o_ref[...] = x + _rms(y, g1_ref[...])


def _odd_mixer(xf, g0, g1, w_in, conv_w, conv_b, gaw, gab, gxw, gxb, a_param, w_out, batch, tm):
    n, d = xf.shape
    t = n // batch
    nt = t // tm
    row = pl.BlockSpec((tm, d), lambda b, i: (b * nt + i, 0))
    vec = _const_spec((1, d))
    return pl.pallas_call(
        _odd_kernel,
        grid=(batch, nt),
        in_specs=[row, vec, _const_spec(w_in.shape), _const_spec(conv_w.shape), vec,
                  _const_spec(gaw.shape), vec, _const_spec(gxw.shape), vec, vec,
                  _const_spec(w_out.shape), vec],
        out_specs=row,
        out_shape=jax.ShapeDtypeStruct((n, d), F32),
        scratch_shapes=[pltpu.VMEM((tm + CARRY_ROWS, d), F32), pltpu.VMEM((tm + CARRY_ROWS, d), F32),
                        pltpu.VMEM((1, d), F32),
                        pltpu.VMEM((d // LANES, tm, LANES), F32),
                        pltpu.VMEM((d // LANES, tm, LANES), F32)],
        compiler_params=_cparams(2),
        name="odd_mixer",
    )(xf, g0, w_in, conv_w, conv_b, gaw, gab, gxw, gxb, a_param, w_out, g1)


def _xattn_kernel(x_ref, g2_ref, wq_ref, kv_ref, wo_ref, g3_ref, o_ref):
    o_ref[...] = _xattn_tile(x_ref[...], g2_ref, wq_ref, kv_ref, wo_ref, g3_ref)


def _outproj_xattn_kernel(x_ref, a_ref, b_ref, w_ref, g1_ref,
                          g2_ref, wq_ref, kv_ref, wo_ref, g3_ref, o_ref):
    ka = a_ref.shape[1]
    y = _dot(a_ref[...], w_ref[0:ka, :]) + _dot(b_ref[...], w_ref[ka:, :])
    x1 = x_ref[...] + _rms(y, g1_ref[...])
    o_ref[...] = _xattn_tile(x1, g2_ref, wq_ref, kv_ref, wo_ref, g3_ref)


def _xattn_tile(x, g2_ref, wq_ref, kv_ref, wo_ref, g3_ref):
    d = x.shape[1]
    hd = d // XA_HEADS
    h = _rms(x, g2_ref[...]).astype(BF16)
    q = (_dot(h, wq_ref[...]) * (hd ** -0.5 * LOG2E)).astype(BF16)
    def scores(hh):
        return _dot_nt(q[:, hh * hd:(hh + 1) * hd], kv_ref[:, hh * hd:(hh + 1) * hd])

    outs = []
    s_next = scores(0)
    for hh in range(XA_HEADS):
        s = s_next
        if hh + 1 < XA_HEADS:
            s_next = scores(hh + 1)
        v = kv_ref[:, d + hh * hd: d + (hh + 1) * hd]
        m = jnp.max(s, axis=-1, keepdims=True)
        p = jnp.exp2(s - m)
        l = jnp.sum(p, axis=-1, keepdims=True)
        outs.append((_dot(p.astype(BF16), v) / l).astype(BF16))
    o = jnp.concatenate(outs, axis=1)
    y = _dot(o, wo_ref[...])
    return x + _rms(y, g3_ref[...])


def _xattn(xf, g2, g3, wq, kv_mem, wo, layer, batch, tm, mix=None):
    n, d = xf.shape
    t = n // batch
    nt = t // tm
    n_mem = kv_mem.shape[1] // batch
    row = lambda w: pl.BlockSpec((tm, w), lambda b, i: (b * nt + i, 0))
    vec = _const_spec((1, d))
    xa_specs = [vec, _layer_spec(wq.shape, layer),
                pl.BlockSpec((None, n_mem, 2 * d), lambda b, i: (layer, b, 0)),
                _layer_spec(wo.shape, layer), vec]
    xa_args = (g2, wq, kv_mem, wo, g3)
    if mix is None:
        body, specs, args, name = _xattn_kernel, [row(d)] + xa_specs, (xf,) + xa_args, "xattn"
    else:
        a, b, w_out, g1 = mix
        body, name = _outproj_xattn_kernel, "outproj_xattn"
        specs = [row(d), row(a.shape[1]), row(b.shape[1]),
                 pl.BlockSpec(w_out.shape, lambda *_: (0, 0), pipeline_mode=pl.Buffered(1)),
                 vec] + xa_specs
        args = (xf, a, b, w_out, g1) + xa_args
    return pl.pallas_call(
        body,
        grid=(batch, nt),
        in_specs=specs,
        out_specs=row(d),
        out_shape=jax.ShapeDtypeStruct((n, d), F32),
        compiler_params=_cparams(2),
        name=name,
    )(*args)


FFN_TF = 256


def _ffn_kernel(x_ref, g4_ref, win_ref, wo_ref, g5_ref, o_ref, act_scr):
    dff = wo_ref.shape[0]
    x = x_ref[...]
    h = _rms(x, g4_ref[...]).astype(BF16)
    for c in range(dff // FFN_TF):
        cols = slice(c * FFN_TF, (c + 1) * FFN_TF)
        gt = _dot(h, win_ref[:, cols])
        up = _dot(h, win_ref[:, dff + c * FFN_TF: dff + (c + 1) * FFN_TF])
        act_scr[:, cols] = (gt * _sigmoid(gt) * up).astype(BF16)
    y = _dot(act_scr[...], wo_ref[...])
    o_ref[...] = x + _rms(y, g5_ref[...])


def _ffn(xf, g4, g5, w_in, w_out, layer, tm):
    n, d = xf.shape
    dff = w_out.shape[1]
    row = pl.BlockSpec((tm, d), lambda i: (i, 0))
    vec = _const_spec((1, d))
    return pl.pallas_call(
        _ffn_kernel,
        grid=(n // tm,),
        in_specs=[row, vec, _layer_spec(w_in.shape, layer), _layer_spec(w_out.shape, layer), vec],
        out_specs=row,
        out_shape=jax.ShapeDtypeStruct((n, d), F32),
        scratch_shapes=[pltpu.VMEM((tm, dff), BF16)],
        compiler_params=_cparams(1),
        name="ffn",
    )(xf, g4, w_in, w_out, g5)


def _relayout_even(w_in, w_uq, w_ukv, a_log, dt_bias):
    d = w_in.shape[0]
    qkvz = 3 * GDN_HEADS * GDN_DK + GDN_HEADS * GDN_DK
    ab = 2 * GDN_HEADS
    lat = w_in.shape[1] - qkvz - ab - MLA_ROPE
    pad = LANES - MLA_ROPE - ab
    w_in_re = jnp.concatenate(
        [w_in[:, :qkvz], w_in[:, qkvz + ab:qkvz + ab + lat], w_in[:, qkvz + ab + lat:],
         w_in[:, qkvz:qkvz + ab], jnp.zeros((d, pad), w_in.dtype)], axis=1).astype(BF16)
    r = w_uq.shape[0]
    uq = w_uq.reshape(r, MLA_HEADS, MLA_NOPE + MLA_ROPE)
    pe = jnp.pad(uq[:, :, MLA_NOPE:], ((0, 0), (0, 0), (0, LANES - MLA_ROPE)))
    wuq_re = jnp.concatenate([uq[:, :, :MLA_NOPE].reshape(r, -1), pe.reshape(r, -1)],
                             axis=1).astype(BF16)
    ukv = w_ukv.reshape(w_ukv.shape[0], MLA_HEADS, -1)
    wukv_re = jnp.concatenate([ukv[:, :, :MLA_NOPE].reshape(r, -1),
                               ukv[:, :, MLA_NOPE:].reshape(r, -1)], axis=1).astype(BF16)
    alog_row = jnp.zeros((1, LANES), F32).at[0, G_LANE:G_LANE + GDN_HEADS].set(a_log)
    dtb_row = jnp.zeros((1, LANES), F32).at[0, G_LANE:G_LANE + GDN_HEADS].set(dt_bias)
    return w_in_re, wuq_re, wukv_re, alog_row, dtb_row


def kernel(x, mem, positions, norm_gains, mem_norm, e_w_in, e_conv_w, e_a_log, e_dt_bias, e_o_norm,
           e_q_norm, e_kv_norm, e_w_uq, e_w_ukv, e_w_out, o_w_in, o_conv_w, o_conv_b, o_gate_a_w,
           o_gate_a_b, o_gate_x_w, o_gate_x_b, o_a_param, o_w_out, xa_wq, xa_wkv, xa_wo, ffn_w_in,
           ffn_w_out):
    batch, t, d = x.shape
    n = batch * t
    depth = norm_gains.shape[0]
    xf = x.reshape(n, d)
    gains = norm_gains.reshape(depth, -1, 1, d)

    cos_t, sin_t = _rope_tables(positions)
    kv_mem = _mem_kv(mem.reshape(-1, d), mem_norm, xa_wkv.astype(BF16))
    xa_wq_b, xa_wo_b = xa_wq.astype(BF16), xa_wo.astype(BF16)
    ffn_w_in_b, ffn_w_out_b = ffn_w_in.astype(BF16), ffn_w_out.astype(BF16)

    for layer in range(depth):
        g = gains[layer]
        mix = None
        if layer % 2 == 0:
            e = layer // 2
            w_in_re, wuq_re, wukv_re, alog_row, dtb_row = _relayout_even(
                e_w_in[e], e_w_uq[e], e_w_ukv[e], e_a_log[e], e_dt_bias[e])
            q, k, v, z, gb, qf, kf, vb = _even_inproj(
                xf, g[0], w_in_re, e_conv_w[e], alog_row, dtb_row,
                e_q_norm[e].reshape(1, -1), e_kv_norm[e].reshape(1, -1), wuq_re, wukv_re,
                cos_t, sin_t, batch, tm=512)
            out_a = _gdn(q, k, v, z, gb, e_o_norm[e].reshape(1, -1), batch)
            out_b = _mla(qf, kf, vb, batch)
            mix = (out_a, out_b, e_w_out[e].astype(BF16), g[1])
        else:
            o = layer // 2
            xf = _odd_mixer(xf, g[0], g[1], o_w_in[o].astype(BF16), o_conv_w[o],
                            o_conv_b[o].reshape(1, d), o_gate_a_w[o].astype(BF16),
                            o_gate_a_b[o].reshape(1, d), o_gate_x_w[o].astype(BF16),
                            o_gate_x_b[o].reshape(1, d), o_a_param[o].reshape(1, d),
                            o_w_out[o].astype(BF16), batch, tm=512)
        xf = _xattn(xf, g[2], g[3], xa_wq_b, kv_mem, xa_wo_b, layer, batch, tm=512, mix=mix)
        xf = _ffn(xf, g[4], g[5], ffn_w_in_b, ffn_w_out_b, layer, tm=1024)
    return xf.reshape(batch, t, d)
```

```python
import functools
import math

import jax
import jax.numpy as jnp
from jax import lax
from jax.experimental import pallas as pl
from jax.experimental.pallas import tpu as pltpu

F32 = jnp.float32
BF16 = jnp.bfloat16

EPS = 1e-6
CHUNK = 64
GDN_HEADS = 4
GDN_DK = 128
MLA_HEADS = 4
MLA_NOPE = 128
MLA_ROPE = 64
MLA_SCALE = (MLA_NOPE + MLA_ROPE) ** -0.5
LOG2E = math.log2(math.e)
ROPE_BASE = 10000.0
LRU_BLOCKS = 4
LRU_C = 8.0
XA_HEADS = 4
CONV_W = 4

LANES = 128
CARRY_ROWS = 8
V7X_VMEM_LIMIT = 56 * 1024 * 1024

NT_DIMS = (((1,), (1,)), ((), ()))
TN_DIMS = (((0,), (0,)), ((), ()))


def _cparams(n_axes):
    return pltpu.CompilerParams(
        dimension_semantics=("arbitrary",) * n_axes,
        vmem_limit_bytes=V7X_VMEM_LIMIT)


def _dot(a, b):
    return jnp.dot(a, b, preferred_element_type=F32)


def _dot_nt(a, b):
    return lax.dot_general(a, b, NT_DIMS, preferred_element_type=F32)


def _rms(xf, g):
    ms = jnp.mean(xf * xf, axis=-1, keepdims=True)
    return xf * lax.rsqrt(ms + EPS) * g


def _sigmoid(x):
    return 1.0 / (1.0 + jnp.exp(-x))


def _softplus(x):
    return jnp.maximum(x, 0.0) + jnp.log1p(jnp.exp(-jnp.abs(x)))


def _const_spec(shape):
    nd = len(shape)
    return pl.BlockSpec(shape, lambda *_: (0,) * nd)


def _layer_spec(shape, layer):
    nd = len(shape) - 1
    return pl.BlockSpec((None,) + tuple(shape[1:]), lambda *_: (layer,) + (0,) * nd,
                        pipeline_mode=pl.Buffered(1))


def _rope_kernel(pos_ref, inv_ref, cos_ref, sin_ref):
    half = MLA_ROPE // 2
    per_row = LANES // half
    tr = pos_ref.shape[0]
    ang = pos_ref[...].astype(F32) * inv_ref[...]
    lane = lax.broadcasted_iota(jnp.int32, ang.shape, 1)
    for tbl, ref in ((jnp.cos(ang), cos_ref), (jnp.sin(ang), sin_ref)):
        for j in range(per_row):
            rep = jnp.where((lane >= half * j) & (lane < half * (j + 1)), tbl, 0.0)
            width = half
            while width < LANES:
                rep = rep + pltpu.roll(rep, width, axis=1)
                width *= 2
            ref[pl.ds(j, tr, stride=per_row), :] = rep


def _rope_tables(positions):
    n = positions.size
    half = MLA_ROPE // 2
    per_row = LANES // half
    inv_freq = ROPE_BASE ** (-jnp.arange(0, MLA_ROPE, 2, dtype=F32) / MLA_ROPE)
    inv_row = jnp.tile(inv_freq, per_row).reshape(1, LANES)
    pos_rep = jnp.repeat(positions.reshape(n // per_row, per_row), half, axis=1)
    rows = n // per_row
    tr = min(rows, 512)
    out_spec = pl.BlockSpec((tr * per_row, LANES), lambda i: (i, 0))
    return pl.pallas_call(
        _rope_kernel,
        grid=(rows // tr,),
        in_specs=[pl.BlockSpec((tr, LANES), lambda i: (i, 0)), _const_spec((1, LANES))],
        out_specs=[out_spec, out_spec],
        out_shape=[jax.ShapeDtypeStruct((n, LANES), F32)] * 2,
        compiler_params=_cparams(1),
        name="rope_tables",
    )(pos_rep, inv_row)


def _memkv_kernel(mem_ref, g_ref, w_ref, o_ref):
    mn = _rms(mem_ref[...], g_ref[...]).astype(BF16)
    o_ref[0] = _dot(mn, w_ref[0]).astype(BF16)


def _mem_kv(mem2d, mem_norm, wkv_bf16):
    depth, d, d2 = wkv_bf16.shape
    rows = mem2d.shape[0]
    tr = min(rows, 512)
    return pl.pallas_call(
        _memkv_kernel,
        grid=(depth, rows // tr),
        in_specs=[pl.BlockSpec((tr, d), lambda l, i: (i, 0)),
                  _const_spec((1, d)),
                  pl.BlockSpec((1, d, d2), lambda l, i: (l, 0, 0))],
        out_specs=pl.BlockSpec((1, tr, d2), lambda l, i: (l, i, 0)),
        out_shape=jax.ShapeDtypeStruct((depth, rows, d2), BF16),
        compiler_params=_cparams(2),
        name="mem_kv",
    )(mem2d, mem_norm.reshape(1, d), wkv_bf16)


def _rope_tile(x, cos, sin, lane):
    half = MLA_ROPE // 2
    up = pltpu.roll(x, LANES - half, axis=1)
    dn = pltpu.roll(x, half, axis=1)
    rot = jnp.where(lane < half, -up, dn)
    return jnp.where(lane < MLA_ROPE, x * cos + rot * sin, 0.0)


def _causal_conv4(x, cw_ref, cs, xpad, qpad):
    assert cw_ref.shape[0] == 4
    tm = x.shape[0]
    xpad[CARRY_ROWS:CARRY_ROWS + tm, cs] = x
    xm1 = xpad[pl.ds(CARRY_ROWS - 1, tm), cs]
    p = cw_ref[3:4, cs] * x + cw_ref[2:3, cs] * xm1
    q = cw_ref[1:2, cs] * x + cw_ref[0:1, cs] * xm1
    qpad[CARRY_ROWS:CARRY_ROWS + tm, cs] = q
    out = p + qpad[pl.ds(CARRY_ROWS - 2, tm), cs]
    xpad[0:CARRY_ROWS, cs] = x[tm - CARRY_ROWS:, :]
    qpad[0:CARRY_ROWS, cs] = q[tm - CARRY_ROWS:, :]
    return out


def _even_in_kernel(x_ref, g_ref, win_ref, cw_ref, alog_ref, dtb_ref, qn_ref, kvn_ref,
                    wuq_ref, wukv_ref, cos_ref, sin_ref,
                    q_out, k_out, v_out, z_out, gb_out, qf_out, kf_out, vb_out,
                    xpad, qpad):
    tm = x_ref.shape[0]
    qk_w = GDN_HEADS * GDN_DK
    qkv_w = 3 * qk_w

    @pl.when(pl.program_id(1) == 0)
    def _():
        xpad[0:CARRY_ROWS, :] = jnp.zeros((CARRY_ROWS, qkv_w), F32)
        qpad[0:CARRY_ROWS, :] = jnp.zeros((CARRY_ROWS, qkv_w), F32)

    h = _rms(x_ref[...], g_ref[...]).astype(BF16)
    lane = lax.broadcasted_iota(jnp.int32, (tm, LANES), 1)
    cos = cos_ref[...]
    sin = sin_ref[...]
    nope_w = MLA_HEADS * MLA_NOPE
    z_off = qkv_w
    cq_off = z_off + qk_w
    ckv_off = cq_off + qn_ref.shape[1]
    misc_off = ckv_off + kvn_ref.shape[1]
    gw = 2 * GDN_DK
    st = {}

    def proj_cols(lo, width):
        return _dot(h, win_ref[:, lo:lo + width])

    def mm_cq():
        st["cq"] = proj_cols(cq_off, qn_ref.shape[1])

    def ep_cq():
        st["cqb"] = _rms(st.pop("cq"), qn_ref[...] * (MLA_SCALE * LOG2E)).astype(BF16)

    def mm_ckv():
        st["ckv"] = proj_cols(ckv_off, kvn_ref.shape[1])

    def ep_ckv():
        st["ckvb"] = _rms(st.pop("ckv"), kvn_ref[...]).astype(BF16)

    def mm_misc():
        st["misc"] = proj_cols(misc_off, LANES)

    def ep_misc():
        misc = st.pop("misc")
        gdec = -jnp.exp(alog_ref[...]) * _softplus(misc + dtb_ref[...])
        beta = _sigmoid(misc)
        gb_out[...] = jnp.where((lane >= G_LANE) & (lane < G_LANE + GDN_HEADS), gdec,
                                jnp.where((lane >= B_LANE) & (lane < B_LANE + GDN_HEADS),
                                          beta, 0.0))
        st["kpe"] = _rope_tile(misc, cos, sin, lane).astype(BF16)

    def mm_uq():
        st["qf"] = _dot(st.pop("cqb"), wuq_ref[...])

    def ep_uq():
        qf = st.pop("qf")
        for hh in range(MLA_HEADS):
            base = 2 * LANES * hh
            qf_out[:, base:base + LANES] = qf[:, hh * LANES:(hh + 1) * LANES].astype(BF16)
            pe = qf[:, nope_w + hh * LANES: nope_w + (hh + 1) * LANES]
            qf_out[:, base + LANES:base + 2 * LANES] = _rope_tile(pe, cos, sin, lane).astype(BF16)

    def mm_ukv():
        st["kv"] = _dot(st.pop("ckvb"), wukv_ref[...])

    def ep_ukv():
        kv = st.pop("kv")
        for hh in range(MLA_HEADS):
            base = 2 * LANES * hh
            kf_out[:, base:base + LANES] = kv[:, hh * LANES:(hh + 1) * LANES].astype(BF16)
            kf_out[:, base + LANES:base + 2 * LANES] = st["kpe"]
        vb_out[...] = kv[:, nope_w:].astype(BF16)

    def mm_group(lo):
        def run():
            st[lo] = proj_cols(lo, gw)
        return run

    def ep_qkv(lo):
        def run():
            cs = slice(lo, lo + gw)
            acc = _causal_conv4(st.pop(lo), cw_ref, cs, xpad, qpad)
            act = acc * _sigmoid(acc)
            which, off = divmod(lo, qk_w)
            if which == 2:
                v_out[:, off:off + gw] = act.astype(BF16)
                return
            out, scale = (q_out, GDN_DK ** -0.5) if which == 0 else (k_out, 1.0)
            for sub in range(gw // GDN_DK):
                a = act[:, sub * GDN_DK:(sub + 1) * GDN_DK]
                nrm = a * (lax.rsqrt(jnp.sum(a * a, axis=-1, keepdims=True) + EPS) * scale)
                out[:, off + sub * GDN_DK: off + (sub + 1) * GDN_DK] = nrm.astype(BF16)
        return run

    def ep_z(lo):
        def run():
            z_out[:, lo - z_off: lo - z_off + gw] = st.pop(lo).astype(BF16)
        return run

    pairs = [(mm_cq, ep_cq), (mm_ckv, ep_ckv), (mm_misc, ep_misc), (mm_uq, ep_uq), (mm_ukv, ep_ukv)]
    pairs += [(mm_group(lo), ep_qkv(lo)) for lo in range(0, qkv_w, gw)]
    pairs += [(mm_group(lo), ep_z(lo)) for lo in range(z_off, cq_off, gw)]
    pairs[0][0]()
    for i, (_, ep) in enumerate(pairs):
        if i + 1 < len(pairs):
            pairs[i + 1][0]()
        ep()


def _even_inproj(xf, g0, w_in_re, conv_w, alog_row, dtb_row, q_norm, kv_norm, wuq_re, wukv_re,
                 cos_t, sin_t, batch, tm):
    n, d = xf.shape
    t = n // batch
    nt = t // tm
    e_in = w_in_re.shape[1]
    qkv_w = conv_w.shape[1]
    qk_w = qkv_w // 3
    row = lambda w: pl.BlockSpec((tm, w), lambda b, i: (b * nt + i, 0))
    outs = [(qk_w, BF16)] * 4 + [(LANES, F32), (2 * LANES * MLA_HEADS, BF16),
                                 (2 * LANES * MLA_HEADS, BF16), (LANES * MLA_HEADS, BF16)]
    return pl.pallas_call(
        _even_in_kernel,
        grid=(batch, nt),
        in_specs=[row(d), _const_spec((1, d)), _const_spec((d, e_in)), _const_spec((CONV_W, qkv_w)),
                  _const_spec((1, LANES)), _const_spec((1, LANES)),
                  _const_spec(q_norm.shape), _const_spec(kv_norm.shape),
                  _const_spec(wuq_re.shape), _const_spec(wukv_re.shape),
                  row(LANES), row(LANES)],
        out_specs=[row(w) for w, _ in outs],
        out_shape=[jax.ShapeDtypeStruct((n, w), dt) for w, dt in outs],
        scratch_shapes=[pltpu.VMEM((tm + CARRY_ROWS, qkv_w), F32),
                        pltpu.VMEM((tm + CARRY_ROWS, qkv_w), F32)],
        compiler_params=_cparams(2),
        name="even_inproj",
    )(xf, g0, w_in_re, conv_w, alog_row, dtb_row, q_norm, kv_norm, wuq_re, wukv_re, cos_t, sin_t)


SUPER = 2 * CHUNK
G_LANE = MLA_ROPE
B_LANE = MLA_ROPE + GDN_HEADS
GDN_PIPE = 2


def _gdn_kernel(q_ref, k_ref, v_ref, z_ref, gb_ref, on_ref, out_ref,
                gc_scr, gct_scr, gl_scr, u_scr, w_scr, a_scr, qe_scr, kd_scr, s_scr):
    t = q_ref.shape[0]
    n_super = t // SUPER
    n_chunk = t // CHUNK

    gb = gb_ref[...]
    rowi = lax.broadcasted_iota(jnp.int32, gb.shape, 0) & (CHUNK - 1)
    gc = gb
    s = 1
    while s < CHUNK:
        gc = jnp.where(rowi >= s, gc + pltpu.roll(gc, s, axis=0), gc)
        s *= 2
    gc_scr[...] = gc
    gct_scr[...] = gc.T
    g3 = gc.reshape(n_chunk, CHUNK, LANES)
    gl_scr[...] = jnp.broadcast_to(g3[:, CHUNK - 1:CHUNK, :], g3.shape).reshape(t, LANES)

    ri = lax.broadcasted_iota(jnp.int32, (SUPER, SUPER), 0)
    ci = lax.broadcasted_iota(jnp.int32, (SUPER, SUPER), 1)
    same = (ri >= CHUNK) == (ci >= CHUNK)
    causal = same & (ri >= ci)
    strict = same & (ri > ci)
    eye = (ri == ci).astype(F32)

    on = on_ref[...]
    heads = range(GDN_HEADS)
    hsl = [slice(hh * GDN_DK, (hh + 1) * GDN_DK) for hh in heads]
    lane_g = [slice(G_LANE + hh, G_LANE + hh + 1) for hh in heads]
    lane_b = [slice(B_LANE + hh, B_LANE + hh + 1) for hh in heads]

    def solve_stages(sc0):
        probs = []
        for sub in range(GDN_PIPE):
            rows = pl.ds(pl.multiple_of((sc0 + sub) * SUPER, SUPER), SUPER)
            probs += [(rows, hh) for hh in heads]
        kc = [k_ref[rows, hsl[hh]] for rows, hh in probs]
        qc = [q_ref[rows, hsl[hh]] for rows, hh in probs]
        gcol = [gc_scr[rows, lane_g[hh]] for rows, hh in probs]
        bcol = [gb_ref[rows, lane_b[hh]] for rows, hh in probs]
        grow = [gct_scr[lane_g[hh], rows] for rows, hh in probs]
        kk = [_dot_nt(k, k) for k in kc]
        qk = [_dot_nt(q, k) for q, k in zip(qc, kc)]
        yield
        decay = [jnp.where(causal, jnp.exp(jnp.where(causal, gc - gr, 0.0)), 0.0)
                 for gc, gr in zip(gcol, grow)]
        m = [jnp.where(strict, -(x * b * dc), 0.0) for x, b, dc in zip(kk, bcol, decay)]
        qacc = [eye + x for x in m]
        mb = [x.astype(BF16) for x in m]
        mj = [_dot(x, x) for x in mb]
        yield
        lvl = 2
        while lvl < CHUNK // 2:
            mb = [x.astype(BF16) for x in mj]
            r = [_dot(x, jnp.concatenate([x, qa.astype(BF16)], axis=1)) for x, qa in zip(mb, qacc)]
            yield
            mj = [x[:, :SUPER] for x in r]
            qacc = [qa + x[:, SUPER:] for qa, x in zip(qacc, r)]
            lvl *= 2
        last = [_dot(x.astype(BF16), qa.astype(BF16)) for qa, x in zip(qacc, mj)]
        yield
        qacc = [qa + x for qa, x in zip(qacc, last)]
        eg = [jnp.exp(gc) for gc in gcol]
        sol = []
        for i, (rows, hh) in enumerate(probs):
            vc = v_ref[rows, hsl[hh]].astype(F32)
            kf = kc[i].astype(F32)
            rhs = jnp.concatenate([vc * bcol[i], kf * (bcol[i] * eg[i])], axis=1).astype(BF16)
            sol.append(_dot(qacc[i].astype(BF16), rhs))
        yield
        for i, (rows, hh) in enumerate(probs):
            u_scr[rows, hsl[hh]] = sol[i][:, :GDN_DK]
            w_scr[rows, hsl[hh]] = sol[i][:, GDN_DK:].astype(BF16)
            aqk = qk[i] * decay[i]
            a_sh = pltpu.roll(aqk, CHUNK, axis=1)
            a_scr[rows, hsl[hh]] = jnp.where(ri < CHUNK, aqk, a_sh).astype(BF16)
            qe_scr[rows, hsl[hh]] = (qc[i].astype(F32) * eg[i]).astype(BF16)
            glast = gl_scr[rows, lane_g[hh]]
            kd_scr[rows, hsl[hh]] = (kc[i].astype(F32) * jnp.exp(glast - gcol[i])).astype(BF16)

    def scan_stages(sc0, st):
        for half in range(GDN_PIPE * SUPER // CHUNK):
            r0 = pl.multiple_of(sc0 * SUPER + half * CHUNK, CHUNK)
            rows = pl.ds(r0, CHUNK)
            sb = [x.astype(BF16) for x in st]
            t1 = [_dot(w_scr[rows, hsl[hh]], sb[hh]) for hh in heads]
            oq = [_dot(qe_scr[rows, hsl[hh]], sb[hh]) for hh in heads]
            yield
            vb = [(u_scr[rows, hsl[hh]] - t1[hh]).astype(BF16) for hh in heads]
            oa = [_dot(a_scr[rows, hsl[hh]][:, :CHUNK], vb[hh]) for hh in heads]
            kv = [lax.dot_general(kd_scr[rows, hsl[hh]], vb[hh], TN_DIMS,
                                  preferred_element_type=F32) for hh in heads]
            yield
            for hh in heads:
                gam = jnp.exp(gl_scr[pl.ds(r0, 1), lane_g[hh]])
                st[hh] = st[hh] * gam + kv[hh]
                z = z_ref[rows, hsl[hh]].astype(F32)
                out_ref[rows, hsl[hh]] = (
                    _rms(oq[hh] + oa[hh], on) * (z * _sigmoid(z))).astype(BF16)

    _round_robin([solve_stages(0)])
    s_scr[...] = jnp.zeros(s_scr.shape, F32)
    n_steps = n_super // GDN_PIPE

    def pipe_body(it, _):
        st = [s_scr[hh] for hh in heads]
        _round_robin([solve_stages((it + 1) * GDN_PIPE), scan_stages(it * GDN_PIPE, st)])
        for hh in heads:
            s_scr[hh] = st[hh]
        return 0

    lax.fori_loop(0, n_steps - 1, pipe_body, 0)
    st = [s_scr[hh] for hh in heads]
    _round_robin([scan_stages((n_steps - 1) * GDN_PIPE, st)])


def _gdn(q, k, v, z, gb, o_norm, batch):
    n, w = q.shape
    t = n // batch
    row = lambda ww: pl.BlockSpec((t, ww), lambda b: (b, 0))
    return pl.pallas_call(
        _gdn_kernel,
        grid=(batch,),
        in_specs=[row(w), row(w), row(w), row(w), row(LANES), _const_spec((1, GDN_DK))],
        out_specs=row(w),
        out_shape=jax.ShapeDtypeStruct((n, w), BF16),
        scratch_shapes=[pltpu.VMEM((t, LANES), F32),
                        pltpu.VMEM((LANES, t), F32),
                        pltpu.VMEM((t, LANES), F32),
                        pltpu.VMEM((t, w), F32),
                        pltpu.VMEM((t, w), BF16),
                        pltpu.VMEM((t, w), BF16),
                        pltpu.VMEM((t, w), BF16),
                        pltpu.VMEM((t, w), BF16),
                        pltpu.VMEM((GDN_HEADS, GDN_DK, GDN_DK), F32)],
        compiler_params=_cparams(1),
        name="gdn",
    )(q, k, v, z, gb, o_norm)


MLA_TQ = 256


def _mla_kernel(q_ref, k_ref, v_ref, o_ref):
    t = q_ref.shape[0]
    tq = MLA_TQ
    nq = t // tq
    hw = 2 * LANES
    heads = range(q_ref.shape[1] // hw)
    ri = lax.broadcasted_iota(jnp.int32, (tq, tq), 0) // CHUNK
    ci = lax.broadcasted_iota(jnp.int32, (tq, tq), 1) // CHUNK
    diag_mask = ci <= ri

    groups = [(i, h) for i in range(nq) for h in heads]

    def scores(i, h):
        kw = (i + 1) * tq
        return _dot_nt(q_ref[i * tq:(i + 1) * tq, h * hw:(h + 1) * hw],
                       k_ref[0:kw, h * hw:(h + 1) * hw])

    s_next = scores(*groups[0])
    for g, (i, h) in enumerate(groups):
        s = s_next
        if g + 1 < len(groups):
            s_next = scores(*groups[g + 1])
        kw = (i + 1) * tq
        s_diag = jnp.where(diag_mask, s[:, kw - tq:], -jnp.inf)
        s = s_diag if i == 0 else jnp.concatenate([s[:, :kw - tq], s_diag], axis=1)
        m = jnp.max(s, axis=-1, keepdims=True)
        p = jnp.exp2(s - m)
        l = jnp.sum(p, axis=-1, keepdims=True)
        pv = _dot(p.astype(BF16), v_ref[0:kw, h * LANES:(h + 1) * LANES])
        o_ref[i * tq:(i + 1) * tq, h * LANES:(h + 1) * LANES] = (pv / l).astype(BF16)


MLA_HEADS_PER_STEP = 2


def _mla(qf, kf, vb, batch):
    n = qf.shape[0]
    t = n // batch
    hps = MLA_HEADS_PER_STEP
    spec = lambda w: pl.BlockSpec((t, w * hps), lambda b, g: (b, g))
    return pl.pallas_call(
        _mla_kernel,
        grid=(batch, MLA_HEADS // hps),
        in_specs=[spec(2 * LANES), spec(2 * LANES), spec(LANES)],
        out_specs=spec(LANES),
        out_shape=jax.ShapeDtypeStruct((n, vb.shape[1]), BF16),
        compiler_params=_cparams(2),
        name="mla_attn",
    )(qf, kf, vb)


SCAN_PHASES = 8


def _gelu_tanh(x):
    c2 = 2.0 * math.sqrt(2.0 / math.pi) * LOG2E
    return x / (1.0 + jnp.exp2(x * (-c2 - (c2 * 0.044715) * (x * x))))


def _odd_kernel(x_ref, g0_ref, win_ref, cw_ref, cb_ref, gaw_ref, gab_ref, gxw_ref, gxb_ref,
                ap_ref, wout_ref, g1_ref, o_ref, xpad, qpad, hcar, a_scr, u_scr):
    tm, d = x_ref.shape
    bw = d // LRU_BLOCKS

    @pl.when(pl.program_id(1) == 0)
    def _():
        xpad[0:CARRY_ROWS, :] = jnp.zeros((CARRY_ROWS, d), F32)
        qpad[0:CARRY_ROWS, :] = jnp.zeros((CARRY_ROWS, d), F32)
        hcar[...] = jnp.zeros(hcar.shape, F32)

    x = x_ref[...]
    h = _rms(x, g0_ref[...]).astype(BF16)
    grp = tm // SCAN_PHASES
    slabs_per_block = bw // LANES
    rowg = lax.broadcasted_iota(jnp.int32, (grp, 1), 0)

    blocks = range(LRU_BLOCKS)
    csl = [slice(n * bw, (n + 1) * bw) for n in blocks]
    st = [dict() for _ in blocks]

    def projx(n):
        st[n]["xb"] = _dot(h, win_ref[:, csl[n]])

    def projy(n):
        st[n]["yb"] = _dot(h, win_ref[:, d + n * bw: d + (n + 1) * bw])

    def conv(n):
        cs, xb = csl[n], st[n].pop("xb")
        st[n]["xc"] = _causal_conv4(xb, cw_ref, cs, xpad, qpad) + cb_ref[:, cs]

    def gates(n):
        xcb = st[n]["xc"].astype(BF16)
        st[n]["ra"] = _dot(xcb, gaw_ref[n])
        st[n]["ia"] = _dot(xcb, gxw_ref[n])

    def recur(n):
        cs, xc = csl[n], st[n].pop("xc")
        r = _sigmoid(st[n].pop("ra") + gab_ref[:, cs])
        ig = _sigmoid(st[n].pop("ia") + gxb_ref[:, cs])
        a = jnp.exp2(r * ((-LRU_C * LOG2E) * _softplus(-ap_ref[:, cs])))
        om = 1.0 - a * a
        u = jnp.where(om > 0.0, om * lax.rsqrt(om), 0.0) * (ig * xc)

        slabs = range(n * slabs_per_block, (n + 1) * slabs_per_block)
        for k, sl in enumerate(slabs):
            a_scr[sl] = a[:, k * LANES:(k + 1) * LANES]
            u_scr[sl] = u[:, k * LANES:(k + 1) * LANES]
        pa, pu = [], []
        for sl in slabs:
            ca = a_scr[sl, pl.ds(0, grp, stride=SCAN_PHASES), :]
            cu = u_scr[sl, pl.ds(0, grp, stride=SCAN_PHASES), :]
            la, lu = [ca], [cu]
            for j in range(1, SCAN_PHASES):
                aj = a_scr[sl, pl.ds(j, grp, stride=SCAN_PHASES), :]
                uj = u_scr[sl, pl.ds(j, grp, stride=SCAN_PHASES), :]
                cu = aj * cu + uj
                ca = aj * ca
                la.append(ca)
                lu.append(cu)
            pa.append(la)
            pu.append(lu)
        ga = jnp.concatenate([p[-1] for p in pa], axis=1)
        gu = jnp.concatenate([p[-1] for p in pu], axis=1)
        s = 1
        while s < grp:
            valid = rowg >= s
            gu = jnp.where(valid, ga * pltpu.roll(gu, s, axis=0) + gu, gu)
            ga = jnp.where(valid, ga * pltpu.roll(ga, s, axis=0), ga)
            s *= 2
        h_in = hcar[:, cs]
        h_end = ga * h_in + gu
        hcar[:, cs] = h_end[grp - 1:grp, :]
        h_prev = jnp.where(rowg == 0, h_in, pltpu.roll(h_end, 1, axis=0))
        for k, sl in enumerate(slabs):
            hp = h_prev[:, k * LANES:(k + 1) * LANES]
            for j in range(SCAN_PHASES):
                u_scr[sl, pl.ds(j, grp, stride=SCAN_PHASES), :] = pa[k][j] * hp + pu[k][j]
        st[n]["hs"] = jnp.concatenate([u_scr[sl] for sl in slabs], axis=1)

    def gate(n):
        st[n]["hg"] = (st[n].pop("hs") * _gelu_tanh(st[n].pop("yb"))).astype(BF16)

    def outp(n):
        st[n]["y"] = _dot(st[n].pop("hg"), wout_ref[csl[n], :])

    chain = (projx, conv, gates, recur, projy, gate, outp)
    for k in range(LRU_BLOCKS + len(chain) - 1):
        for s, stage in enumerate(chain):
            if 0 <= k - s < LRU_BLOCKS:
                stage(k - s)
    y = functools.reduce(lambda p, q: p + q, [st[n]["y"] for n in blocks])
    o_ref[...] = x + _rms(y, g1_ref[...])


def _odd_mixer(xf, g0, g1, w_in, conv_w, conv_b, gaw, gab, gxw, gxb, a_param, w_out, batch, tm):
    n, d = xf.shape
    t = n // batch
    nt = t // tm
    row = pl.BlockSpec((tm, d), lambda b, i: (b * nt + i, 0))
    vec = _const_spec((1, d))
    return pl.pallas_call(
        _odd_kernel,
        grid=(batch, nt),
        in_specs=[row, vec, _const_spec(w_in.shape), _const_spec(conv_w.shape), vec,
                  _const_spec(gaw.shape), vec, _const_spec(gxw.shape), vec, vec,
                  _const_spec(w_out.shape), vec],
        out_specs=row,
        out_shape=jax.ShapeDtypeStruct((n, d), F32),
        scratch_shapes=[pltpu.VMEM((tm + CARRY_ROWS, d), F32), pltpu.VMEM((tm + CARRY_ROWS, d), F32),
                        pltpu.VMEM((1, d), F32),
                        pltpu.VMEM((d // LANES, tm, LANES), F32),
                        pltpu.VMEM((d // LANES, tm, LANES), F32)],
        compiler_params=_cparams(2),
        name="odd_mixer",
    )(xf, g0, w_in, conv_w, conv_b, gaw, gab, gxw, gxb, a_param, w_out, g1)


XA_SPLIT = 2


def _round_robin(gens):
    live = list(gens)
    while live:
        live = [g for g in live if next(g, StopIteration) is not StopIteration]


def _xattn_kernel(x_ref, g2_ref, wq_ref, kv_ref, wo_ref, g3_ref, o_ref):
    tm = x_ref.shape[0]
    rs = tm // XA_SPLIT
    _round_robin([_xattn_stages(slice(k * rs, (k + 1) * rs), x_ref, None, g2_ref, wq_ref, kv_ref,
                                wo_ref, g3_ref, o_ref) for k in range(XA_SPLIT)])


def _outproj_xattn_kernel(x_ref, a_ref, b_ref, w_ref, g1_ref,
                          g2_ref, wq_ref, kv_ref, wo_ref, g3_ref, o_ref):
    tm = x_ref.shape[0]
    rs = tm // XA_SPLIT
    _round_robin([_xattn_stages(slice(k * rs, (k + 1) * rs), x_ref, (a_ref, b_ref, w_ref, g1_ref),
                                g2_ref, wq_ref, kv_ref, wo_ref, g3_ref, o_ref)
                  for k in range(XA_SPLIT)])


def _xattn_stages(rows, x_ref, mix, g2_ref, wq_ref, kv_ref, wo_ref, g3_ref, o_ref):
    x = x_ref[rows, :]
    d = x.shape[1]
    hd = d // XA_HEADS
    if mix is not None:
        a_ref, b_ref, w_ref, g1_ref = mix
        ka = a_ref.shape[1]
        y = _dot(a_ref[rows, :], w_ref[0:ka, :]) + _dot(b_ref[rows, :], w_ref[ka:, :])
        yield
        x = x + _rms(y, g1_ref[...])
    h = _rms(x, g2_ref[...]).astype(BF16)
    q = _dot(h, wq_ref[...])
    yield
    q = (q * (hd ** -0.5 * LOG2E)).astype(BF16)
    s = [_dot_nt(q[:, hh * hd:(hh + 1) * hd], kv_ref[:, hh * hd:(hh + 1) * hd])
         for hh in range(XA_HEADS)]
    yield
    outs = []
    for hh in range(XA_HEADS):
        m = jnp.max(s[hh], axis=-1, keepdims=True)
        p = jnp.exp2(s[hh] - m)
        l = jnp.sum(p, axis=-1, keepdims=True)
        pv = _dot(p.astype(BF16), kv_ref[:, d + hh * hd: d + (hh + 1) * hd])
        outs.append((pv / l).astype(BF16))
    yield
    y = _dot(jnp.concatenate(outs, axis=1), wo_ref[...])
    yield
    o_ref[rows, :] = x + _rms(y, g3_ref[...])


def _xattn(xf, g2, g3, wq, kv_mem, wo, layer, batch, tm, mix=None):
    n, d = xf.shape
    t = n // batch
    nt = t // tm
    n_mem = kv_mem.shape[1] // batch
    row = lambda w: pl.BlockSpec((tm, w), lambda b, i: (b * nt + i, 0))
    vec = _const_spec((1, d))
    xa_specs = [vec, _layer_spec(wq.shape, layer),
                pl.BlockSpec((None, n_mem, 2 * d), lambda b, i: (layer, b, 0)),
                _layer_spec(wo.shape, layer), vec]
    xa_args = (g2, wq, kv_mem, wo, g3)
    if mix is None:
        body, specs, args, name = _xattn_kernel, [row(d)] + xa_specs, (xf,) + xa_args, "xattn"
    else:
        a, b, w_out, g1 = mix
        body, name = _outproj_xattn_kernel, "outproj_xattn"
        specs = [row(d), row(a.shape[1]), row(b.shape[1]),
                 pl.BlockSpec(w_out.shape, lambda *_: (0, 0), pipeline_mode=pl.Buffered(1)),
                 vec] + xa_specs
        args = (xf, a, b, w_out, g1) + xa_args
    return pl.pallas_call(
        body,
        grid=(batch, nt),
        in_specs=specs,
        out_specs=row(d),
        out_shape=jax.ShapeDtypeStruct((n, d), F32),
        compiler_params=_cparams(2),
        name=name,
    )(*args)


FFN_TF = 256


def _ffn_kernel(x_ref, g4_ref, win_ref, wo_ref, g5_ref, o_ref, act_scr):
    dff = wo_ref.shape[0]
    x = x_ref[...]
    h = _rms(x, g4_ref[...]).astype(BF16)
    for c in range(dff // FFN_TF):
        cols = slice(c * FFN_TF, (c + 1) * FFN_TF)
        gt = _dot(h, win_ref[:, cols])
        up = _dot(h, win_ref[:, dff + c * FFN_TF: dff + (c + 1) * FFN_TF])
        act_scr[:, cols] = (gt * _sigmoid(gt) * up).astype(BF16)
    y = _dot(act_scr[...], wo_ref[...])
    o_ref[...] = x + _rms(y, g5_ref[...])


def _ffn(xf, g4, g5, w_in, w_out, layer, tm):
    n, d = xf.shape
    dff = w_out.shape[1]
    row = pl.BlockSpec((tm, d), lambda i: (i, 0))
    vec = _const_spec((1, d))
    return pl.pallas_call(
        _ffn_kernel,
        grid=(n // tm,),
        in_specs=[row, vec, _layer_spec(w_in.shape, layer), _layer_spec(w_out.shape, layer), vec],
        out_specs=row,
        out_shape=jax.ShapeDtypeStruct((n, d), F32),
        scratch_shapes=[pltpu.VMEM((tm, dff), BF16)],
        compiler_params=_cparams(1),
        name="ffn",
    )(xf, g4, w_in, w_out, g5)


def _relayout_even(w_in, w_uq, w_ukv, a_log, dt_bias):
    d = w_in.shape[0]
    qkvz = 3 * GDN_HEADS * GDN_DK + GDN_HEADS * GDN_DK
    ab = 2 * GDN_HEADS
    lat = w_in.shape[1] - qkvz - ab - MLA_ROPE
    pad = LANES - MLA_ROPE - ab
    w_in_re = jnp.concatenate(
        [w_in[:, :qkvz], w_in[:, qkvz + ab:qkvz + ab + lat], w_in[:, qkvz + ab + lat:],
         w_in[:, qkvz:qkvz + ab], jnp.zeros((d, pad), w_in.dtype)], axis=1).astype(BF16)
    r = w_uq.shape[0]
    uq = w_uq.reshape(r, MLA_HEADS, MLA_NOPE + MLA_ROPE)
    pe = jnp.pad(uq[:, :, MLA_NOPE:], ((0, 0), (0, 0), (0, LANES - MLA_ROPE)))
    wuq_re = jnp.concatenate([uq[:, :, :MLA_NOPE].reshape(r, -1), pe.reshape(r, -1)],
                             axis=1).astype(BF16)
    ukv = w_ukv.reshape(w_ukv.shape[0], MLA_HEADS, -1)
    wukv_re = jnp.concatenate([ukv[:, :, :MLA_NOPE].reshape(r, -1),
                               ukv[:, :, MLA_NOPE:].reshape(r, -1)], axis=1).astype(BF16)
    alog_row = jnp.zeros((1, LANES), F32).at[0, G_LANE:G_LANE + GDN_HEADS].set(a_log)
    dtb_row = jnp.zeros((1, LANES), F32).at[0, G_LANE:G_LANE + GDN_HEADS].set(dt_bias)
    return w_in_re, wuq_re, wukv_re, alog_row, dtb_row


def kernel(x, mem, positions, norm_gains, mem_norm, e_w_in, e_conv_w, e_a_log, e_dt_bias, e_o_norm,
           e_q_norm, e_kv_norm, e_w_uq, e_w_ukv, e_w_out, o_w_in, o_conv_w, o_conv_b, o_gate_a_w,
           o_gate_a_b, o_gate_x_w, o_gate_x_b, o_a_param, o_w_out, xa_wq, xa_wkv, xa_wo, ffn_w_in,
           ffn_w_out):
    batch, t, d = x.shape
    n = batch * t
    depth = norm_gains.shape[0]
    xf = x.reshape(n, d)
    gains = norm_gains.reshape(depth, -1, 1, d)

    cos_t, sin_t = _rope_tables(positions)
    kv_mem = _mem_kv(mem.reshape(-1, d), mem_norm, xa_wkv.astype(BF16))
    xa_wq_b, xa_wo_b = xa_wq.astype(BF16), xa_wo.astype(BF16)
    ffn_w_in_b, ffn_w_out_b = ffn_w_in.astype(BF16), ffn_w_out.astype(BF16)

    for layer in range(depth):
        g = gains[layer]
        mix = None
        if layer % 2 == 0:
            e = layer // 2
            w_in_re, wuq_re, wukv_re, alog_row, dtb_row = _relayout_even(
                e_w_in[e], e_w_uq[e], e_w_ukv[e], e_a_log[e], e_dt_bias[e])
            q, k, v, z, gb, qf, kf, vb = _even_inproj(
                xf, g[0], w_in_re, e_conv_w[e], alog_row, dtb_row,
                e_q_norm[e].reshape(1, -1), e_kv_norm[e].reshape(1, -1), wuq_re, wukv_re,
                cos_t, sin_t, batch, tm=512)
            out_a = _gdn(q, k, v, z, gb, e_o_norm[e].reshape(1, -1), batch)
            out_b = _mla(qf, kf, vb, batch)
            mix = (out_a, out_b, e_w_out[e].astype(BF16), g[1])
        else:
            o = layer // 2
            xf = _odd_mixer(xf, g[0], g[1], o_w_in[o].astype(BF16), o_conv_w[o],
                            o_conv_b[o].reshape(1, d), o_gate_a_w[o].astype(BF16),
                            o_gate_a_b[o].reshape(1, d), o_gate_x_w[o].astype(BF16),
                            o_gate_x_b[o].reshape(1, d), o_a_param[o].reshape(1, d),
                            o_w_out[o].astype(BF16), batch, tm=512)
        xf = _xattn(xf, g[2], g[3], xa_wq_b, kv_mem, xa_wo_b, layer, batch, tm=1024, mix=mix)
        xf = _ffn(xf, g[4], g[5], ffn_w_in_b, ffn_w_out_b, layer, tm=1024)
    return xf.reshape(batch, t, d)
```

```python
import functools
import math

import jax
import jax.numpy as jnp
from jax import lax
from jax.experimental import pallas as pl
from jax.experimental.pallas import tpu as pltpu

F32 = jnp.float32
BF16 = jnp.bfloat16

EPS = 1e-6
CHUNK = 64
GDN_HEADS = 4
GDN_DK = 128
MLA_HEADS = 4
MLA_NOPE = 128
MLA_ROPE = 64
MLA_SCALE = (MLA_NOPE + MLA_ROPE) ** -0.5
LOG2E = math.log2(math.e)
ROPE_BASE = 10000.0
LRU_BLOCKS = 4
LRU_C = 8.0
XA_HEADS = 4
CONV_W = 4

LANES = 128
CARRY_ROWS = 8
V7X_VMEM_LIMIT = 56 * 1024 * 1024

NT_DIMS = (((1,), (1,)), ((), ()))
TN_DIMS = (((0,), (0,)), ((), ()))


def _cparams(n_axes):
    return pltpu.CompilerParams(
        dimension_semantics=("arbitrary",) * n_axes,
        vmem_limit_bytes=V7X_VMEM_LIMIT)


def _dot(a, b):
    return jnp.dot(a, b, preferred_element_type=F32)


def _dot_nt(a, b):
    return lax.dot_general(a, b, NT_DIMS, preferred_element_type=F32)


def _rms(xf, g):
    ms = jnp.mean(xf * xf, axis=-1, keepdims=True)
    return xf * lax.rsqrt(ms + EPS) * g


def _sigmoid(x):
    return 1.0 / (1.0 + jnp.exp(-x))


def _softplus(x):
    return jnp.maximum(x, 0.0) + jnp.log1p(jnp.exp(-jnp.abs(x)))


def _const_spec(shape):
    nd = len(shape)
    return pl.BlockSpec(shape, lambda *_: (0,) * nd)


def _layer_spec(shape, layer):
    nd = len(shape) - 1
    return pl.BlockSpec((None,) + tuple(shape[1:]), lambda *_: (layer,) + (0,) * nd,
                        pipeline_mode=pl.Buffered(1))


def _rope_kernel(pos_ref, inv_ref, cos_ref, sin_ref):
    half = MLA_ROPE // 2
    per_row = LANES // half
    tr = pos_ref.shape[0]
    ang = pos_ref[...].astype(F32) * inv_ref[...]
    lane = lax.broadcasted_iota(jnp.int32, ang.shape, 1)
    for tbl, ref in ((jnp.cos(ang), cos_ref), (jnp.sin(ang), sin_ref)):
        for j in range(per_row):
            rep = jnp.where((lane >= half * j) & (lane < half * (j + 1)), tbl, 0.0)
            width = half
            while width < LANES:
                rep = rep + pltpu.roll(rep, width, axis=1)
                width *= 2
            ref[pl.ds(j, tr, stride=per_row), :] = rep


def _rope_tables(positions):
    n = positions.size
    half = MLA_ROPE // 2
    per_row = LANES // half
    inv_freq = ROPE_BASE ** (-jnp.arange(0, MLA_ROPE, 2, dtype=F32) / MLA_ROPE)
    inv_row = jnp.tile(inv_freq, per_row).reshape(1, LANES)
    pos_rep = jnp.repeat(positions.reshape(n // per_row, per_row), half, axis=1)
    rows = n // per_row
    tr = min(rows, 512)
    out_spec = pl.BlockSpec((tr * per_row, LANES), lambda i: (i, 0))
    return pl.pallas_call(
        _rope_kernel,
        grid=(rows // tr,),
        in_specs=[pl.BlockSpec((tr, LANES), lambda i: (i, 0)), _const_spec((1, LANES))],
        out_specs=[out_spec, out_spec],
        out_shape=[jax.ShapeDtypeStruct((n, LANES), F32)] * 2,
        compiler_params=_cparams(1),
        name="rope_tables",
    )(pos_rep, inv_row)


def _memkv_kernel(mem_ref, g_ref, w_ref, o_ref, wb_scr):
    @pl.when(pl.program_id(1) == 0)
    def _():
        wb_scr[...] = w_ref[...].astype(BF16)

    mn = _rms(mem_ref[...], g_ref[...]).astype(BF16)
    o_ref[...] = _dot(mn, wb_scr[...]).astype(BF16)


def _mem_kv(mem2d, mem_norm, wkv):
    depth, d, d2 = wkv.shape
    rows = mem2d.shape[0]
    tr = min(rows, 512)
    return pl.pallas_call(
        _memkv_kernel,
        grid=(depth, rows // tr),
        in_specs=[pl.BlockSpec((tr, d), lambda l, i: (i, 0)),
                  _const_spec((1, d)),
                  pl.BlockSpec((None, d, d2), lambda l, i: (l, 0, 0))],
        out_specs=pl.BlockSpec((None, tr, d2), lambda l, i: (l, i, 0)),
        out_shape=jax.ShapeDtypeStruct((depth, rows, d2), BF16),
        scratch_shapes=[pltpu.VMEM((d, d2), BF16)],
        compiler_params=_cparams(2),
        name="mem_kv",
    )(mem2d, mem_norm.reshape(1, d), wkv)


def _rope_tile(x, cos, sin, lane):
    half = MLA_ROPE // 2
    up = pltpu.roll(x, LANES - half, axis=1)
    dn = pltpu.roll(x, half, axis=1)
    rot = jnp.where(lane < half, -up, dn)
    return jnp.where(lane < MLA_ROPE, x * cos + rot * sin, 0.0)


ROW_PHASES = 8


def _causal_conv_phases(x, xslab, sl, cw_ref, ls, bias=None):
    assert cw_ref.shape[0] == CONV_W and CONV_W - 1 <= CARRY_ROWS
    tm = x.shape[0]
    grp = tm // ROW_PHASES
    xslab[sl, CARRY_ROWS:CARRY_ROWS + tm, :] = x
    ph = [xslab[sl, pl.ds(CARRY_ROWS + j, grp, stride=ROW_PHASES), :]
          for j in range(-(CONV_W - 1), ROW_PHASES)]
    xslab[sl, 0:CARRY_ROWS, :] = x[tm - CARRY_ROWS:, :]
    w = [cw_ref[i:i + 1, ls] for i in range(CONV_W)]
    out = []
    for j in range(ROW_PHASES):
        acc = w[CONV_W - 1] * ph[j + CONV_W - 1]
        if bias is not None:
            acc = acc + bias
        for i in range(CONV_W - 1):
            acc = acc + w[i] * ph[j + i]
        out.append(acc)
    return out


def _even_in_kernel(x_ref, g_ref, win_ref, cw_ref, alog_ref, dtb_ref, qn_ref, kvn_ref,
                    wuq_ref, wukv_ref, cos_ref, sin_ref,
                    q_out, k_out, v_out, z_out, gb_out, qf_out, kf_out, vb_out,
                    xslab, yslab):
    tm = x_ref.shape[0]
    qk_w = GDN_HEADS * GDN_DK
    qkv_w = 3 * qk_w

    @pl.when(pl.program_id(1) == 0)
    def _():
        xslab[:, 0:CARRY_ROWS, :] = jnp.zeros((qkv_w // LANES, CARRY_ROWS, LANES), F32)

    h = _rms(x_ref[...], g_ref[...]).astype(BF16)
    lane = lax.broadcasted_iota(jnp.int32, (tm, LANES), 1)
    cos = cos_ref[...]
    sin = sin_ref[...]
    nope_w = MLA_HEADS * MLA_NOPE
    z_off = qkv_w
    cq_off = z_off + qk_w
    ckv_off = cq_off + qn_ref.shape[1]
    misc_off = ckv_off + kvn_ref.shape[1]
    gw = 2 * GDN_DK
    st = {}

    def proj_cols(lo, width):
        return _dot(h, win_ref[:, lo:lo + width])

    def mm_cq():
        st["cq"] = proj_cols(cq_off, qn_ref.shape[1])

    def ep_cq():
        st["cqb"] = _rms(st.pop("cq"), qn_ref[...] * (MLA_SCALE * LOG2E)).astype(BF16)

    def mm_ckv():
        st["ckv"] = proj_cols(ckv_off, kvn_ref.shape[1])

    def ep_ckv():
        st["ckvb"] = _rms(st.pop("ckv"), kvn_ref[...]).astype(BF16)

    def mm_misc():
        st["misc"] = proj_cols(misc_off, LANES)

    def ep_misc():
        misc = st.pop("misc")
        gdec = -jnp.exp(alog_ref[...]) * _softplus(misc + dtb_ref[...])
        beta = _sigmoid(misc)
        gb_out[...] = jnp.where((lane >= G_LANE) & (lane < G_LANE + GDN_HEADS), gdec,
                                jnp.where((lane >= B_LANE) & (lane < B_LANE + GDN_HEADS),
                                          beta, 0.0))
        st["kpe"] = _rope_tile(misc, cos, sin, lane).astype(BF16)

    def mm_uq():
        st["qf"] = _dot(st.pop("cqb"), wuq_ref[...])

    def ep_uq():
        qf = st.pop("qf")
        for hh in range(MLA_HEADS):
            base = 2 * LANES * hh
            qf_out[:, base:base + LANES] = qf[:, hh * LANES:(hh + 1) * LANES].astype(BF16)
            pe = qf[:, nope_w + hh * LANES: nope_w + (hh + 1) * LANES]
            qf_out[:, base + LANES:base + 2 * LANES] = _rope_tile(pe, cos, sin, lane).astype(BF16)

    def mm_ukv():
        st["kv"] = _dot(st.pop("ckvb"), wukv_ref[...])

    def ep_ukv():
        kv = st.pop("kv")
        for hh in range(MLA_HEADS):
            base = 2 * LANES * hh
            kf_out[:, base:base + LANES] = kv[:, hh * LANES:(hh + 1) * LANES].astype(BF16)
            kf_out[:, base + LANES:base + 2 * LANES] = st["kpe"]
        vb_out[...] = kv[:, nope_w:].astype(BF16)

    def mm_group(lo):
        def run():
            st[lo] = proj_cols(lo, gw)
        return run

    def ep_qkv(lo):
        def run():
            raw = st.pop(lo)
            which, off = divmod(lo, qk_w)
            out, scale = ((q_out, GDN_DK ** -0.5), (k_out, 1.0), (v_out, None))[which]
            grp = tm // ROW_PHASES
            for sub in range(gw // LANES):
                sl = lo // LANES + sub
                ls = slice(lo + sub * LANES, lo + (sub + 1) * LANES)
                blocks = _causal_conv_phases(raw[:, sub * LANES:(sub + 1) * LANES], xslab, sl,
                                             cw_ref, ls)
                for j, acc in enumerate(blocks):
                    a = acc * _sigmoid(acc)
                    if scale is not None:
                        a = a * (lax.rsqrt(jnp.sum(a * a, axis=-1, keepdims=True) + EPS) * scale)
                    yslab[sl, pl.ds(j, grp, stride=ROW_PHASES), :] = a
                out[:, off + sub * LANES: off + (sub + 1) * LANES] = yslab[sl].astype(BF16)
        return run

    def ep_z(lo):
        def run():
            z_out[:, lo - z_off: lo - z_off + gw] = st.pop(lo).astype(BF16)
        return run

    pairs = [(mm_group(lo), ep_qkv(lo)) for lo in range(0, qkv_w, gw)]
    pairs += [(mm_cq, ep_cq), (mm_ckv, ep_ckv), (mm_misc, ep_misc), (mm_uq, ep_uq), (mm_ukv, ep_ukv)]
    pairs += [(mm_group(lo), ep_z(lo)) for lo in range(z_off, cq_off, gw)]
    for mm, ep in pairs:
        mm()
        ep()


def _even_inproj(xf, g0, w_in_re, conv_w, alog_row, dtb_row, q_norm, kv_norm, wuq_re, wukv_re,
                 cos_t, sin_t, batch, tm):
    n, d = xf.shape
    t = n // batch
    nt = t // tm
    e_in = w_in_re.shape[1]
    qkv_w = conv_w.shape[1]
    qk_w = qkv_w // 3
    row = lambda w: pl.BlockSpec((tm, w), lambda b, i: (b * nt + i, 0))
    outs = [(qk_w, BF16)] * 4 + [(LANES, F32), (2 * LANES * MLA_HEADS, BF16),
                                 (2 * LANES * MLA_HEADS, BF16), (LANES * MLA_HEADS, BF16)]
    return pl.pallas_call(
        _even_in_kernel,
        grid=(batch, nt),
        in_specs=[row(d), _const_spec((1, d)), _const_spec((d, e_in)), _const_spec((CONV_W, qkv_w)),
                  _const_spec((1, LANES)), _const_spec((1, LANES)),
                  _const_spec(q_norm.shape), _const_spec(kv_norm.shape),
                  _const_spec(wuq_re.shape), _const_spec(wukv_re.shape),
                  row(LANES), row(LANES)],
        out_specs=[row(w) for w, _ in outs],
        out_shape=[jax.ShapeDtypeStruct((n, w), dt) for w, dt in outs],
        scratch_shapes=[pltpu.VMEM((qkv_w // LANES, tm + CARRY_ROWS, LANES), F32),
                        pltpu.VMEM((qkv_w // LANES, tm, LANES), F32)],
        compiler_params=_cparams(2),
        name="even_inproj",
    )(xf, g0, w_in_re, conv_w, alog_row, dtb_row, q_norm, kv_norm, wuq_re, wukv_re, cos_t, sin_t)


SUPER = 2 * CHUNK
G_LANE = MLA_ROPE
B_LANE = MLA_ROPE + GDN_HEADS
GDN_PIPE = 2


def _gdn_kernel(q_ref, k_ref, v_ref, z_ref, gb_ref, on_ref, out_ref,
                gc_scr, gct_scr, gl_scr, u_scr, w_scr, a_scr, qe_scr, kd_scr, s_scr):
    t = q_ref.shape[0]
    n_super = t // SUPER
    n_chunk = t // CHUNK

    gb = gb_ref[...]
    rowi = lax.broadcasted_iota(jnp.int32, gb.shape, 0) & (CHUNK - 1)
    gc = gb
    s = 1
    while s < CHUNK:
        gc = jnp.where(rowi >= s, gc + pltpu.roll(gc, s, axis=0), gc)
        s *= 2
    gc_scr[...] = gc
    gct_scr[...] = gc.T
    g3 = gc.reshape(n_chunk, CHUNK, LANES)
    gl_scr[...] = jnp.broadcast_to(g3[:, CHUNK - 1:CHUNK, :], g3.shape).reshape(t, LANES)

    ri = lax.broadcasted_iota(jnp.int32, (SUPER, SUPER), 0)
    ci = lax.broadcasted_iota(jnp.int32, (SUPER, SUPER), 1)
    same = (ri >= CHUNK) == (ci >= CHUNK)
    causal = same & (ri >= ci)
    strict = same & (ri > ci)
    eye = (ri == ci).astype(F32)

    on = on_ref[...]
    heads = range(GDN_HEADS)
    hsl = [slice(hh * GDN_DK, (hh + 1) * GDN_DK) for hh in heads]
    lane_g = [slice(G_LANE + hh, G_LANE + hh + 1) for hh in heads]
    lane_b = [slice(B_LANE + hh, B_LANE + hh + 1) for hh in heads]

    def solve_stages(sc0):
        probs = []
        for sub in range(GDN_PIPE):
            rows = pl.ds(pl.multiple_of((sc0 + sub) * SUPER, SUPER), SUPER)
            probs += [(rows, hh) for hh in heads]
        kc = [k_ref[rows, hsl[hh]] for rows, hh in probs]
        qc = [q_ref[rows, hsl[hh]] for rows, hh in probs]
        gcol = [gc_scr[rows, lane_g[hh]] for rows, hh in probs]
        bcol = [gb_ref[rows, lane_b[hh]] for rows, hh in probs]
        grow = [gct_scr[lane_g[hh], rows] for rows, hh in probs]
        kk = [_dot_nt(k, k) for k in kc]
        qk = [_dot_nt(q, k) for q, k in zip(qc, kc)]
        yield
        decay = [jnp.where(causal, jnp.exp(jnp.where(causal, gc - gr, 0.0)), 0.0)
                 for gc, gr in zip(gcol, grow)]
        m = [jnp.where(strict, -(x * b * dc), 0.0) for x, b, dc in zip(kk, bcol, decay)]
        qacc = [eye + x for x in m]
        mb = [x.astype(BF16) for x in m]
        mj = [_dot(x, x) for x in mb]
        yield
        lvl = 2
        while lvl < CHUNK // 2:
            mb = [x.astype(BF16) for x in mj]
            r = [_dot(x, jnp.concatenate([x, qa.astype(BF16)], axis=1)) for x, qa in zip(mb, qacc)]
            yield
            mj = [x[:, :SUPER] for x in r]
            qacc = [qa + x[:, SUPER:] for qa, x in zip(qacc, r)]
            lvl *= 2
        last = [_dot(x.astype(BF16), qa.astype(BF16)) for qa, x in zip(qacc, mj)]
        yield
        qacc = [qa + x for qa, x in zip(qacc, last)]
        eg = [jnp.exp(gc) for gc in gcol]
        sol = []
        for i, (rows, hh) in enumerate(probs):
            vc = v_ref[rows, hsl[hh]].astype(F32)
            kf = kc[i].astype(F32)
            rhs = jnp.concatenate([vc * bcol[i], kf * (bcol[i] * eg[i])], axis=1).astype(BF16)
            sol.append(_dot(qacc[i].astype(BF16), rhs))
        yield
        for i, (rows, hh) in enumerate(probs):
            u_scr[rows, hsl[hh]] = sol[i][:, :GDN_DK]
            w_scr[rows, hsl[hh]] = sol[i][:, GDN_DK:].astype(BF16)
            aqk = qk[i] * decay[i]
            a_sh = pltpu.roll(aqk, CHUNK, axis=1)
            a_scr[rows, hsl[hh]] = jnp.where(ri < CHUNK, aqk, a_sh).astype(BF16)
            qe_scr[rows, hsl[hh]] = (qc[i].astype(F32) * eg[i]).astype(BF16)
            glast = gl_scr[rows, lane_g[hh]]
            kd_scr[rows, hsl[hh]] = (kc[i].astype(F32) * jnp.exp(glast - gcol[i])).astype(BF16)

    def scan_stages(sc0, st):
        for half in range(GDN_PIPE * SUPER // CHUNK):
            r0 = pl.multiple_of(sc0 * SUPER + half * CHUNK, CHUNK)
            rows = pl.ds(r0, CHUNK)
            sb = [x.astype(BF16) for x in st]
            t1 = [_dot(w_scr[rows, hsl[hh]], sb[hh]) for hh in heads]
            oq = [_dot(qe_scr[rows, hsl[hh]], sb[hh]) for hh in heads]
            yield
            vb = [(u_scr[rows, hsl[hh]] - t1[hh]).astype(BF16) for hh in heads]
            oa = [_dot(a_scr[rows, hsl[hh]][:, :CHUNK], vb[hh]) for hh in heads]
            kv = [lax.dot_general(kd_scr[rows, hsl[hh]], vb[hh], TN_DIMS,
                                  preferred_element_type=F32) for hh in heads]
            yield
            for hh in heads:
                gam = jnp.exp(gl_scr[pl.ds(r0, 1), lane_g[hh]])
                st[hh] = st[hh] * gam + kv[hh]
                z = z_ref[rows, hsl[hh]].astype(F32)
                out_ref[rows, hsl[hh]] = (
                    _rms(oq[hh] + oa[hh], on) * (z * _sigmoid(z))).astype(BF16)

    _round_robin([solve_stages(0)])
    s_scr[...] = jnp.zeros(s_scr.shape, F32)
    n_steps = n_super // GDN_PIPE

    def pipe_body(it, _):
        st = [s_scr[hh] for hh in heads]
        _round_robin([solve_stages((it + 1) * GDN_PIPE), scan_stages(it * GDN_PIPE, st)])
        for hh in heads:
            s_scr[hh] = st[hh]
        return 0

    lax.fori_loop(0, n_steps - 1, pipe_body, 0)
    st = [s_scr[hh] for hh in heads]
    _round_robin([scan_stages((n_steps - 1) * GDN_PIPE, st)])


def _gdn(q, k, v, z, gb, o_norm, batch):
    n, w = q.shape
    t = n // batch
    row = lambda ww: pl.BlockSpec((t, ww), lambda b: (b, 0))
    return pl.pallas_call(
        _gdn_kernel,
        grid=(batch,),
        in_specs=[row(w), row(w), row(w), row(w), row(LANES), _const_spec((1, GDN_DK))],
        out_specs=row(w),
        out_shape=jax.ShapeDtypeStruct((n, w), BF16),
        scratch_shapes=[pltpu.VMEM((t, LANES), F32),
                        pltpu.VMEM((LANES, t), F32),
                        pltpu.VMEM((t, LANES), F32),
                        pltpu.VMEM((t, w), F32),
                        pltpu.VMEM((t, w), BF16),
                        pltpu.VMEM((t, w), BF16),
                        pltpu.VMEM((t, w), BF16),
                        pltpu.VMEM((t, w), BF16),
                        pltpu.VMEM((GDN_HEADS, GDN_DK, GDN_DK), F32)],
        compiler_params=_cparams(1),
        name="gdn",
    )(q, k, v, z, gb, o_norm)


MLA_TQ = 256


def _mla_kernel(q_ref, k_ref, v_ref, o_ref):
    t = q_ref.shape[0]
    tq = MLA_TQ
    nq = t // tq
    hw = 2 * LANES
    heads = range(q_ref.shape[1] // hw)
    ri = lax.broadcasted_iota(jnp.int32, (tq, tq), 0) // CHUNK
    ci = lax.broadcasted_iota(jnp.int32, (tq, tq), 1) // CHUNK
    diag_mask = ci <= ri

    groups = [(i, h) for i in range(nq) for h in heads]

    def scores(i, h):
        kw = (i + 1) * tq
        return _dot_nt(q_ref[i * tq:(i + 1) * tq, h * hw:(h + 1) * hw],
                       k_ref[0:kw, h * hw:(h + 1) * hw])

    s_next = scores(*groups[0])
    for g, (i, h) in enumerate(groups):
        s = s_next
        if g + 1 < len(groups):
            s_next = scores(*groups[g + 1])
        kw = (i + 1) * tq
        s_diag = jnp.where(diag_mask, s[:, kw - tq:], -jnp.inf)
        s = s_diag if i == 0 else jnp.concatenate([s[:, :kw - tq], s_diag], axis=1)
        m = jnp.max(s, axis=-1, keepdims=True)
        p = jnp.exp2(s - m)
        l = jnp.sum(p, axis=-1, keepdims=True)
        pv = _dot(p.astype(BF16), v_ref[0:kw, h * LANES:(h + 1) * LANES])
        o_ref[i * tq:(i + 1) * tq, h * LANES:(h + 1) * LANES] = (pv / l).astype(BF16)


MLA_HEADS_PER_STEP = 2


def _mla(qf, kf, vb, batch):
    n = qf.shape[0]
    t = n // batch
    hps = MLA_HEADS_PER_STEP
    spec = lambda w: pl.BlockSpec((t, w * hps), lambda b, g: (b, g))
    return pl.pallas_call(
        _mla_kernel,
        grid=(batch, MLA_HEADS // hps),
        in_specs=[spec(2 * LANES), spec(2 * LANES), spec(LANES)],
        out_specs=spec(LANES),
        out_shape=jax.ShapeDtypeStruct((n, vb.shape[1]), BF16),
        compiler_params=_cparams(2),
        name="mla_attn",
    )(qf, kf, vb)


SCAN_PHASES = ROW_PHASES


def _gelu_tanh(x):
    c2 = 2.0 * math.sqrt(2.0 / math.pi) * LOG2E
    return x / (1.0 + jnp.exp2(x * (-c2 - (c2 * 0.044715) * (x * x))))


def _odd_kernel(x_ref, g0_ref, win_f32, cw_ref, cb_ref, gaw_f32, gab_ref, gxw_f32, gxb_ref,
                ap_ref, wout_f32, g1_ref, o_ref, xslab, hslab, hcar,
                win_ref, gaw_ref, gxw_ref, wout_ref):
    tm, d = x_ref.shape
    bw = d // LRU_BLOCKS
    n_slab = d // LANES
    assert cw_ref.shape[0] == CONV_W and CONV_W - 1 <= CARRY_ROWS
    _cast_once([(win_f32, win_ref), (gaw_f32, gaw_ref), (gxw_f32, gxw_ref), (wout_f32, wout_ref)])

    @pl.when(pl.program_id(1) == 0)
    def _():
        xslab[:, 0:CARRY_ROWS, :] = jnp.zeros((n_slab, CARRY_ROWS, LANES), F32)
        hcar[...] = jnp.zeros(hcar.shape, F32)

    x = x_ref[...]
    h = _rms(x, g0_ref[...]).astype(BF16)
    grp = tm // SCAN_PHASES
    slabs_per_block = bw // LANES
    rowg = lax.broadcasted_iota(jnp.int32, (grp, 1), 0)

    blocks = range(LRU_BLOCKS)
    csl = [slice(n * bw, (n + 1) * bw) for n in blocks]
    st = [dict() for _ in blocks]

    def projx(n):
        st[n]["xb"] = _dot(h, win_ref[:, csl[n]])

    def projy(n):
        st[n]["yb"] = _dot(h, win_ref[:, d + n * bw: d + (n + 1) * bw])

    def conv(n):
        xb = st[n].pop("xb")
        cols = []
        for k in range(slabs_per_block):
            sl = n * slabs_per_block + k
            ls = slice(n * bw + k * LANES, n * bw + (k + 1) * LANES)
            out = _causal_conv_phases(xb[:, k * LANES:(k + 1) * LANES], xslab, sl, cw_ref, ls,
                                      bias=cb_ref[:, ls])
            cols.append(jnp.concatenate(out, axis=0))
        st[n]["xc"] = jnp.concatenate(cols, axis=1)

    def gates(n):
        xcb = st[n]["xc"].astype(BF16)
        st[n]["ra"] = _dot(xcb, gaw_ref[n])
        st[n]["ia"] = _dot(xcb, gxw_ref[n])

    def recur(n):
        cs, xc = csl[n], st[n].pop("xc")
        r = _sigmoid(st[n].pop("ra") + gab_ref[:, cs])
        ig = _sigmoid(st[n].pop("ia") + gxb_ref[:, cs])
        a = jnp.exp2(r * ((-LRU_C * LOG2E) * _softplus(-ap_ref[:, cs])))
        om = 1.0 - a * a
        u = jnp.where(om > 0.0, om * lax.rsqrt(om), 0.0) * (ig * xc)

        ca, cu = a[0:grp], u[0:grp]
        la, lu = [ca], [cu]
        for j in range(1, SCAN_PHASES):
            aj, uj = a[j * grp:(j + 1) * grp], u[j * grp:(j + 1) * grp]
            cu = aj * cu + uj
            ca = aj * ca
            la.append(ca)
            lu.append(cu)
        ga, gu = ca, cu
        s = 1
        while s < grp:
            valid = rowg >= s
            gu = jnp.where(valid, ga * pltpu.roll(gu, s, axis=0) + gu, gu)
            ga = jnp.where(valid, ga * pltpu.roll(ga, s, axis=0), ga)
            s *= 2
        h_in = hcar[:, cs]
        h_end = ga * h_in + gu
        hcar[:, cs] = h_end[grp - 1:grp, :]
        h_prev = jnp.where(rowg == 0, h_in, pltpu.roll(h_end, 1, axis=0))
        slabs = range(n * slabs_per_block, (n + 1) * slabs_per_block)
        for j in range(SCAN_PHASES):
            hj = la[j] * h_prev + lu[j]
            for k, sl in enumerate(slabs):
                hslab[sl, pl.ds(j, grp, stride=SCAN_PHASES), :] = hj[:, k * LANES:(k + 1) * LANES]
        st[n]["hs"] = jnp.concatenate([hslab[sl] for sl in slabs], axis=1)

    def gate(n):
        st[n]["hg"] = (st[n].pop("hs") * _gelu_tanh(st[n].pop("yb"))).astype(BF16)

    def outp(n):
        st[n]["y"] = _dot(st[n].pop("hg"), wout_ref[csl[n], :])

    chain = (projx, conv, gates, recur, projy, gate, outp)
    for k in range(LRU_BLOCKS + len(chain) - 1):
        for s, stage in enumerate(chain):
            if 0 <= k - s < LRU_BLOCKS:
                stage(k - s)
    y = functools.reduce(lambda p, q: p + q, [st[n]["y"] for n in blocks])
    o_ref[...] = x + _rms(y, g1_ref[...])


def _odd_mixer(xf, g0, g1, w_in, conv_w, conv_b, gaw, gab, gxw, gxb, a_param, w_out, o, batch, tm):
    n, d = xf.shape
    t = n // batch
    nt = t // tm
    row = pl.BlockSpec((tm, d), lambda b, i: (b * nt + i, 0))
    vec = _const_spec((1, d))
    weights = (w_in, gaw, gxw, w_out)
    wspec = [_layer_spec(w.shape, o) for w in weights]
    return pl.pallas_call(
        _odd_kernel,
        grid=(batch, nt),
        in_specs=[row, vec, wspec[0], _const_spec(conv_w.shape), vec,
                  wspec[1], vec, wspec[2], vec, vec, wspec[3], vec],
        out_specs=row,
        out_shape=jax.ShapeDtypeStruct((n, d), F32),
        scratch_shapes=[pltpu.VMEM((d // LANES, tm + CARRY_ROWS, LANES), F32),
                        pltpu.VMEM((d // LANES, tm, LANES), F32),
                        pltpu.VMEM((1, d), F32)]
                       + [pltpu.VMEM(w.shape[1:], BF16) for w in weights],
        compiler_params=_cparams(2),
        name="odd_mixer",
    )(xf, g0, w_in, conv_w, conv_b, gaw, gab, gxw, gxb, a_param, w_out, g1)


XA_SPLIT = 2


def _round_robin(gens):
    live = list(gens)
    while live:
        live = [g for g in live if next(g, StopIteration) is not StopIteration]


def _cast_once(pairs, n_axes=2):
    first = pl.program_id(0) == 0
    for axis in range(1, n_axes):
        first = first & (pl.program_id(axis) == 0)

    @pl.when(first)
    def _():
        for src, dst in pairs:
            dst[...] = src[...].astype(BF16)


def _xattn_kernel(x_ref, g2_ref, wq_f32, kv_ref, wo_f32, g3_ref, o_ref, wq_ref, wo_ref):
    _cast_once([(wq_f32, wq_ref), (wo_f32, wo_ref)])
    tm = x_ref.shape[0]
    rs = tm // XA_SPLIT
    _round_robin([_xattn_stages(slice(k * rs, (k + 1) * rs), x_ref, None, g2_ref, wq_ref, kv_ref,
                                wo_ref, g3_ref, o_ref) for k in range(XA_SPLIT)])


def _outproj_xattn_kernel(x_ref, a_ref, b_ref, w_f32, g1_ref,
                          g2_ref, wq_f32, kv_ref, wo_f32, g3_ref, o_ref, wq_ref, wo_ref, w_ref):
    _cast_once([(wq_f32, wq_ref), (wo_f32, wo_ref), (w_f32, w_ref)])
    tm = x_ref.shape[0]
    rs = tm // XA_SPLIT
    _round_robin([_xattn_stages(slice(k * rs, (k + 1) * rs), x_ref, (a_ref, b_ref, w_ref, g1_ref),
                                g2_ref, wq_ref, kv_ref, wo_ref, g3_ref, o_ref)
                  for k in range(XA_SPLIT)])


def _xattn_stages(rows, x_ref, mix, g2_ref, wq_ref, kv_ref, wo_ref, g3_ref, o_ref):
    x = x_ref[rows, :]
    d = x.shape[1]
    hd = d // XA_HEADS
    if mix is not None:
        a_ref, b_ref, w_ref, g1_ref = mix
        ka = a_ref.shape[1]
        y = _dot(a_ref[rows, :], w_ref[0:ka, :]) + _dot(b_ref[rows, :], w_ref[ka:, :])
        yield
        x = x + _rms(y, g1_ref[...])
    h = _rms(x, g2_ref[...]).astype(BF16)
    q = _dot(h, wq_ref[...])
    yield
    q = (q * (hd ** -0.5 * LOG2E)).astype(BF16)
    s = [_dot_nt(q[:, hh * hd:(hh + 1) * hd], kv_ref[:, hh * hd:(hh + 1) * hd])
         for hh in range(XA_HEADS)]
    yield
    outs = []
    for hh in range(XA_HEADS):
        m = jnp.max(s[hh], axis=-1, keepdims=True)
        p = jnp.exp2(s[hh] - m)
        l = jnp.sum(p, axis=-1, keepdims=True)
        pv = _dot(p.astype(BF16), kv_ref[:, d + hh * hd: d + (hh + 1) * hd])
        outs.append((pv / l).astype(BF16))
    yield
    y = _dot(jnp.concatenate(outs, axis=1), wo_ref[...])
    yield
    o_ref[rows, :] = x + _rms(y, g3_ref[...])


def _xattn(xf, g2, g3, wq, kv_mem, wo, layer, batch, tm, mix=None):
    n, d = xf.shape
    t = n // batch
    nt = t // tm
    n_mem = kv_mem.shape[1] // batch
    row = lambda w: pl.BlockSpec((tm, w), lambda b, i: (b * nt + i, 0))
    vec = _const_spec((1, d))
    xa_specs = [vec, _layer_spec(wq.shape, layer),
                pl.BlockSpec((None, n_mem, 2 * d), lambda b, i: (layer, b, 0)),
                _layer_spec(wo.shape, layer), vec]
    xa_args = (g2, wq, kv_mem, wo, g3)
    scratch = [pltpu.VMEM(wq.shape[1:], BF16), pltpu.VMEM(wo.shape[1:], BF16)]
    if mix is None:
        body, specs, args, name = _xattn_kernel, [row(d)] + xa_specs, (xf,) + xa_args, "xattn"
    else:
        a, b, w_out, w_out_layer, g1 = mix
        body, name = _outproj_xattn_kernel, "outproj_xattn"
        specs = [row(d), row(a.shape[1]), row(b.shape[1]), _layer_spec(w_out.shape, w_out_layer),
                 vec] + xa_specs
        args = (xf, a, b, w_out, g1) + xa_args
        scratch.append(pltpu.VMEM(w_out.shape[1:], BF16))
    return pl.pallas_call(
        body,
        grid=(batch, nt),
        in_specs=specs,
        out_specs=row(d),
        out_shape=jax.ShapeDtypeStruct((n, d), F32),
        scratch_shapes=scratch,
        compiler_params=_cparams(2),
        name=name,
    )(*args)


FFN_TF = 256


def _ffn_kernel(x_ref, g4_ref, win_ref, wo_ref, g5_ref, o_ref, act_scr):
    dff = wo_ref.shape[0]
    x = x_ref[...]
    h = _rms(x, g4_ref[...]).astype(BF16)
    for c in range(dff // FFN_TF):
        cols = slice(c * FFN_TF, (c + 1) * FFN_TF)
        gt = _dot(h, win_ref[:, cols])
        up = _dot(h, win_ref[:, dff + c * FFN_TF: dff + (c + 1) * FFN_TF])
        act_scr[:, cols] = (gt * _sigmoid(gt) * up).astype(BF16)
    y = _dot(act_scr[...], wo_ref[...])
    o_ref[...] = x + _rms(y, g5_ref[...])


def _ffn(xf, g4, g5, w_in, w_out, layer, tm):
    n, d = xf.shape
    dff = w_out.shape[1]
    row = pl.BlockSpec((tm, d), lambda i: (i, 0))
    vec = _const_spec((1, d))
    return pl.pallas_call(
        _ffn_kernel,
        grid=(n // tm,),
        in_specs=[row, vec, _layer_spec(w_in.shape, layer), _layer_spec(w_out.shape, layer), vec],
        out_specs=row,
        out_shape=jax.ShapeDtypeStruct((n, d), F32),
        scratch_shapes=[pltpu.VMEM((tm, dff), BF16)],
        compiler_params=_cparams(1),
        name="ffn",
    )(xf, g4, w_in, w_out, g5)


def _relayout_even(w_in, w_uq, w_ukv, a_log, dt_bias):
    d = w_in.shape[0]
    qkvz = 3 * GDN_HEADS * GDN_DK + GDN_HEADS * GDN_DK
    ab = 2 * GDN_HEADS
    lat = w_in.shape[1] - qkvz - ab - MLA_ROPE
    pad = LANES - MLA_ROPE - ab
    w_in = w_in.astype(BF16)
    w_in_re = jnp.concatenate(
        [w_in[:, :qkvz], w_in[:, qkvz + ab:qkvz + ab + lat], w_in[:, qkvz + ab + lat:],
         w_in[:, qkvz:qkvz + ab], jnp.zeros((d, pad), w_in.dtype)], axis=1)
    r = w_uq.shape[0]
    uq = w_uq.reshape(r, MLA_HEADS, MLA_NOPE + MLA_ROPE)
    pe = jnp.pad(uq[:, :, MLA_NOPE:], ((0, 0), (0, 0), (0, LANES - MLA_ROPE)))
    wuq_re = jnp.concatenate([uq[:, :, :MLA_NOPE].reshape(r, -1), pe.reshape(r, -1)],
                             axis=1).astype(BF16)
    ukv = w_ukv.reshape(w_ukv.shape[0], MLA_HEADS, -1)
    wukv_re = jnp.concatenate([ukv[:, :, :MLA_NOPE].reshape(r, -1),
                               ukv[:, :, MLA_NOPE:].reshape(r, -1)], axis=1).astype(BF16)
    alog_row = jnp.zeros((1, LANES), F32).at[0, G_LANE:G_LANE + GDN_HEADS].set(a_log)
    dtb_row = jnp.zeros((1, LANES), F32).at[0, G_LANE:G_LANE + GDN_HEADS].set(dt_bias)
    return w_in_re, wuq_re, wukv_re, alog_row, dtb_row


def kernel(x, mem, positions, norm_gains, mem_norm, e_w_in, e_conv_w, e_a_log, e_dt_bias, e_o_norm,
           e_q_norm, e_kv_norm, e_w_uq, e_w_ukv, e_w_out, o_w_in, o_conv_w, o_conv_b, o_gate_a_w,
           o_gate_a_b, o_gate_x_w, o_gate_x_b, o_a_param, o_w_out, xa_wq, xa_wkv, xa_wo, ffn_w_in,
           ffn_w_out):
    batch, t, d = x.shape
    n = batch * t
    depth = norm_gains.shape[0]
    xf = x.reshape(n, d)
    gains = norm_gains.reshape(depth, -1, 1, d)

    cos_t, sin_t = _rope_tables(positions)
    kv_mem = _mem_kv(mem.reshape(-1, d), mem_norm, xa_wkv)
    ffn_w_in_b, ffn_w_out_b = ffn_w_in.astype(BF16), ffn_w_out.astype(BF16)

    for layer in range(depth):
        g = gains[layer]
        mix = None
        if layer % 2 == 0:
            e = layer // 2
            w_in_re, wuq_re, wukv_re, alog_row, dtb_row = _relayout_even(
                e_w_in[e], e_w_uq[e], e_w_ukv[e], e_a_log[e], e_dt_bias[e])
            q, k, v, z, gb, qf, kf, vb = _even_inproj(
                xf, g[0], w_in_re, e_conv_w[e], alog_row, dtb_row,
                e_q_norm[e].reshape(1, -1), e_kv_norm[e].reshape(1, -1), wuq_re, wukv_re,
                cos_t, sin_t, batch, tm=512)
            out_a = _gdn(q, k, v, z, gb, e_o_norm[e].reshape(1, -1), batch)
            out_b = _mla(qf, kf, vb, batch)
            mix = (out_a, out_b, e_w_out, e, g[1])
        else:
            o = layer // 2
            xf = _odd_mixer(xf, g[0], g[1], o_w_in, o_conv_w[o],
                            o_conv_b[o].reshape(1, d), o_gate_a_w,
                            o_gate_a_b[o].reshape(1, d), o_gate_x_w,
                            o_gate_x_b[o].reshape(1, d), o_a_param[o].reshape(1, d),
                            o_w_out, o, batch, tm=512)
        xf = _xattn(xf, g[2], g[3], xa_wq, kv_mem, xa_wo, layer, batch, tm=1024, mix=mix)
        xf = _ffn(xf, g[4], g[5], ffn_w_in_b, ffn_w_out_b, layer, tm=1024)
    return xf.reshape(batch, t, d)
```

```python
import functools
import math

import jax
import jax.numpy as jnp
from jax import lax
from jax.experimental import pallas as pl
from jax.experimental.pallas import tpu as pltpu

F32 = jnp.float32
BF16 = jnp.bfloat16

EPS = 1e-6
CHUNK = 64
GDN_HEADS = 4
GDN_DK = 128
MLA_HEADS = 4
MLA_NOPE = 128
MLA_ROPE = 64
MLA_SCALE = (MLA_NOPE + MLA_ROPE) ** -0.5
LOG2E = math.log2(math.e)
ROPE_BASE = 10000.0
LRU_BLOCKS = 4
LRU_C = 8.0
XA_HEADS = 4
CONV_W = 4

LANES = 128
CARRY_ROWS = 8
V7X_VMEM_LIMIT = 56 * 1024 * 1024

NT_DIMS = (((1,), (1,)), ((), ()))
TN_DIMS = (((0,), (0,)), ((), ()))


def _cparams(n_axes):
    return pltpu.CompilerParams(
        dimension_semantics=("arbitrary",) * n_axes,
        vmem_limit_bytes=V7X_VMEM_LIMIT)


def _dot(a, b):
    return jnp.dot(a, b, preferred_element_type=F32)


def _dot_nt(a, b):
    return lax.dot_general(a, b, NT_DIMS, preferred_element_type=F32)


def _rms(xf, g):
    ms = jnp.mean(xf * xf, axis=-1, keepdims=True)
    return xf * lax.rsqrt(ms + EPS) * g


def _sigmoid(x):
    return 1.0 / (1.0 + jnp.exp(-x))


def _softplus(x):
    return jnp.maximum(x, 0.0) + jnp.log1p(jnp.exp(-jnp.abs(x)))


def _const_spec(shape):
    nd = len(shape)
    return pl.BlockSpec(shape, lambda *_: (0,) * nd)


def _layer_spec(shape, layer):
    nd = len(shape) - 1
    return pl.BlockSpec((None,) + tuple(shape[1:]), lambda *_: (layer,) + (0,) * nd,
                        pipeline_mode=pl.Buffered(1))


def _rope_kernel(pos_ref, inv_ref, cos_ref, sin_ref):
    half = MLA_ROPE // 2
    per_row = LANES // half
    tr = pos_ref.shape[0]
    ang = pos_ref[...].astype(F32) * inv_ref[...]
    lane = lax.broadcasted_iota(jnp.int32, ang.shape, 1)
    for tbl, ref in ((jnp.cos(ang), cos_ref), (jnp.sin(ang), sin_ref)):
        for j in range(per_row):
            rep = jnp.where((lane >= half * j) & (lane < half * (j + 1)), tbl, 0.0)
            width = half
            while width < LANES:
                rep = rep + pltpu.roll(rep, width, axis=1)
                width *= 2
            ref[pl.ds(j, tr, stride=per_row), :] = rep


def _rope_tables(positions):
    n = positions.size
    half = MLA_ROPE // 2
    per_row = LANES // half
    inv_freq = ROPE_BASE ** (-jnp.arange(0, MLA_ROPE, 2, dtype=F32) / MLA_ROPE)
    inv_row = jnp.tile(inv_freq, per_row).reshape(1, LANES)
    pos_rep = jnp.repeat(positions.reshape(n // per_row, per_row), half, axis=1)
    rows = n // per_row
    tr = min(rows, 512)
    out_spec = pl.BlockSpec((tr * per_row, LANES), lambda i: (i, 0))
    return pl.pallas_call(
        _rope_kernel,
        grid=(rows // tr,),
        in_specs=[pl.BlockSpec((tr, LANES), lambda i: (i, 0)), _const_spec((1, LANES))],
        out_specs=[out_spec, out_spec],
        out_shape=[jax.ShapeDtypeStruct((n, LANES), F32)] * 2,
        compiler_params=_cparams(1),
        name="rope_tables",
    )(pos_rep, inv_row)


def _memkv_kernel(mem_ref, g_ref, w_ref, o_ref, wb_scr):
    @pl.when(pl.program_id(1) == 0)
    def _():
        wb_scr[...] = w_ref[...].astype(BF16)

    mn = _rms(mem_ref[...], g_ref[...]).astype(BF16)
    o_ref[...] = _dot(mn, wb_scr[...]).astype(BF16)


def _mem_kv(mem2d, mem_norm, wkv):
    depth, d, d2 = wkv.shape
    rows = mem2d.shape[0]
    tr = min(rows, 512)
    return pl.pallas_call(
        _memkv_kernel,
        grid=(depth, rows // tr),
        in_specs=[pl.BlockSpec((tr, d), lambda l, i: (i, 0)),
                  _const_spec((1, d)),
                  pl.BlockSpec((None, d, d2), lambda l, i: (l, 0, 0))],
        out_specs=pl.BlockSpec((None, tr, d2), lambda l, i: (l, i, 0)),
        out_shape=jax.ShapeDtypeStruct((depth, rows, d2), BF16),
        scratch_shapes=[pltpu.VMEM((d, d2), BF16)],
        compiler_params=_cparams(2),
        name="mem_kv",
    )(mem2d, mem_norm.reshape(1, d), wkv)


def _rope_tile(x, cos, sin, lane):
    half = MLA_ROPE // 2
    up = pltpu.roll(x, LANES - half, axis=1)
    dn = pltpu.roll(x, half, axis=1)
    rot = jnp.where(lane < half, -up, dn)
    return jnp.where(lane < MLA_ROPE, x * cos + rot * sin, 0.0)


ROW_PHASES = 8


def _causal_conv_phases(x, xslab, sl, cw_ref, ls, bias=None):
    assert cw_ref.shape[0] == CONV_W and CONV_W - 1 <= CARRY_ROWS
    tm = x.shape[0]
    grp = tm // ROW_PHASES
    xslab[sl, CARRY_ROWS:CARRY_ROWS + tm, :] = x
    ph = [xslab[sl, pl.ds(CARRY_ROWS + j, grp, stride=ROW_PHASES), :]
          for j in range(-(CONV_W - 1), ROW_PHASES)]
    xslab[sl, 0:CARRY_ROWS, :] = x[tm - CARRY_ROWS:, :]
    w = [cw_ref[i:i + 1, ls] for i in range(CONV_W)]
    out = []
    for j in range(ROW_PHASES):
        acc = w[CONV_W - 1] * ph[j + CONV_W - 1]
        if bias is not None:
            acc = acc + bias
        for i in range(CONV_W - 1):
            acc = acc + w[i] * ph[j + i]
        out.append(acc)
    return out


def _even_in_kernel(x_ref, g_ref, win_ref, cw_ref, alog_ref, dtb_ref, qn_ref, kvn_ref,
                    wuq_ref, wukv_ref, cos_ref, sin_ref,
                    q_out, k_out, v_out, z_out, gb_out, qf_out, kf_out, vb_out,
                    xslab, yslab):
    tm = x_ref.shape[0]
    qk_w = GDN_HEADS * GDN_DK
    qkv_w = 3 * qk_w

    @pl.when(pl.program_id(1) == 0)
    def _():
        xslab[:, 0:CARRY_ROWS, :] = jnp.zeros((qkv_w // LANES, CARRY_ROWS, LANES), F32)

    h = _rms(x_ref[...], g_ref[...]).astype(BF16)
    lane = lax.broadcasted_iota(jnp.int32, (tm, LANES), 1)
    cos = cos_ref[...]
    sin = sin_ref[...]
    nope_w = MLA_HEADS * MLA_NOPE
    z_off = qkv_w
    cq_off = z_off + qk_w
    ckv_off = cq_off + qn_ref.shape[1]
    misc_off = ckv_off + kvn_ref.shape[1]
    gw = 2 * GDN_DK
    st = {}

    def proj_cols(lo, width):
        return _dot(h, win_ref[:, lo:lo + width])

    def mm_cq():
        st["cq"] = proj_cols(cq_off, qn_ref.shape[1])

    def ep_cq():
        st["cqb"] = _rms(st.pop("cq"), qn_ref[...] * (MLA_SCALE * LOG2E)).astype(BF16)

    def mm_ckv():
        st["ckv"] = proj_cols(ckv_off, kvn_ref.shape[1])

    def ep_ckv():
        st["ckvb"] = _rms(st.pop("ckv"), kvn_ref[...]).astype(BF16)

    def mm_misc():
        st["misc"] = proj_cols(misc_off, LANES)

    def ep_misc():
        misc = st.pop("misc")
        gdec = -jnp.exp(alog_ref[...]) * _softplus(misc + dtb_ref[...])
        beta = _sigmoid(misc)
        gb_out[...] = jnp.where((lane >= G_LANE) & (lane < G_LANE + GDN_HEADS), gdec,
                                jnp.where((lane >= B_LANE) & (lane < B_LANE + GDN_HEADS),
                                          beta, 0.0))
        st["kpe"] = _rope_tile(misc, cos, sin, lane).astype(BF16)

    def mm_uq():
        st["qf"] = _dot(st.pop("cqb"), wuq_ref[...])

    def ep_uq():
        qf = st.pop("qf")
        for hh in range(MLA_HEADS):
            base = 2 * LANES * hh
            qf_out[:, base:base + LANES] = qf[:, hh * LANES:(hh + 1) * LANES].astype(BF16)
            pe = qf[:, nope_w + hh * LANES: nope_w + (hh + 1) * LANES]
            qf_out[:, base + LANES:base + 2 * LANES] = _rope_tile(pe, cos, sin, lane).astype(BF16)

    def mm_ukv():
        st["kv"] = _dot(st.pop("ckvb"), wukv_ref[...])

    def ep_ukv():
        kv = st.pop("kv")
        for hh in range(MLA_HEADS):
            base = 2 * LANES * hh
            kf_out[:, base:base + LANES] = kv[:, hh * LANES:(hh + 1) * LANES].astype(BF16)
            kf_out[:, base + LANES:base + 2 * LANES] = st["kpe"]
        vb_out[...] = kv[:, nope_w:].astype(BF16)

    def mm_group(lo):
        def run():
            st[lo] = proj_cols(lo, gw)
        return run

    def ep_qkv(lo):
        def run():
            raw = st.pop(lo)
            which, off = divmod(lo, qk_w)
            out, scale = ((q_out, GDN_DK ** -0.5), (k_out, 1.0), (v_out, None))[which]
            grp = tm // ROW_PHASES
            for sub in range(gw // LANES):
                sl = lo // LANES + sub
                ls = slice(lo + sub * LANES, lo + (sub + 1) * LANES)
                blocks = _causal_conv_phases(raw[:, sub * LANES:(sub + 1) * LANES], xslab, sl,
                                             cw_ref, ls)
                for j, acc in enumerate(blocks):
                    a = acc * _sigmoid(acc)
                    if scale is not None:
                        a = a * (lax.rsqrt(jnp.sum(a * a, axis=-1, keepdims=True) + EPS) * scale)
                    yslab[sl, pl.ds(j, grp, stride=ROW_PHASES), :] = a
                out[:, off + sub * LANES: off + (sub + 1) * LANES] = yslab[sl].astype(BF16)
        return run

    def ep_z(lo):
        def run():
            z_out[:, lo - z_off: lo - z_off + gw] = st.pop(lo).astype(BF16)
        return run

    pairs = [(mm_group(lo), ep_qkv(lo)) for lo in range(0, qkv_w, gw)]
    pairs += [(mm_cq, ep_cq), (mm_ckv, ep_ckv), (mm_misc, ep_misc), (mm_uq, ep_uq), (mm_ukv, ep_ukv)]
    pairs += [(mm_group(lo), ep_z(lo)) for lo in range(z_off, cq_off, gw)]
    for mm, ep in pairs:
        mm()
        ep()


def _even_inproj(xf, g0, w_in_re, conv_w, alog_row, dtb_row, q_norm, kv_norm, wuq_re, wukv_re,
                 cos_t, sin_t, batch, tm):
    n, d = xf.shape
    t = n // batch
    nt = t // tm
    e_in = w_in_re.shape[1]
    qkv_w = conv_w.shape[1]
    qk_w = qkv_w // 3
    row = lambda w: pl.BlockSpec((tm, w), lambda b, i: (b * nt + i, 0))
    outs = [(qk_w, BF16)] * 4 + [(LANES, F32), (2 * LANES * MLA_HEADS, BF16),
                                 (2 * LANES * MLA_HEADS, BF16), (LANES * MLA_HEADS, BF16)]
    return pl.pallas_call(
        _even_in_kernel,
        grid=(batch, nt),
        in_specs=[row(d), _const_spec((1, d)), _const_spec((d, e_in)), _const_spec((CONV_W, qkv_w)),
                  _const_spec((1, LANES)), _const_spec((1, LANES)),
                  _const_spec(q_norm.shape), _const_spec(kv_norm.shape),
                  _const_spec(wuq_re.shape), _const_spec(wukv_re.shape),
                  row(LANES), row(LANES)],
        out_specs=[row(w) for w, _ in outs],
        out_shape=[jax.ShapeDtypeStruct((n, w), dt) for w, dt in outs],
        scratch_shapes=[pltpu.VMEM((qkv_w // LANES, tm + CARRY_ROWS, LANES), F32),
                        pltpu.VMEM((qkv_w // LANES, tm, LANES), F32)],
        compiler_params=_cparams(2),
        name="even_inproj",
    )(xf, g0, w_in_re, conv_w, alog_row, dtb_row, q_norm, kv_norm, wuq_re, wukv_re, cos_t, sin_t)


SUPER = 2 * CHUNK
G_LANE = MLA_ROPE
B_LANE = MLA_ROPE + GDN_HEADS
GDN_PIPE = 2


def _gdn_kernel(q_ref, k_ref, v_ref, z_ref, gb_ref, on_ref, out_ref,
                gc_scr, gct_scr, gl_scr, u_scr, w_scr, a_scr, qe_scr, kd_scr, s_scr):
    t = q_ref.shape[0]
    n_super = t // SUPER
    n_chunk = t // CHUNK

    gb = gb_ref[...]
    rowi = lax.broadcasted_iota(jnp.int32, gb.shape, 0) & (CHUNK - 1)
    gc = gb
    s = 1
    while s < CHUNK:
        gc = jnp.where(rowi >= s, gc + pltpu.roll(gc, s, axis=0), gc)
        s *= 2
    gc_scr[...] = gc
    gct_scr[...] = gc.T
    g3 = gc.reshape(n_chunk, CHUNK, LANES)
    gl_scr[...] = jnp.broadcast_to(g3[:, CHUNK - 1:CHUNK, :], g3.shape).reshape(t, LANES)

    ri = lax.broadcasted_iota(jnp.int32, (SUPER, SUPER), 0)
    ci = lax.broadcasted_iota(jnp.int32, (SUPER, SUPER), 1)
    same = (ri >= CHUNK) == (ci >= CHUNK)
    causal = same & (ri >= ci)
    strict = same & (ri > ci)
    eye = (ri == ci).astype(F32)

    on = on_ref[...]
    heads = range(GDN_HEADS)
    hsl = [slice(hh * GDN_DK, (hh + 1) * GDN_DK) for hh in heads]
    lane_g = [slice(G_LANE + hh, G_LANE + hh + 1) for hh in heads]
    lane_b = [slice(B_LANE + hh, B_LANE + hh + 1) for hh in heads]

    def solve_stages(sc0):
        probs = []
        for sub in range(GDN_PIPE):
            rows = pl.ds(pl.multiple_of((sc0 + sub) * SUPER, SUPER), SUPER)
            probs += [(rows, hh) for hh in heads]
        kc = [k_ref[rows, hsl[hh]] for rows, hh in probs]
        qc = [q_ref[rows, hsl[hh]] for rows, hh in probs]
        gcol = [gc_scr[rows, lane_g[hh]] for rows, hh in probs]
        bcol = [gb_ref[rows, lane_b[hh]] for rows, hh in probs]
        grow = [gct_scr[lane_g[hh], rows] for rows, hh in probs]
        kk = [_dot_nt(k, k) for k in kc]
        qk = [_dot_nt(q, k) for q, k in zip(qc, kc)]
        yield
        decay = [jnp.where(causal, jnp.exp(jnp.where(causal, gc - gr, 0.0)), 0.0)
                 for gc, gr in zip(gcol, grow)]
        m = [jnp.where(strict, -(x * b * dc), 0.0) for x, b, dc in zip(kk, bcol, decay)]
        qacc = [eye + x for x in m]
        mb = [x.astype(BF16) for x in m]
        mj = [_dot(x, x) for x in mb]
        yield
        lvl = 2
        while lvl < CHUNK // 2:
            mb = [x.astype(BF16) for x in mj]
            r = [_dot(x, jnp.concatenate([x, qa.astype(BF16)], axis=1)) for x, qa in zip(mb, qacc)]
            yield
            mj = [x[:, :SUPER] for x in r]
            qacc = [qa + x[:, SUPER:] for qa, x in zip(qacc, r)]
            lvl *= 2
        last = [_dot(x.astype(BF16), qa.astype(BF16)) for qa, x in zip(qacc, mj)]
        yield
        qacc = [qa + x for qa, x in zip(qacc, last)]
        eg = [jnp.exp(gc) for gc in gcol]
        sol = []
        for i, (rows, hh) in enumerate(probs):
            vc = v_ref[rows, hsl[hh]].astype(F32)
            kf = kc[i].astype(F32)
            rhs = jnp.concatenate([vc * bcol[i], kf * (bcol[i] * eg[i])], axis=1).astype(BF16)
            sol.append(_dot(qacc[i].astype(BF16), rhs))
        yield
        for i, (rows, hh) in enumerate(probs):
            u_scr[rows, hsl[hh]] = sol[i][:, :GDN_DK]
            w_scr[rows, hsl[hh]] = sol[i][:, GDN_DK:].astype(BF16)
            aqk = qk[i] * decay[i]
            a_sh = pltpu.roll(aqk, CHUNK, axis=1)
            a_scr[rows, hsl[hh]] = jnp.where(ri < CHUNK, aqk, a_sh).astype(BF16)
            qe_scr[rows, hsl[hh]] = (qc[i].astype(F32) * eg[i]).astype(BF16)
            glast = gl_scr[rows, lane_g[hh]]
            kd_scr[rows, hsl[hh]] = (kc[i].astype(F32) * jnp.exp(glast - gcol[i])).astype(BF16)

    def scan_stages(sc0, st):
        for half in range(GDN_PIPE * SUPER // CHUNK):
            r0 = pl.multiple_of(sc0 * SUPER + half * CHUNK, CHUNK)
            rows = pl.ds(r0, CHUNK)
            sb = [x.astype(BF16) for x in st]
            t1 = [_dot(w_scr[rows, hsl[hh]], sb[hh]) for hh in heads]
            oq = [_dot(qe_scr[rows, hsl[hh]], sb[hh]) for hh in heads]
            yield
            vb = [(u_scr[rows, hsl[hh]] - t1[hh]).astype(BF16) for hh in heads]
            oa = [_dot(a_scr[rows, hsl[hh]][:, :CHUNK], vb[hh]) for hh in heads]
            kv = [lax.dot_general(kd_scr[rows, hsl[hh]], vb[hh], TN_DIMS,
                                  preferred_element_type=F32) for hh in heads]
            yield
            for hh in heads:
                gam = jnp.exp(gl_scr[pl.ds(r0, 1), lane_g[hh]])
                st[hh] = st[hh] * gam + kv[hh]
                z = z_ref[rows, hsl[hh]].astype(F32)
                out_ref[rows, hsl[hh]] = (
                    _rms(oq[hh] + oa[hh], on) * (z * _sigmoid(z))).astype(BF16)

    _round_robin([solve_stages(0)])
    s_scr[...] = jnp.zeros(s_scr.shape, F32)
    n_steps = n_super // GDN_PIPE

    def pipe_body(it, _):
        st = [s_scr[hh] for hh in heads]
        _round_robin([solve_stages((it + 1) * GDN_PIPE), scan_stages(it * GDN_PIPE, st)])
        for hh in heads:
            s_scr[hh] = st[hh]
        return 0

    lax.fori_loop(0, n_steps - 1, pipe_body, 0)
    st = [s_scr[hh] for hh in heads]
    _round_robin([scan_stages((n_steps - 1) * GDN_PIPE, st)])


def _gdn(q, k, v, z, gb, o_norm, batch):
    n, w = q.shape
    t = n // batch
    row = lambda ww: pl.BlockSpec((t, ww), lambda b: (b, 0))
    return pl.pallas_call(
        _gdn_kernel,
        grid=(batch,),
        in_specs=[row(w), row(w), row(w), row(w), row(LANES), _const_spec((1, GDN_DK))],
        out_specs=row(w),
        out_shape=jax.ShapeDtypeStruct((n, w), BF16),
        scratch_shapes=[pltpu.VMEM((t, LANES), F32),
                        pltpu.VMEM((LANES, t), F32),
                        pltpu.VMEM((t, LANES), F32),
                        pltpu.VMEM((t, w), F32),
                        pltpu.VMEM((t, w), BF16),
                        pltpu.VMEM((t, w), BF16),
                        pltpu.VMEM((t, w), BF16),
                        pltpu.VMEM((t, w), BF16),
                        pltpu.VMEM((GDN_HEADS, GDN_DK, GDN_DK), F32)],
        compiler_params=_cparams(1),
        name="gdn",
    )(q, k, v, z, gb, o_norm)


MLA_TQ = 256


def _mla_kernel(q_ref, k_ref, v_ref, *rest):
    n_cast = (len(rest) - 1) // 2
    o_ref = rest[n_cast]
    for src, dst in zip(rest[:n_cast], rest[n_cast + 1:]):
        dst[...] = src[...].astype(BF16)
    t = q_ref.shape[0]
    tq = MLA_TQ
    nq = t // tq
    hw = 2 * LANES
    heads = range(q_ref.shape[1] // hw)
    ri = lax.broadcasted_iota(jnp.int32, (tq, tq), 0) // CHUNK
    ci = lax.broadcasted_iota(jnp.int32, (tq, tq), 1) // CHUNK
    diag_mask = ci <= ri

    groups = [(i, h) for i in range(nq) for h in heads]

    def scores(i, h):
        kw = (i + 1) * tq
        return _dot_nt(q_ref[i * tq:(i + 1) * tq, h * hw:(h + 1) * hw],
                       k_ref[0:kw, h * hw:(h + 1) * hw])

    s_next = scores(*groups[0])
    for g, (i, h) in enumerate(groups):
        s = s_next
        if g + 1 < len(groups):
            s_next = scores(*groups[g + 1])
        kw = (i + 1) * tq
        s_diag = jnp.where(diag_mask, s[:, kw - tq:], -jnp.inf)
        s = s_diag if i == 0 else jnp.concatenate([s[:, :kw - tq], s_diag], axis=1)
        m = jnp.max(s, axis=-1, keepdims=True)
        p = jnp.exp2(s - m)
        l = jnp.sum(p, axis=-1, keepdims=True)
        pv = _dot(p.astype(BF16), v_ref[0:kw, h * LANES:(h + 1) * LANES])
        o_ref[i * tq:(i + 1) * tq, h * LANES:(h + 1) * LANES] = (pv / l).astype(BF16)


MLA_HEADS_PER_STEP = 2


def _mla(qf, kf, vb, batch, cast_weights=()):
    n = qf.shape[0]
    t = n // batch
    hps = MLA_HEADS_PER_STEP
    hg = MLA_HEADS // hps
    steps = batch * hg
    spec = lambda w: pl.BlockSpec((t, w * hps), lambda b, g: (b, g))
    cast_specs = [pl.BlockSpec((w.shape[0], w.shape[1] // steps, w.shape[2]),
                               lambda b, g: (0, b * hg + g, 0)) for w in cast_weights]
    outs = pl.pallas_call(
        _mla_kernel,
        grid=(batch, hg),
        in_specs=[spec(2 * LANES), spec(2 * LANES), spec(LANES)] + cast_specs,
        out_specs=[spec(LANES)] + cast_specs,
        out_shape=[jax.ShapeDtypeStruct((n, vb.shape[1]), BF16)]
                  + [jax.ShapeDtypeStruct(w.shape, BF16) for w in cast_weights],
        compiler_params=_cparams(2),
        name="mla_attn",
    )(qf, kf, vb, *cast_weights)
    return outs[0], outs[1:]


SCAN_PHASES = ROW_PHASES


def _gelu_tanh(x):
    c2 = 2.0 * math.sqrt(2.0 / math.pi) * LOG2E
    return x / (1.0 + jnp.exp2(x * (-c2 - (c2 * 0.044715) * (x * x))))


def _odd_kernel(x_ref, g0_ref, win_f32, cw_ref, cb_ref, gaw_f32, gab_ref, gxw_f32, gxb_ref,
                ap_ref, wout_f32, g1_ref, o_ref, xslab, hslab, hcar,
                win_ref, gaw_ref, gxw_ref, wout_ref):
    tm, d = x_ref.shape
    bw = d // LRU_BLOCKS
    n_slab = d // LANES
    assert cw_ref.shape[0] == CONV_W and CONV_W - 1 <= CARRY_ROWS
    _cast_once([(win_f32, win_ref), (gaw_f32, gaw_ref), (gxw_f32, gxw_ref), (wout_f32, wout_ref)])

    @pl.when(pl.program_id(1) == 0)
    def _():
        xslab[:, 0:CARRY_ROWS, :] = jnp.zeros((n_slab, CARRY_ROWS, LANES), F32)
        hcar[...] = jnp.zeros(hcar.shape, F32)

    x = x_ref[...]
    h = _rms(x, g0_ref[...]).astype(BF16)
    grp = tm // SCAN_PHASES
    slabs_per_block = bw // LANES
    rowg = lax.broadcasted_iota(jnp.int32, (grp, 1), 0)

    blocks = range(LRU_BLOCKS)
    csl = [slice(n * bw, (n + 1) * bw) for n in blocks]
    st = [dict() for _ in blocks]

    def projx(n):
        st[n]["xb"] = _dot(h, win_ref[:, csl[n]])

    def projy(n):
        st[n]["yb"] = _dot(h, win_ref[:, d + n * bw: d + (n + 1) * bw])

    def conv(n):
        xb = st[n].pop("xb")
        cols = []
        for k in range(slabs_per_block):
            sl = n * slabs_per_block + k
            ls = slice(n * bw + k * LANES, n * bw + (k + 1) * LANES)
            out = _causal_conv_phases(xb[:, k * LANES:(k + 1) * LANES], xslab, sl, cw_ref, ls,
                                      bias=cb_ref[:, ls])
            cols.append(jnp.concatenate(out, axis=0))
        st[n]["xc"] = jnp.concatenate(cols, axis=1)

    def gates(n):
        xcb = st[n]["xc"].astype(BF16)
        st[n]["ra"] = _dot(xcb, gaw_ref[n])
        st[n]["ia"] = _dot(xcb, gxw_ref[n])

    def recur(n):
        cs, xc = csl[n], st[n].pop("xc")
        r = _sigmoid(st[n].pop("ra") + gab_ref[:, cs])
        ig = _sigmoid(st[n].pop("ia") + gxb_ref[:, cs])
        a = jnp.exp2(r * ((-LRU_C * LOG2E) * _softplus(-ap_ref[:, cs])))
        om = 1.0 - a * a
        u = jnp.where(om > 0.0, om * lax.rsqrt(om), 0.0) * (ig * xc)

        ca, cu = a[0:grp], u[0:grp]
        la, lu = [ca], [cu]
        for j in range(1, SCAN_PHASES):
            aj, uj = a[j * grp:(j + 1) * grp], u[j * grp:(j + 1) * grp]
            cu = aj * cu + uj
            ca = aj * ca
            la.append(ca)
            lu.append(cu)
        ga, gu = ca, cu
        s = 1
        while s < grp:
            valid = rowg >= s
            gu = jnp.where(valid, ga * pltpu.roll(gu, s, axis=0) + gu, gu)
            ga = jnp.where(valid, ga * pltpu.roll(ga, s, axis=0), ga)
            s *= 2
        h_in = hcar[:, cs]
        h_end = ga * h_in + gu
        hcar[:, cs] = h_end[grp - 1:grp, :]
        h_prev = jnp.where(rowg == 0, h_in, pltpu.roll(h_end, 1, axis=0))
        slabs = range(n * slabs_per_block, (n + 1) * slabs_per_block)
        for j in range(SCAN_PHASES):
            hj = la[j] * h_prev + lu[j]
            for k, sl in enumerate(slabs):
                hslab[sl, pl.ds(j, grp, stride=SCAN_PHASES), :] = hj[:, k * LANES:(k + 1) * LANES]
        st[n]["hs"] = jnp.concatenate([hslab[sl] for sl in slabs], axis=1)

    def gate(n):
        st[n]["hg"] = (st[n].pop("hs") * _gelu_tanh(st[n].pop("yb"))).astype(BF16)

    def outp(n):
        st[n]["y"] = _dot(st[n].pop("hg"), wout_ref[csl[n], :])

    chain = (projx, conv, gates, recur, projy, gate, outp)
    for k in range(LRU_BLOCKS + len(chain) - 1):
        for s, stage in enumerate(chain):
            if 0 <= k - s < LRU_BLOCKS:
                stage(k - s)
    y = functools.reduce(lambda p, q: p + q, [st[n]["y"] for n in blocks])
    o_ref[...] = x + _rms(y, g1_ref[...])


def _odd_mixer(xf, g0, g1, w_in, conv_w, conv_b, gaw, gab, gxw, gxb, a_param, w_out, o, batch, tm):
    n, d = xf.shape
    t = n // batch
    nt = t // tm
    row = pl.BlockSpec((tm, d), lambda b, i: (b * nt + i, 0))
    vec = _const_spec((1, d))
    weights = (w_in, gaw, gxw, w_out)
    wspec = [_layer_spec(w.shape, o) for w in weights]
    return pl.pallas_call(
        _odd_kernel,
        grid=(batch, nt),
        in_specs=[row, vec, wspec[0], _const_spec(conv_w.shape), vec,
                  wspec[1], vec, wspec[2], vec, vec, wspec[3], vec],
        out_specs=row,
        out_shape=jax.ShapeDtypeStruct((n, d), F32),
        scratch_shapes=[pltpu.VMEM((d // LANES, tm + CARRY_ROWS, LANES), F32),
                        pltpu.VMEM((d // LANES, tm, LANES), F32),
                        pltpu.VMEM((1, d), F32)]
                       + [pltpu.VMEM(w.shape[1:], BF16) for w in weights],
        compiler_params=_cparams(2),
        name="odd_mixer",
    )(xf, g0, w_in, conv_w, conv_b, gaw, gab, gxw, gxb, a_param, w_out, g1)


XA_SPLIT = 2


def _round_robin(gens):
    live = list(gens)
    while live:
        live = [g for g in live if next(g, StopIteration) is not StopIteration]


def _cast_once(pairs, n_axes=2):
    first = pl.program_id(0) == 0
    for axis in range(1, n_axes):
        first = first & (pl.program_id(axis) == 0)

    @pl.when(first)
    def _():
        for src, dst in pairs:
            dst[...] = src[...].astype(BF16)


def _xattn_kernel(x_ref, g2_ref, wq_f32, kv_ref, wo_f32, g3_ref, o_ref, wq_ref, wo_ref):
    _cast_once([(wq_f32, wq_ref), (wo_f32, wo_ref)])
    tm = x_ref.shape[0]
    rs = tm // XA_SPLIT
    _round_robin([_xattn_stages(slice(k * rs, (k + 1) * rs), x_ref, None, g2_ref, wq_ref, kv_ref,
                                wo_ref, g3_ref, o_ref) for k in range(XA_SPLIT)])


def _outproj_xattn_kernel(x_ref, a_ref, b_ref, w_f32, g1_ref,
                          g2_ref, wq_f32, kv_ref, wo_f32, g3_ref, o_ref, wq_ref, wo_ref, w_ref):
    _cast_once([(wq_f32, wq_ref), (wo_f32, wo_ref), (w_f32, w_ref)])
    tm = x_ref.shape[0]
    rs = tm // XA_SPLIT
    _round_robin([_xattn_stages(slice(k * rs, (k + 1) * rs), x_ref, (a_ref, b_ref, w_ref, g1_ref),
                                g2_ref, wq_ref, kv_ref, wo_ref, g3_ref, o_ref)
                  for k in range(XA_SPLIT)])


def _xattn_stages(rows, x_ref, mix, g2_ref, wq_ref, kv_ref, wo_ref, g3_ref, o_ref):
    x = x_ref[rows, :]
    d = x.shape[1]
    hd = d // XA_HEADS
    if mix is not None:
        a_ref, b_ref, w_ref, g1_ref = mix
        ka = a_ref.shape[1]
        y = _dot(a_ref[rows, :], w_ref[0:ka, :]) + _dot(b_ref[rows, :], w_ref[ka:, :])
        yield
        x = x + _rms(y, g1_ref[...])
    h = _rms(x, g2_ref[...]).astype(BF16)
    q = _dot(h, wq_ref[...])
    yield
    q = (q * (hd ** -0.5 * LOG2E)).astype(BF16)
    s = [_dot_nt(q[:, hh * hd:(hh + 1) * hd], kv_ref[:, hh * hd:(hh + 1) * hd])
         for hh in range(XA_HEADS)]
    yield
    outs = []
    for hh in range(XA_HEADS):
        m = jnp.max(s[hh], axis=-1, keepdims=True)
        p = jnp.exp2(s[hh] - m)
        l = jnp.sum(p, axis=-1, keepdims=True)
        pv = _dot(p.astype(BF16), kv_ref[:, d + hh * hd: d + (hh + 1) * hd])
        outs.append((pv / l).astype(BF16))
    yield
    y = _dot(jnp.concatenate(outs, axis=1), wo_ref[...])
    yield
    o_ref[rows, :] = x + _rms(y, g3_ref[...])


def _xattn(xf, g2, g3, wq, kv_mem, wo, layer, batch, tm, mix=None):
    n, d = xf.shape
    t = n // batch
    nt = t // tm
    n_mem = kv_mem.shape[1] // batch
    row = lambda w: pl.BlockSpec((tm, w), lambda b, i: (b * nt + i, 0))
    vec = _const_spec((1, d))
    xa_specs = [vec, _layer_spec(wq.shape, layer),
                pl.BlockSpec((None, n_mem, 2 * d), lambda b, i: (layer, b, 0)),
                _layer_spec(wo.shape, layer), vec]
    xa_args = (g2, wq, kv_mem, wo, g3)
    scratch = [pltpu.VMEM(wq.shape[1:], BF16), pltpu.VMEM(wo.shape[1:], BF16)]
    if mix is None:
        body, specs, args, name = _xattn_kernel, [row(d)] + xa_specs, (xf,) + xa_args, "xattn"
    else:
        a, b, w_out, w_out_layer, g1 = mix
        body, name = _outproj_xattn_kernel, "outproj_xattn"
        specs = [row(d), row(a.shape[1]), row(b.shape[1]), _layer_spec(w_out.shape, w_out_layer),
                 vec] + xa_specs
        args = (xf, a, b, w_out, g1) + xa_args
        scratch.append(pltpu.VMEM(w_out.shape[1:], BF16))
    return pl.pallas_call(
        body,
        grid=(batch, nt),
        in_specs=specs,
        out_specs=row(d),
        out_shape=jax.ShapeDtypeStruct((n, d), F32),
        scratch_shapes=scratch,
        compiler_params=_cparams(2),
        name=name,
    )(*args)


FFN_TF = 256


def _ffn_kernel(x_ref, g4_ref, win_ref, wo_ref, g5_ref, o_ref, act_scr):
    dff = wo_ref.shape[0]
    x = x_ref[...]
    h = _rms(x, g4_ref[...]).astype(BF16)
    for c in range(dff // FFN_TF):
        cols = slice(c * FFN_TF, (c + 1) * FFN_TF)
        gt = _dot(h, win_ref[:, cols])
        up = _dot(h, win_ref[:, dff + c * FFN_TF: dff + (c + 1) * FFN_TF])
        act_scr[:, cols] = (gt * _sigmoid(gt) * up).astype(BF16)
    y = _dot(act_scr[...], wo_ref[...])
    o_ref[...] = x + _rms(y, g5_ref[...])


def _ffn(xf, g4, g5, w_in, w_out, layer, tm):
    n, d = xf.shape
    dff = w_out.shape[1]
    row = pl.BlockSpec((tm, d), lambda i: (i, 0))
    vec = _const_spec((1, d))
    return pl.pallas_call(
        _ffn_kernel,
        grid=(n // tm,),
        in_specs=[row, vec, _layer_spec(w_in.shape, layer), _layer_spec(w_out.shape, layer), vec],
        out_specs=row,
        out_shape=jax.ShapeDtypeStruct((n, d), F32),
        scratch_shapes=[pltpu.VMEM((tm, dff), BF16)],
        compiler_params=_cparams(1),
        name="ffn",
    )(xf, g4, w_in, w_out, g5)


def _relayout_even(w_in, w_uq, w_ukv, a_log, dt_bias):
    d = w_in.shape[0]
    qkvz = 3 * GDN_HEADS * GDN_DK + GDN_HEADS * GDN_DK
    ab = 2 * GDN_HEADS
    lat = w_in.shape[1] - qkvz - ab - MLA_ROPE
    pad = LANES - MLA_ROPE - ab
    w_in = w_in.astype(BF16)
    w_in_re = jnp.concatenate(
        [w_in[:, :qkvz], w_in[:, qkvz + ab:qkvz + ab + lat], w_in[:, qkvz + ab + lat:],
         w_in[:, qkvz:qkvz + ab], jnp.zeros((d, pad), w_in.dtype)], axis=1)
    r = w_uq.shape[0]
    uq = w_uq.reshape(r, MLA_HEADS, MLA_NOPE + MLA_ROPE)
    pe = jnp.pad(uq[:, :, MLA_NOPE:], ((0, 0), (0, 0), (0, LANES - MLA_ROPE)))
    wuq_re = jnp.concatenate([uq[:, :, :MLA_NOPE].reshape(r, -1), pe.reshape(r, -1)],
                             axis=1).astype(BF16)
    ukv = w_ukv.reshape(w_ukv.shape[0], MLA_HEADS, -1)
    wukv_re = jnp.concatenate([ukv[:, :, :MLA_NOPE].reshape(r, -1),
                               ukv[:, :, MLA_NOPE:].reshape(r, -1)], axis=1).astype(BF16)
    alog_row = jnp.zeros((1, LANES), F32).at[0, G_LANE:G_LANE + GDN_HEADS].set(a_log)
    dtb_row = jnp.zeros((1, LANES), F32).at[0, G_LANE:G_LANE + GDN_HEADS].set(dt_bias)
    return w_in_re, wuq_re, wukv_re, alog_row, dtb_row


def kernel(x, mem, positions, norm_gains, mem_norm, e_w_in, e_conv_w, e_a_log, e_dt_bias, e_o_norm,
           e_q_norm, e_kv_norm, e_w_uq, e_w_ukv, e_w_out, o_w_in, o_conv_w, o_conv_b, o_gate_a_w,
           o_gate_a_b, o_gate_x_w, o_gate_x_b, o_a_param, o_w_out, xa_wq, xa_wkv, xa_wo, ffn_w_in,
           ffn_w_out):
    batch, t, d = x.shape
    n = batch * t
    depth = norm_gains.shape[0]
    xf = x.reshape(n, d)
    gains = norm_gains.reshape(depth, -1, 1, d)

    cos_t, sin_t = _rope_tables(positions)
    kv_mem = _mem_kv(mem.reshape(-1, d), mem_norm, xa_wkv)
    ffn_w_b = None

    for layer in range(depth):
        g = gains[layer]
        mix = None
        if layer % 2 == 0:
            e = layer // 2
            w_in_re, wuq_re, wukv_re, alog_row, dtb_row = _relayout_even(
                e_w_in[e], e_w_uq[e], e_w_ukv[e], e_a_log[e], e_dt_bias[e])
            q, k, v, z, gb, qf, kf, vb = _even_inproj(
                xf, g[0], w_in_re, e_conv_w[e], alog_row, dtb_row,
                e_q_norm[e].reshape(1, -1), e_kv_norm[e].reshape(1, -1), wuq_re, wukv_re,
                cos_t, sin_t, batch, tm=512)
            out_a = _gdn(q, k, v, z, gb, e_o_norm[e].reshape(1, -1), batch)
            if ffn_w_b is None:
                out_b, ffn_w_b = _mla(qf, kf, vb, batch, cast_weights=(ffn_w_in, ffn_w_out))
            else:
                out_b, _ = _mla(qf, kf, vb, batch)
            mix = (out_a, out_b, e_w_out, e, g[1])
        else:
            o = layer // 2
            xf = _odd_mixer(xf, g[0], g[1], o_w_in, o_conv_w[o],
                            o_conv_b[o].reshape(1, d), o_gate_a_w,
                            o_gate_a_b[o].reshape(1, d), o_gate_x_w,
                            o_gate_x_b[o].reshape(1, d), o_a_param[o].reshape(1, d),
                            o_w_out, o, batch, tm=512)
        xf = _xattn(xf, g[2], g[3], xa_wq, kv_mem, xa_wo, layer, batch, tm=1024, mix=mix)
        xf = _ffn(xf, g[4], g[5], ffn_w_b[0], ffn_w_b[1], layer, tm=1024)
    return xf.reshape(batch, t, d)
```

```python
import functools
import math

import jax
import jax.numpy as jnp
from jax import lax
from jax.experimental import pallas as pl
from jax.experimental.pallas import tpu as pltpu

F32 = jnp.float32
BF16 = jnp.bfloat16

EPS = 1e-6
CHUNK = 64
GDN_HEADS = 4
GDN_DK = 128
MLA_HEADS = 4
MLA_NOPE = 128
MLA_ROPE = 64
MLA_SCALE = (MLA_NOPE + MLA_ROPE) ** -0.5
LOG2E = math.log2(math.e)
ROPE_BASE = 10000.0
LRU_BLOCKS = 4
LRU_C = 8.0
XA_HEADS = 4
CONV_W = 4

LANES = 128
CARRY_ROWS = 8
V7X_VMEM_LIMIT = 56 * 1024 * 1024

ROW_TILE = {"even_inproj": 512, "odd_mixer": 512, "xattn": 1024, "ffn": 1024}

NT_DIMS = (((1,), (1,)), ((), ()))
TN_DIMS = (((0,), (0,)), ((), ()))


def _cparams(n_axes):
    return pltpu.CompilerParams(
        dimension_semantics=("arbitrary",) * n_axes,
        vmem_limit_bytes=V7X_VMEM_LIMIT)


def _dot(a, b):
    return jnp.dot(a, b, preferred_element_type=F32)


def _dot_nt(a, b):
    return lax.dot_general(a, b, NT_DIMS, preferred_element_type=F32)


def _rms(xf, g):
    ms = jnp.mean(xf * xf, axis=-1, keepdims=True)
    return xf * lax.rsqrt(ms + EPS) * g


def _sigmoid(x):
    return 1.0 / (1.0 + jnp.exp(-x))


def _softplus(x):
    return jnp.maximum(x, 0.0) + jnp.log1p(jnp.exp(-jnp.abs(x)))


def _const_spec(shape):
    nd = len(shape)
    return pl.BlockSpec(shape, lambda *_: (0,) * nd)


def _layer_spec(shape, layer):
    nd = len(shape) - 1
    return pl.BlockSpec((None,) + tuple(shape[1:]), lambda *_: (layer,) + (0,) * nd,
                        pipeline_mode=pl.Buffered(1))


def _rope_kernel(pos_ref, inv_ref, cos_ref, sin_ref):
    half = MLA_ROPE // 2
    per_row = LANES // half
    tr = pos_ref.shape[0]
    ang = pos_ref[...].astype(F32) * inv_ref[...]
    lane = lax.broadcasted_iota(jnp.int32, ang.shape, 1)
    for tbl, ref in ((jnp.cos(ang), cos_ref), (jnp.sin(ang), sin_ref)):
        for j in range(per_row):
            rep = jnp.where((lane >= half * j) & (lane < half * (j + 1)), tbl, 0.0)
            width = half
            while width < LANES:
                rep = rep + pltpu.roll(rep, width, axis=1)
                width *= 2
            ref[pl.ds(j, tr, stride=per_row), :] = rep


def _rope_tables(positions):
    n = positions.size
    half = MLA_ROPE // 2
    per_row = LANES // half
    inv_freq = ROPE_BASE ** (-jnp.arange(0, MLA_ROPE, 2, dtype=F32) / MLA_ROPE)
    inv_row = jnp.tile(inv_freq, per_row).reshape(1, LANES)
    pos_rep = jnp.repeat(positions.reshape(n // per_row, per_row), half, axis=1)
    rows = n // per_row
    tr = min(rows, 512)
    out_spec = pl.BlockSpec((tr * per_row, LANES), lambda i: (i, 0))
    return pl.pallas_call(
        _rope_kernel,
        grid=(rows // tr,),
        in_specs=[pl.BlockSpec((tr, LANES), lambda i: (i, 0)), _const_spec((1, LANES))],
        out_specs=[out_spec, out_spec],
        out_shape=[jax.ShapeDtypeStruct((n, LANES), F32)] * 2,
        compiler_params=_cparams(1),
        name="rope_tables",
    )(pos_rep, inv_row)


def _memkv_kernel(mem_ref, g_ref, w_ref, o_ref, wb_scr):
    @pl.when(pl.program_id(1) == 0)
    def _():
        wb_scr[...] = w_ref[...].astype(BF16)

    mn = _rms(mem_ref[...], g_ref[...]).astype(BF16)
    o_ref[...] = _dot(mn, wb_scr[...]).astype(BF16)


def _mem_kv(mem2d, mem_norm, wkv):
    depth, d, d2 = wkv.shape
    rows = mem2d.shape[0]
    tr = min(rows, 512)
    return pl.pallas_call(
        _memkv_kernel,
        grid=(depth, rows // tr),
        in_specs=[pl.BlockSpec((tr, d), lambda l, i: (i, 0)),
                  _const_spec((1, d)),
                  pl.BlockSpec((None, d, d2), lambda l, i: (l, 0, 0))],
        out_specs=pl.BlockSpec((None, tr, d2), lambda l, i: (l, i, 0)),
        out_shape=jax.ShapeDtypeStruct((depth, rows, d2), BF16),
        scratch_shapes=[pltpu.VMEM((d, d2), BF16)],
        compiler_params=_cparams(2),
        name="mem_kv",
    )(mem2d, mem_norm.reshape(1, d), wkv)


def _rope_tile(x, cos, sin, lane):
    half = MLA_ROPE // 2
    up = pltpu.roll(x, LANES - half, axis=1)
    dn = pltpu.roll(x, half, axis=1)
    rot = jnp.where(lane < half, -up, dn)
    return jnp.where(lane < MLA_ROPE, x * cos + rot * sin, 0.0)


ROW_PHASES = 8


def _causal_conv_phases(x, xslab, sl, cw_ref, ls, bias=None):
    assert cw_ref.shape[0] == CONV_W and CONV_W - 1 <= CARRY_ROWS
    tm = x.shape[0]
    grp = tm // ROW_PHASES
    xslab[sl, CARRY_ROWS:CARRY_ROWS + tm, :] = x
    ph = [xslab[sl, pl.ds(CARRY_ROWS + j, grp, stride=ROW_PHASES), :]
          for j in range(-(CONV_W - 1), ROW_PHASES)]
    xslab[sl, 0:CARRY_ROWS, :] = x[tm - CARRY_ROWS:, :]
    w = [cw_ref[i:i + 1, ls] for i in range(CONV_W)]
    out = []
    for j in range(ROW_PHASES):
        acc = w[CONV_W - 1] * ph[j + CONV_W - 1]
        if bias is not None:
            acc = acc + bias
        for i in range(CONV_W - 1):
            acc = acc + w[i] * ph[j + i]
        out.append(acc)
    return out


def _even_in_kernel(x_ref, g_ref, win_ref, cw_ref, alog_ref, dtb_ref, qn_ref, kvn_ref,
                    wuq_ref, wukv_ref, cos_ref, sin_ref,
                    q_out, k_out, v_out, z_out, gb_out, qf_out, kf_out, vb_out,
                    xslab, yslab):
    tm = x_ref.shape[0]
    qk_w = GDN_HEADS * GDN_DK
    qkv_w = 3 * qk_w

    @pl.when(pl.program_id(1) == 0)
    def _():
        xslab[:, 0:CARRY_ROWS, :] = jnp.zeros((qkv_w // LANES, CARRY_ROWS, LANES), F32)

    h = _rms(x_ref[...], g_ref[...]).astype(BF16)
    lane = lax.broadcasted_iota(jnp.int32, (tm, LANES), 1)
    cos = cos_ref[...]
    sin = sin_ref[...]
    nope_w = MLA_HEADS * MLA_NOPE
    z_off = qkv_w
    cq_off = z_off + qk_w
    ckv_off = cq_off + qn_ref.shape[1]
    misc_off = ckv_off + kvn_ref.shape[1]
    gw = 2 * GDN_DK
    st = {}

    def proj_cols(lo, width):
        return _dot(h, win_ref[:, lo:lo + width])

    def mm_cq():
        st["cq"] = proj_cols(cq_off, qn_ref.shape[1])

    def ep_cq():
        st["cqb"] = _rms(st.pop("cq"), qn_ref[...] * (MLA_SCALE * LOG2E)).astype(BF16)

    def mm_ckv():
        st["ckv"] = proj_cols(ckv_off, kvn_ref.shape[1])

    def ep_ckv():
        st["ckvb"] = _rms(st.pop("ckv"), kvn_ref[...]).astype(BF16)

    def mm_misc():
        st["misc"] = proj_cols(misc_off, LANES)

    def ep_misc():
        misc = st.pop("misc")
        gdec = -jnp.exp(alog_ref[...]) * _softplus(misc + dtb_ref[...])
        beta = _sigmoid(misc)
        gb_out[...] = jnp.where((lane >= G_LANE) & (lane < G_LANE + GDN_HEADS), gdec,
                                jnp.where((lane >= B_LANE) & (lane < B_LANE + GDN_HEADS),
                                          beta, 0.0))
        st["kpe"] = _rope_tile(misc, cos, sin, lane).astype(BF16)

    def mm_uq():
        st["qf"] = _dot(st.pop("cqb"), wuq_ref[...])

    def ep_uq():
        qf = st.pop("qf")
        for hh in range(MLA_HEADS):
            base = 2 * LANES * hh
            qf_out[:, base:base + LANES] = qf[:, hh * LANES:(hh + 1) * LANES].astype(BF16)
            pe = qf[:, nope_w + hh * LANES: nope_w + (hh + 1) * LANES]
            qf_out[:, base + LANES:base + 2 * LANES] = _rope_tile(pe, cos, sin, lane).astype(BF16)

    def mm_ukv():
        st["kv"] = _dot(st.pop("ckvb"), wukv_ref[...])

    def ep_ukv():
        kv = st.pop("kv")
        for hh in range(MLA_HEADS):
            base = 2 * LANES * hh
            kf_out[:, base:base + LANES] = kv[:, hh * LANES:(hh + 1) * LANES].astype(BF16)
            kf_out[:, base + LANES:base + 2 * LANES] = st["kpe"]
        vb_out[...] = kv[:, nope_w:].astype(BF16)

    def mm_group(lo):
        def run():
            st[lo] = proj_cols(lo, gw)
        return run

    def ep_qkv(lo):
        def run():
            raw = st.pop(lo)
            which, off = divmod(lo, qk_w)
            out, scale = ((q_out, GDN_DK ** -0.5), (k_out, 1.0), (v_out, None))[which]
            grp = tm // ROW_PHASES
            for sub in range(gw // LANES):
                sl = lo // LANES + sub
                ls = slice(lo + sub * LANES, lo + (sub + 1) * LANES)
                blocks = _causal_conv_phases(raw[:, sub * LANES:(sub + 1) * LANES], xslab, sl,
                                             cw_ref, ls)
                for j, acc in enumerate(blocks):
                    a = acc * _sigmoid(acc)
                    if scale is not None:
                        a = a * (lax.rsqrt(jnp.sum(a * a, axis=-1, keepdims=True) + EPS) * scale)
                    yslab[sl, pl.ds(j, grp, stride=ROW_PHASES), :] = a
                out[:, off + sub * LANES: off + (sub + 1) * LANES] = yslab[sl].astype(BF16)
        return run

    def ep_z(lo):
        def run():
            z_out[:, lo - z_off: lo - z_off + gw] = st.pop(lo).astype(BF16)
        return run

    pairs = [(mm_group(lo), ep_qkv(lo)) for lo in range(0, qkv_w, gw)]
    pairs += [(mm_cq, ep_cq), (mm_ckv, ep_ckv), (mm_misc, ep_misc), (mm_uq, ep_uq), (mm_ukv, ep_ukv)]
    pairs += [(mm_group(lo), ep_z(lo)) for lo in range(z_off, cq_off, gw)]
    for mm, ep in pairs:
        mm()
        ep()


def _even_inproj(xf, g0, w_in_re, conv_w, alog_row, dtb_row, q_norm, kv_norm, wuq_re, wukv_re,
                 cos_t, sin_t, batch, tm):
    n, d = xf.shape
    t = n // batch
    nt = t // tm
    e_in = w_in_re.shape[1]
    qkv_w = conv_w.shape[1]
    qk_w = qkv_w // 3
    row = lambda w: pl.BlockSpec((tm, w), lambda b, i: (b * nt + i, 0))
    outs = [(qk_w, BF16)] * 4 + [(LANES, F32), (2 * LANES * MLA_HEADS, BF16),
                                 (2 * LANES * MLA_HEADS, BF16), (LANES * MLA_HEADS, BF16)]
    return pl.pallas_call(
        _even_in_kernel,
        grid=(batch, nt),
        in_specs=[row(d), _const_spec((1, d)), _const_spec((d, e_in)), _const_spec((CONV_W, qkv_w)),
                  _const_spec((1, LANES)), _const_spec((1, LANES)),
                  _const_spec(q_norm.shape), _const_spec(kv_norm.shape),
                  _const_spec(wuq_re.shape), _const_spec(wukv_re.shape),
                  row(LANES), row(LANES)],
        out_specs=[row(w) for w, _ in outs],
        out_shape=[jax.ShapeDtypeStruct((n, w), dt) for w, dt in outs],
        scratch_shapes=[pltpu.VMEM((qkv_w // LANES, tm + CARRY_ROWS, LANES), F32),
                        pltpu.VMEM((qkv_w // LANES, tm, LANES), F32)],
        compiler_params=_cparams(2),
        name="even_inproj",
    )(xf, g0, w_in_re, conv_w, alog_row, dtb_row, q_norm, kv_norm, wuq_re, wukv_re, cos_t, sin_t)


SUPER = 2 * CHUNK
G_LANE = MLA_ROPE
B_LANE = MLA_ROPE + GDN_HEADS
GDN_PIPE = 2


def _gdn_kernel(q_ref, k_ref, v_ref, z_ref, gb_ref, on_ref, out_ref,
                gc_scr, gct_scr, gl_scr, u_scr, w_scr, a_scr, qe_scr, kd_scr, s_scr):
    t = q_ref.shape[0]
    n_super = t // SUPER
    n_chunk = t // CHUNK

    gb = gb_ref[...]
    rowi = lax.broadcasted_iota(jnp.int32, gb.shape, 0) & (CHUNK - 1)
    gc = gb
    s = 1
    while s < CHUNK:
        gc = jnp.where(rowi >= s, gc + pltpu.roll(gc, s, axis=0), gc)
        s *= 2
    gc_scr[...] = gc
    gct_scr[...] = gc.T
    g3 = gc.reshape(n_chunk, CHUNK, LANES)
    gl_scr[...] = jnp.broadcast_to(g3[:, CHUNK - 1:CHUNK, :], g3.shape).reshape(t, LANES)

    ri = lax.broadcasted_iota(jnp.int32, (SUPER, SUPER), 0)
    ci = lax.broadcasted_iota(jnp.int32, (SUPER, SUPER), 1)
    same = (ri >= CHUNK) == (ci >= CHUNK)
    causal = same & (ri >= ci)
    strict = same & (ri > ci)
    eye = (ri == ci).astype(F32)

    on = on_ref[...]
    heads = range(GDN_HEADS)
    hsl = [slice(hh * GDN_DK, (hh + 1) * GDN_DK) for hh in heads]
    lane_g = [slice(G_LANE + hh, G_LANE + hh + 1) for hh in heads]
    lane_b = [slice(B_LANE + hh, B_LANE + hh + 1) for hh in heads]

    def solve_stages(sc0):
        probs = []
        for sub in range(GDN_PIPE):
            rows = pl.ds(pl.multiple_of((sc0 + sub) * SUPER, SUPER), SUPER)
            probs += [(rows, hh) for hh in heads]
        kc = [k_ref[rows, hsl[hh]] for rows, hh in probs]
        qc = [q_ref[rows, hsl[hh]] for rows, hh in probs]
        gcol = [gc_scr[rows, lane_g[hh]] for rows, hh in probs]
        bcol = [gb_ref[rows, lane_b[hh]] for rows, hh in probs]
        grow = [gct_scr[lane_g[hh], rows] for rows, hh in probs]
        kk = [_dot_nt(k, k) for k in kc]
        qk = [_dot_nt(q, k) for q, k in zip(qc, kc)]
        yield
        decay = [jnp.where(causal, jnp.exp(jnp.where(causal, gc - gr, 0.0)), 0.0)
                 for gc, gr in zip(gcol, grow)]
        m = [jnp.where(strict, -(x * b * dc), 0.0) for x, b, dc in zip(kk, bcol, decay)]
        qacc = [eye + x for x in m]
        mb = [x.astype(BF16) for x in m]
        mj = [_dot(x, x) for x in mb]
        yield
        lvl = 2
        while lvl < CHUNK // 2:
            mb = [x.astype(BF16) for x in mj]
            r = [_dot(x, jnp.concatenate([x, qa.astype(BF16)], axis=1)) for x, qa in zip(mb, qacc)]
            yield
            mj = [x[:, :SUPER] for x in r]
            qacc = [qa + x[:, SUPER:] for qa, x in zip(qacc, r)]
            lvl *= 2
        last = [_dot(x.astype(BF16), qa.astype(BF16)) for qa, x in zip(qacc, mj)]
        yield
        qacc = [qa + x for qa, x in zip(qacc, last)]
        eg = [jnp.exp(gc) for gc in gcol]
        sol = []
        for i, (rows, hh) in enumerate(probs):
            vc = v_ref[rows, hsl[hh]].astype(F32)
            kf = kc[i].astype(F32)
            rhs = jnp.concatenate([vc * bcol[i], kf * (bcol[i] * eg[i])], axis=1).astype(BF16)
            sol.append(_dot(qacc[i].astype(BF16), rhs))
        yield
        for i, (rows, hh) in enumerate(probs):
            u_scr[rows, hsl[hh]] = sol[i][:, :GDN_DK]
            w_scr[rows, hsl[hh]] = sol[i][:, GDN_DK:].astype(BF16)
            aqk = qk[i] * decay[i]
            a_sh = pltpu.roll(aqk, CHUNK, axis=1)
            a_scr[rows, hsl[hh]] = jnp.where(ri < CHUNK, aqk, a_sh).astype(BF16)
            qe_scr[rows, hsl[hh]] = (qc[i].astype(F32) * eg[i]).astype(BF16)
            glast = gl_scr[rows, lane_g[hh]]
            kd_scr[rows, hsl[hh]] = (kc[i].astype(F32) * jnp.exp(glast - gcol[i])).astype(BF16)

    def scan_stages(sc0, st):
        for half in range(GDN_PIPE * SUPER // CHUNK):
            r0 = pl.multiple_of(sc0 * SUPER + half * CHUNK, CHUNK)
            rows = pl.ds(r0, CHUNK)
            sb = [x.astype(BF16) for x in st]
            t1 = [_dot(w_scr[rows, hsl[hh]], sb[hh]) for hh in heads]
            oq = [_dot(qe_scr[rows, hsl[hh]], sb[hh]) for hh in heads]
            yield
            vb = [(u_scr[rows, hsl[hh]] - t1[hh]).astype(BF16) for hh in heads]
            oa = [_dot(a_scr[rows, hsl[hh]][:, :CHUNK], vb[hh]) for hh in heads]
            kv = [lax.dot_general(kd_scr[rows, hsl[hh]], vb[hh], TN_DIMS,
                                  preferred_element_type=F32) for hh in heads]
            yield
            for hh in heads:
                gam = jnp.exp(gl_scr[pl.ds(r0, 1), lane_g[hh]])
                st[hh] = st[hh] * gam + kv[hh]
                z = z_ref[rows, hsl[hh]].astype(F32)
                out_ref[rows, hsl[hh]] = (
                    _rms(oq[hh] + oa[hh], on) * (z * _sigmoid(z))).astype(BF16)

    _round_robin([solve_stages(0)])
    s_scr[...] = jnp.zeros(s_scr.shape, F32)
    n_steps = n_super // GDN_PIPE

    def pipe_body(it, _):
        st = [s_scr[hh] for hh in heads]
        _round_robin([solve_stages((it + 1) * GDN_PIPE), scan_stages(it * GDN_PIPE, st)])
        for hh in heads:
            s_scr[hh] = st[hh]
        return 0

    lax.fori_loop(0, n_steps - 1, pipe_body, 0)
    st = [s_scr[hh] for hh in heads]
    _round_robin([scan_stages((n_steps - 1) * GDN_PIPE, st)])


def _gdn(q, k, v, z, gb, o_norm, batch):
    n, w = q.shape
    t = n // batch
    row = lambda ww: pl.BlockSpec((t, ww), lambda b: (b, 0))
    return pl.pallas_call(
        _gdn_kernel,
        grid=(batch,),
        in_specs=[row(w), row(w), row(w), row(w), row(LANES), _const_spec((1, GDN_DK))],
        out_specs=row(w),
        out_shape=jax.ShapeDtypeStruct((n, w), BF16),
        scratch_shapes=[pltpu.VMEM((t, LANES), F32),
                        pltpu.VMEM((LANES, t), F32),
                        pltpu.VMEM((t, LANES), F32),
                        pltpu.VMEM((t, w), F32),
                        pltpu.VMEM((t, w), BF16),
                        pltpu.VMEM((t, w), BF16),
                        pltpu.VMEM((t, w), BF16),
                        pltpu.VMEM((t, w), BF16),
                        pltpu.VMEM((GDN_HEADS, GDN_DK, GDN_DK), F32)],
        compiler_params=_cparams(1),
        name="gdn",
    )(q, k, v, z, gb, o_norm)


MLA_TQ = 256


def _mla_kernel(q_ref, k_ref, v_ref, *rest):
    n_cast = (len(rest) - 1) // 2
    o_ref = rest[n_cast]
    for src, dst in zip(rest[:n_cast], rest[n_cast + 1:]):
        dst[...] = src[...].astype(BF16)
    t = q_ref.shape[0]
    tq = MLA_TQ
    nq = t // tq
    hw = 2 * LANES
    heads = range(q_ref.shape[1] // hw)
    ri = lax.broadcasted_iota(jnp.int32, (tq, tq), 0) // CHUNK
    ci = lax.broadcasted_iota(jnp.int32, (tq, tq), 1) // CHUNK
    diag_mask = ci <= ri

    groups = [(i, h) for i in range(nq) for h in heads]

    def scores(i, h):
        kw = (i + 1) * tq
        return _dot_nt(q_ref[i * tq:(i + 1) * tq, h * hw:(h + 1) * hw],
                       k_ref[0:kw, h * hw:(h + 1) * hw])

    s_next = scores(*groups[0])
    for g, (i, h) in enumerate(groups):
        s = s_next
        if g + 1 < len(groups):
            s_next = scores(*groups[g + 1])
        kw = (i + 1) * tq
        s_diag = jnp.where(diag_mask, s[:, kw - tq:], -jnp.inf)
        s = s_diag if i == 0 else jnp.concatenate([s[:, :kw - tq], s_diag], axis=1)
        m = jnp.max(s, axis=-1, keepdims=True)
        p = jnp.exp2(s - m)
        l = jnp.sum(p, axis=-1, keepdims=True)
        pv = _dot(p.astype(BF16), v_ref[0:kw, h * LANES:(h + 1) * LANES])
        o_ref[i * tq:(i + 1) * tq, h * LANES:(h + 1) * LANES] = (pv / l).astype(BF16)


MLA_HEADS_PER_STEP = 2


def _mla(qf, kf, vb, batch, cast_weights=()):
    n = qf.shape[0]
    t = n // batch
    hps = MLA_HEADS_PER_STEP
    hg = MLA_HEADS // hps
    steps = batch * hg
    spec = lambda w: pl.BlockSpec((t, w * hps), lambda b, g: (b, g))
    cast_specs = [pl.BlockSpec((w.shape[0], w.shape[1] // steps, w.shape[2]),
                               lambda b, g: (0, b * hg + g, 0)) for w in cast_weights]
    outs = pl.pallas_call(
        _mla_kernel,
        grid=(batch, hg),
        in_specs=[spec(2 * LANES), spec(2 * LANES), spec(LANES)] + cast_specs,
        out_specs=[spec(LANES)] + cast_specs,
        out_shape=[jax.ShapeDtypeStruct((n, vb.shape[1]), BF16)]
                  + [jax.ShapeDtypeStruct(w.shape, BF16) for w in cast_weights],
        compiler_params=_cparams(2),
        name="mla_attn",
    )(qf, kf, vb, *cast_weights)
    return outs[0], outs[1:]


SCAN_PHASES = ROW_PHASES


def _gelu_tanh(x):
    c2 = 2.0 * math.sqrt(2.0 / math.pi) * LOG2E
    return x / (1.0 + jnp.exp2(x * (-c2 - (c2 * 0.044715) * (x * x))))


def _odd_kernel(x_ref, g0_ref, win_f32, cw_ref, cb_ref, gaw_f32, gab_ref, gxw_f32, gxb_ref,
                ap_ref, wout_f32, g1_ref, o_ref, xslab, hslab, hcar,
                win_ref, gaw_ref, gxw_ref, wout_ref):
    tm, d = x_ref.shape
    bw = d // LRU_BLOCKS
    n_slab = d // LANES
    assert cw_ref.shape[0] == CONV_W and CONV_W - 1 <= CARRY_ROWS
    _cast_once([(win_f32, win_ref), (gaw_f32, gaw_ref), (gxw_f32, gxw_ref), (wout_f32, wout_ref)])

    @pl.when(pl.program_id(1) == 0)
    def _():
        xslab[:, 0:CARRY_ROWS, :] = jnp.zeros((n_slab, CARRY_ROWS, LANES), F32)
        hcar[...] = jnp.zeros(hcar.shape, F32)

    x = x_ref[...]
    h = _rms(x, g0_ref[...]).astype(BF16)
    grp = tm // SCAN_PHASES
    slabs_per_block = bw // LANES
    rowg = lax.broadcasted_iota(jnp.int32, (grp, 1), 0)

    blocks = range(LRU_BLOCKS)
    csl = [slice(n * bw, (n + 1) * bw) for n in blocks]
    st = [dict() for _ in blocks]

    def projx(n):
        st[n]["xb"] = _dot(h, win_ref[:, csl[n]])

    def projy(n):
        st[n]["yb"] = _dot(h, win_ref[:, d + n * bw: d + (n + 1) * bw])

    def conv(n):
        xb = st[n].pop("xb")
        cols = []
        for k in range(slabs_per_block):
            sl = n * slabs_per_block + k
            ls = slice(n * bw + k * LANES, n * bw + (k + 1) * LANES)
            out = _causal_conv_phases(xb[:, k * LANES:(k + 1) * LANES], xslab, sl, cw_ref, ls,
                                      bias=cb_ref[:, ls])
            cols.append(jnp.concatenate(out, axis=0))
        st[n]["xc"] = jnp.concatenate(cols, axis=1)

    def gates(n):
        xcb = st[n]["xc"].astype(BF16)
        st[n]["ra"] = _dot(xcb, gaw_ref[n])
        st[n]["ia"] = _dot(xcb, gxw_ref[n])

    def recur(n):
        cs, xc = csl[n], st[n].pop("xc")
        r = _sigmoid(st[n].pop("ra") + gab_ref[:, cs])
        ig = _sigmoid(st[n].pop("ia") + gxb_ref[:, cs])
        a = jnp.exp2(r * ((-LRU_C * LOG2E) * _softplus(-ap_ref[:, cs])))
        om = 1.0 - a * a
        u = jnp.where(om > 0.0, om * lax.rsqrt(om), 0.0) * (ig * xc)

        ca, cu = a[0:grp], u[0:grp]
        la, lu = [ca], [cu]
        for j in range(1, SCAN_PHASES):
            aj, uj = a[j * grp:(j + 1) * grp], u[j * grp:(j + 1) * grp]
            cu = aj * cu + uj
            ca = aj * ca
            la.append(ca)
            lu.append(cu)
        ga, gu = ca, cu
        s = 1
        while s < grp:
            valid = rowg >= s
            gu = jnp.where(valid, ga * pltpu.roll(gu, s, axis=0) + gu, gu)
            ga = jnp.where(valid, ga * pltpu.roll(ga, s, axis=0), ga)
            s *= 2
        h_in = hcar[:, cs]
        h_end = ga * h_in + gu
        hcar[:, cs] = h_end[grp - 1:grp, :]
        h_prev = jnp.where(rowg == 0, h_in, pltpu.roll(h_end, 1, axis=0))
        slabs = range(n * slabs_per_block, (n + 1) * slabs_per_block)
        for j in range(SCAN_PHASES):
            hj = la[j] * h_prev + lu[j]
            for k, sl in enumerate(slabs):
                hslab[sl, pl.ds(j, grp, stride=SCAN_PHASES), :] = hj[:, k * LANES:(k + 1) * LANES]
        st[n]["hs"] = jnp.concatenate([hslab[sl] for sl in slabs], axis=1)

    def gate(n):
        st[n]["hg"] = (st[n].pop("hs") * _gelu_tanh(st[n].pop("yb"))).astype(BF16)

    def outp(n):
        st[n]["y"] = _dot(st[n].pop("hg"), wout_ref[csl[n], :])

    chain = (projx, conv, gates, recur, projy, gate, outp)
    for k in range(LRU_BLOCKS + len(chain) - 1):
        for s, stage in enumerate(chain):
            if 0 <= k - s < LRU_BLOCKS:
                stage(k - s)
    y = functools.reduce(lambda p, q: p + q, [st[n]["y"] for n in blocks])
    o_ref[...] = x + _rms(y, g1_ref[...])


def _odd_mixer(xf, g0, g1, w_in, conv_w, conv_b, gaw, gab, gxw, gxb, a_param, w_out, o, batch, tm):
    n, d = xf.shape
    t = n // batch
    nt = t // tm
    row = pl.BlockSpec((tm, d), lambda b, i: (b * nt + i, 0))
    vec = _const_spec((1, d))
    weights = (w_in, gaw, gxw, w_out)
    wspec = [_layer_spec(w.shape, o) for w in weights]
    return pl.pallas_call(
        _odd_kernel,
        grid=(batch, nt),
        in_specs=[row, vec, wspec[0], _const_spec(conv_w.shape), vec,
                  wspec[1], vec, wspec[2], vec, vec, wspec[3], vec],
        out_specs=row,
        out_shape=jax.ShapeDtypeStruct((n, d), F32),
        scratch_shapes=[pltpu.VMEM((d // LANES, tm + CARRY_ROWS, LANES), F32),
                        pltpu.VMEM((d // LANES, tm, LANES), F32),
                        pltpu.VMEM((1, d), F32)]
                       + [pltpu.VMEM(w.shape[1:], BF16) for w in weights],
        compiler_params=_cparams(2),
        name="odd_mixer",
    )(xf, g0, w_in, conv_w, conv_b, gaw, gab, gxw, gxb, a_param, w_out, g1)


XA_SPLIT = 4


def _round_robin(gens):
    live = list(gens)
    while live:
        live = [g for g in live if next(g, StopIteration) is not StopIteration]


def _cast_once(pairs, n_axes=2):
    first = pl.program_id(0) == 0
    for axis in range(1, n_axes):
        first = first & (pl.program_id(axis) == 0)

    @pl.when(first)
    def _():
        for src, dst in pairs:
            dst[...] = src[...].astype(BF16)


def _xattn_kernel(x_ref, g2_ref, wq_f32, kv_ref, wo_f32, g3_ref, o_ref, wq_ref, wo_ref):
    _cast_once([(wq_f32, wq_ref), (wo_f32, wo_ref)])
    tm = x_ref.shape[0]
    rs = tm // XA_SPLIT
    _round_robin([_xattn_stages(slice(k * rs, (k + 1) * rs), x_ref, None, g2_ref, wq_ref, kv_ref,
                                wo_ref, g3_ref, o_ref) for k in range(XA_SPLIT)])


def _outproj_xattn_kernel(x_ref, a_ref, b_ref, w_f32, g1_ref,
                          g2_ref, wq_f32, kv_ref, wo_f32, g3_ref, o_ref, wq_ref, wo_ref, w_ref):
    _cast_once([(wq_f32, wq_ref), (wo_f32, wo_ref), (w_f32, w_ref)])
    tm = x_ref.shape[0]
    rs = tm // XA_SPLIT
    _round_robin([_xattn_stages(slice(k * rs, (k + 1) * rs), x_ref, (a_ref, b_ref, w_ref, g1_ref),
                                g2_ref, wq_ref, kv_ref, wo_ref, g3_ref, o_ref)
                  for k in range(XA_SPLIT)])


def _xattn_stages(rows, x_ref, mix, g2_ref, wq_ref, kv_ref, wo_ref, g3_ref, o_ref):
    x = x_ref[rows, :]
    d = x.shape[1]
    hd = d // XA_HEADS
    if mix is not None:
        a_ref, b_ref, w_ref, g1_ref = mix
        ka = a_ref.shape[1]
        y = _dot(a_ref[rows, :], w_ref[0:ka, :]) + _dot(b_ref[rows, :], w_ref[ka:, :])
        yield
        x = x + _rms(y, g1_ref[...])
    h = _rms(x, g2_ref[...]).astype(BF16)
    q = _dot(h, wq_ref[...])
    yield
    q = (q * (hd ** -0.5 * LOG2E)).astype(BF16)
    s = [_dot_nt(q[:, hh * hd:(hh + 1) * hd], kv_ref[:, hh * hd:(hh + 1) * hd])
         for hh in range(XA_HEADS)]
    yield
    outs = []
    for hh in range(XA_HEADS):
        m = jnp.max(s[hh], axis=-1, keepdims=True)
        p = jnp.exp2(s[hh] - m)
        l = jnp.sum(p, axis=-1, keepdims=True)
        pv = _dot(p.astype(BF16), kv_ref[:, d + hh * hd: d + (hh + 1) * hd])
        outs.append((pv / l).astype(BF16))
    yield
    y = _dot(jnp.concatenate(outs, axis=1), wo_ref[...])
    yield
    o_ref[rows, :] = x + _rms(y, g3_ref[...])


def _xattn(xf, g2, g3, wq, kv_mem, wo, layer, batch, tm, mix=None):
    n, d = xf.shape
    t = n // batch
    nt = t // tm
    n_mem = kv_mem.shape[1] // batch
    row = lambda w: pl.BlockSpec((tm, w), lambda b, i: (b * nt + i, 0))
    vec = _const_spec((1, d))
    xa_specs = [vec, _layer_spec(wq.shape, layer),
                pl.BlockSpec((None, n_mem, 2 * d), lambda b, i: (layer, b, 0)),
                _layer_spec(wo.shape, layer), vec]
    xa_args = (g2, wq, kv_mem, wo, g3)
    scratch = [pltpu.VMEM(wq.shape[1:], BF16), pltpu.VMEM(wo.shape[1:], BF16)]
    if mix is None:
        body, specs, args, name = _xattn_kernel, [row(d)] + xa_specs, (xf,) + xa_args, "xattn"
    else:
        a, b, w_out, w_out_layer, g1 = mix
        body, name = _outproj_xattn_kernel, "outproj_xattn"
        specs = [row(d), row(a.shape[1]), row(b.shape[1]), _layer_spec(w_out.shape, w_out_layer),
                 vec] + xa_specs
        args = (xf, a, b, w_out, g1) + xa_args
        scratch.append(pltpu.VMEM(w_out.shape[1:], BF16))
    return pl.pallas_call(
        body,
        grid=(batch, nt),
        in_specs=specs,
        out_specs=row(d),
        out_shape=jax.ShapeDtypeStruct((n, d), F32),
        scratch_shapes=scratch,
        compiler_params=_cparams(2),
        name=name,
    )(*args)


FFN_TF = 256


def _ffn_kernel(x_ref, g4_ref, win_ref, wo_ref, g5_ref, o_ref, act_scr):
    dff = wo_ref.shape[0]
    x = x_ref[...]
    h = _rms(x, g4_ref[...]).astype(BF16)
    for c in range(dff // FFN_TF):
        cols = slice(c * FFN_TF, (c + 1) * FFN_TF)
        gt = _dot(h, win_ref[:, cols])
        up = _dot(h, win_ref[:, dff + c * FFN_TF: dff + (c + 1) * FFN_TF])
        act_scr[:, cols] = (gt * _sigmoid(gt) * up).astype(BF16)
    y = _dot(act_scr[...], wo_ref[...])
    o_ref[...] = x + _rms(y, g5_ref[...])


def _ffn(xf, g4, g5, w_in, w_out, layer, tm):
    n, d = xf.shape
    dff = w_out.shape[1]
    row = pl.BlockSpec((tm, d), lambda i: (i, 0))
    vec = _const_spec((1, d))
    return pl.pallas_call(
        _ffn_kernel,
        grid=(n // tm,),
        in_specs=[row, vec, _layer_spec(w_in.shape, layer), _layer_spec(w_out.shape, layer), vec],
        out_specs=row,
        out_shape=jax.ShapeDtypeStruct((n, d), F32),
        scratch_shapes=[pltpu.VMEM((tm, dff), BF16)],
        compiler_params=_cparams(1),
        name="ffn",
    )(xf, g4, w_in, w_out, g5)


def _relayout_even(w_in, w_uq, w_ukv, a_log, dt_bias):
    d = w_in.shape[0]
    qkvz = 3 * GDN_HEADS * GDN_DK + GDN_HEADS * GDN_DK
    ab = 2 * GDN_HEADS
    lat = w_in.shape[1] - qkvz - ab - MLA_ROPE
    pad = LANES - MLA_ROPE - ab
    w_in = w_in.astype(BF16)
    w_in_re = jnp.concatenate(
        [w_in[:, :qkvz], w_in[:, qkvz + ab:qkvz + ab + lat], w_in[:, qkvz + ab + lat:],
         w_in[:, qkvz:qkvz + ab], jnp.zeros((d, pad), w_in.dtype)], axis=1)
    r = w_uq.shape[0]
    uq = w_uq.reshape(r, MLA_HEADS, MLA_NOPE + MLA_ROPE)
    pe = jnp.pad(uq[:, :, MLA_NOPE:], ((0, 0), (0, 0), (0, LANES - MLA_ROPE)))
    wuq_re = jnp.concatenate([uq[:, :, :MLA_NOPE].reshape(r, -1), pe.reshape(r, -1)],
                             axis=1).astype(BF16)
    ukv = w_ukv.reshape(w_ukv.shape[0], MLA_HEADS, -1)
    wukv_re = jnp.concatenate([ukv[:, :, :MLA_NOPE].reshape(r, -1),
                               ukv[:, :, MLA_NOPE:].reshape(r, -1)], axis=1).astype(BF16)
    alog_row = jnp.zeros((1, LANES), F32).at[0, G_LANE:G_LANE + GDN_HEADS].set(a_log)
    dtb_row = jnp.zeros((1, LANES), F32).at[0, G_LANE:G_LANE + GDN_HEADS].set(dt_bias)
    return w_in_re, wuq_re, wukv_re, alog_row, dtb_row


def kernel(x, mem, positions, norm_gains, mem_norm, e_w_in, e_conv_w, e_a_log, e_dt_bias, e_o_norm,
           e_q_norm, e_kv_norm, e_w_uq, e_w_ukv, e_w_out, o_w_in, o_conv_w, o_conv_b, o_gate_a_w,
           o_gate_a_b, o_gate_x_w, o_gate_x_b, o_a_param, o_w_out, xa_wq, xa_wkv, xa_wo, ffn_w_in,
           ffn_w_out):
    batch, t, d = x.shape
    n = batch * t
    depth = norm_gains.shape[0]
    xf = x.reshape(n, d)
    gains = norm_gains.reshape(depth, -1, 1, d)

    cos_t, sin_t = _rope_tables(positions)
    kv_mem = _mem_kv(mem.reshape(-1, d), mem_norm, xa_wkv)
    ffn_w_b = None

    for layer in range(depth):
        g = gains[layer]
        mix = None
        if layer % 2 == 0:
            e = layer // 2
            w_in_re, wuq_re, wukv_re, alog_row, dtb_row = _relayout_even(
                e_w_in[e], e_w_uq[e], e_w_ukv[e], e_a_log[e], e_dt_bias[e])
            q, k, v, z, gb, qf, kf, vb = _even_inproj(
                xf, g[0], w_in_re, e_conv_w[e], alog_row, dtb_row,
                e_q_norm[e].reshape(1, -1), e_kv_norm[e].reshape(1, -1), wuq_re, wukv_re,
                cos_t, sin_t, batch, tm=ROW_TILE["even_inproj"])
            out_a = _gdn(q, k, v, z, gb, e_o_norm[e].reshape(1, -1), batch)
            if ffn_w_b is None:
                out_b, ffn_w_b = _mla(qf, kf, vb, batch, cast_weights=(ffn_w_in, ffn_w_out))
            else:
                out_b, _ = _mla(qf, kf, vb, batch)
            mix = (out_a, out_b, e_w_out, e, g[1])
        else:
            o = layer // 2
            xf = _odd_mixer(xf, g[0], g[1], o_w_in, o_conv_w[o],
                            o_conv_b[o].reshape(1, d), o_gate_a_w,
                            o_gate_a_b[o].reshape(1, d), o_gate_x_w,
                            o_gate_x_b[o].reshape(1, d), o_a_param[o].reshape(1, d),
                            o_w_out, o, batch, tm=ROW_TILE["odd_mixer"])
        xf = _xattn(xf, g[2], g[3], xa_wq, kv_mem, xa_wo, layer, batch, tm=ROW_TILE["xattn"],
                    mix=mix)
        xf = _ffn(xf, g[4], g[5], ffn_w_b[0], ffn_w_b[1], layer, tm=ROW_TILE["ffn"])
    return xf.reshape(batch, t, d)
```

```python
import functools
import math

import jax
import jax.numpy as jnp
from jax import lax
from jax.experimental import pallas as pl
from jax.experimental.pallas import tpu as pltpu

F32 = jnp.float32
BF16 = jnp.bfloat16

EPS = 1e-6
CHUNK = 64
GDN_HEADS = 4
GDN_DK = 128
MLA_HEADS = 4
MLA_NOPE = 128
MLA_ROPE = 64
MLA_SCALE = (MLA_NOPE + MLA_ROPE) ** -0.5
LOG2E = math.log2(math.e)
ROPE_BASE = 10000.0
LRU_BLOCKS = 4
LRU_C = 8.0
XA_HEADS = 4
CONV_W = 4

LANES = 128
CARRY_ROWS = 8
V7X_VMEM_LIMIT = 56 * 1024 * 1024

ROW_TILE = {"even_inproj": 512, "odd_mixer": 512, "xattn": 1024, "ffn": 1024}

NT_DIMS = (((1,), (1,)), ((), ()))
TN_DIMS = (((0,), (0,)), ((), ()))


def _cparams(n_axes):
    return pltpu.CompilerParams(
        dimension_semantics=("arbitrary",) * n_axes,
        vmem_limit_bytes=V7X_VMEM_LIMIT)


def _dot(a, b):
    return jnp.dot(a, b, preferred_element_type=F32)


def _dot_nt(a, b):
    return lax.dot_general(a, b, NT_DIMS, preferred_element_type=F32)


def _rms(xf, g):
    ms = jnp.mean(xf * xf, axis=-1, keepdims=True)
    return xf * lax.rsqrt(ms + EPS) * g


def _sigmoid(x):
    return 1.0 / (1.0 + jnp.exp(-x))


def _softplus(x):
    return jnp.maximum(x, 0.0) + jnp.log1p(jnp.exp(-jnp.abs(x)))


def _const_spec(shape):
    nd = len(shape)
    return pl.BlockSpec(shape, lambda *_: (0,) * nd)


def _layer_spec(shape, layer):
    nd = len(shape) - 1
    return pl.BlockSpec((None,) + tuple(shape[1:]), lambda *_: (layer,) + (0,) * nd,
                        pipeline_mode=pl.Buffered(1))


def _rope_kernel(pos_ref, inv_ref, cos_ref, sin_ref):
    half = MLA_ROPE // 2
    per_row = LANES // half
    tr = pos_ref.shape[0]
    ang = pos_ref[...].astype(F32) * inv_ref[...]
    lane = lax.broadcasted_iota(jnp.int32, ang.shape, 1)
    for tbl, ref in ((jnp.cos(ang), cos_ref), (jnp.sin(ang), sin_ref)):
        for j in range(per_row):
            rep = jnp.where((lane >= half * j) & (lane < half * (j + 1)), tbl, 0.0)
            width = half
            while width < LANES:
                rep = rep + pltpu.roll(rep, width, axis=1)
                width *= 2
            ref[pl.ds(j, tr, stride=per_row), :] = rep


def _rope_tables(positions):
    n = positions.size
    half = MLA_ROPE // 2
    per_row = LANES // half
    inv_freq = ROPE_BASE ** (-jnp.arange(0, MLA_ROPE, 2, dtype=F32) / MLA_ROPE)
    inv_row = jnp.tile(inv_freq, per_row).reshape(1, LANES)
    pos_rep = jnp.repeat(positions.reshape(n // per_row, per_row), half, axis=1)
    rows = n // per_row
    tr = min(rows, 512)
    out_spec = pl.BlockSpec((tr * per_row, LANES), lambda i: (i, 0))
    return pl.pallas_call(
        _rope_kernel,
        grid=(rows // tr,),
        in_specs=[pl.BlockSpec((tr, LANES), lambda i: (i, 0)), _const_spec((1, LANES))],
        out_specs=[out_spec, out_spec],
        out_shape=[jax.ShapeDtypeStruct((n, LANES), F32)] * 2,
        compiler_params=_cparams(1),
        name="rope_tables",
    )(pos_rep, inv_row)


def _memkv_kernel(mem_ref, g_ref, w_ref, o_ref, wb_scr):
    @pl.when(pl.program_id(1) == 0)
    def _():
        wb_scr[...] = w_ref[...].astype(BF16)

    mn = _rms(mem_ref[...], g_ref[...]).astype(BF16)
    o_ref[...] = _dot(mn, wb_scr[...]).astype(BF16)


def _mem_kv(mem2d, mem_norm, wkv):
    depth, d, d2 = wkv.shape
    rows = mem2d.shape[0]
    tr = min(rows, 512)
    return pl.pallas_call(
        _memkv_kernel,
        grid=(depth, rows // tr),
        in_specs=[pl.BlockSpec((tr, d), lambda l, i: (i, 0)),
                  _const_spec((1, d)),
                  pl.BlockSpec((None, d, d2), lambda l, i: (l, 0, 0))],
        out_specs=pl.BlockSpec((None, tr, d2), lambda l, i: (l, i, 0)),
        out_shape=jax.ShapeDtypeStruct((depth, rows, d2), BF16),
        scratch_shapes=[pltpu.VMEM((d, d2), BF16)],
        compiler_params=_cparams(2),
        name="mem_kv",
    )(mem2d, mem_norm.reshape(1, d), wkv)


def _rope_tile(x, cos, sin, lane):
    half = MLA_ROPE // 2
    up = pltpu.roll(x, LANES - half, axis=1)
    dn = pltpu.roll(x, half, axis=1)
    rot = jnp.where(lane < half, -up, dn)
    return jnp.where(lane < MLA_ROPE, x * cos + rot * sin, 0.0)


ROW_PHASES = 8


def _causal_conv_phases(x, xslab, sl, cw_ref, ls, bias=None):
    assert cw_ref.shape[0] == CONV_W and CONV_W - 1 <= CARRY_ROWS
    tm = x.shape[0]
    grp = tm // ROW_PHASES
    xslab[sl, CARRY_ROWS:CARRY_ROWS + tm, :] = x
    ph = [xslab[sl, pl.ds(CARRY_ROWS + j, grp, stride=ROW_PHASES), :]
          for j in range(-(CONV_W - 1), ROW_PHASES)]
    xslab[sl, 0:CARRY_ROWS, :] = x[tm - CARRY_ROWS:, :]
    w = [cw_ref[i:i + 1, ls] for i in range(CONV_W)]
    out = []
    for j in range(ROW_PHASES):
        acc = w[CONV_W - 1] * ph[j + CONV_W - 1]
        if bias is not None:
            acc = acc + bias
        for i in range(CONV_W - 1):
            acc = acc + w[i] * ph[j + i]
        out.append(acc)
    return out


def _even_in_kernel(x_ref, g_ref, win_ref, cw_ref, alog_ref, dtb_ref, qn_ref, kvn_ref,
                    wuq_ref, wukv_ref, cos_ref, sin_ref,
                    q_out, k_out, v_out, z_out, gb_out, qf_out, kf_out, vb_out,
                    xslab, yslab):
    tm = x_ref.shape[0]
    qk_w = GDN_HEADS * GDN_DK
    qkv_w = 3 * qk_w

    @pl.when(pl.program_id(1) == 0)
    def _():
        xslab[:, 0:CARRY_ROWS, :] = jnp.zeros((qkv_w // LANES, CARRY_ROWS, LANES), F32)

    h = _rms(x_ref[...], g_ref[...]).astype(BF16)
    lane = lax.broadcasted_iota(jnp.int32, (tm, LANES), 1)
    cos = cos_ref[...]
    sin = sin_ref[...]
    nope_w = MLA_HEADS * MLA_NOPE
    z_off = qkv_w
    cq_off = z_off + qk_w
    ckv_off = cq_off + qn_ref.shape[1]
    misc_off = ckv_off + kvn_ref.shape[1]
    gw = 2 * GDN_DK
    st = {}

    def proj_cols(lo, width):
        return _dot(h, win_ref[:, lo:lo + width])

    def mm_cq():
        st["cq"] = proj_cols(cq_off, qn_ref.shape[1])

    def ep_cq():
        st["cqb"] = _rms(st.pop("cq"), qn_ref[...] * (MLA_SCALE * LOG2E)).astype(BF16)

    def mm_ckv():
        st["ckv"] = proj_cols(ckv_off, kvn_ref.shape[1])

    def ep_ckv():
        st["ckvb"] = _rms(st.pop("ckv"), kvn_ref[...]).astype(BF16)

    def mm_misc():
        st["misc"] = proj_cols(misc_off, LANES)

    def ep_misc():
        misc = st.pop("misc")
        gdec = -jnp.exp(alog_ref[...]) * _softplus(misc + dtb_ref[...])
        beta = _sigmoid(misc)
        gb_out[...] = jnp.where((lane >= G_LANE) & (lane < G_LANE + GDN_HEADS), gdec,
                                jnp.where((lane >= B_LANE) & (lane < B_LANE + GDN_HEADS),
                                          beta, 0.0))
        st["kpe"] = _rope_tile(misc, cos, sin, lane).astype(BF16)

    def mm_uq():
        st["qf"] = _dot(st.pop("cqb"), wuq_ref[...])

    def ep_uq():
        qf = st.pop("qf")
        for hh in range(MLA_HEADS):
            base = 2 * LANES * hh
            qf_out[:, base:base + LANES] = qf[:, hh * LANES:(hh + 1) * LANES].astype(BF16)
            pe = qf[:, nope_w + hh * LANES: nope_w + (hh + 1) * LANES]
            qf_out[:, base + LANES:base + 2 * LANES] = _rope_tile(pe, cos, sin, lane).astype(BF16)

    def mm_ukv():
        st["kv"] = _dot(st.pop("ckvb"), wukv_ref[...])

    def ep_ukv():
        kv = st.pop("kv")
        for hh in range(MLA_HEADS):
            base = 2 * LANES * hh
            kf_out[:, base:base + LANES] = kv[:, hh * LANES:(hh + 1) * LANES].astype(BF16)
            kf_out[:, base + LANES:base + 2 * LANES] = st["kpe"]
        vb_out[...] = kv[:, nope_w:].astype(BF16)

    def mm_group(lo):
        def run():
            st[lo] = proj_cols(lo, gw)
        return run

    def ep_qkv(lo):
        def run():
            raw = st.pop(lo)
            which, off = divmod(lo, qk_w)
            out, scale = ((q_out, GDN_DK ** -0.5), (k_out, 1.0), (v_out, None))[which]
            grp = tm // ROW_PHASES
            for sub in range(gw // LANES):
                sl = lo // LANES + sub
                ls = slice(lo + sub * LANES, lo + (sub + 1) * LANES)
                blocks = _causal_conv_phases(raw[:, sub * LANES:(sub + 1) * LANES], xslab, sl,
                                             cw_ref, ls)
                for j, acc in enumerate(blocks):
                    a = acc * _sigmoid(acc)
                    if scale is not None:
                        a = a * (lax.rsqrt(jnp.sum(a * a, axis=-1, keepdims=True) + EPS) * scale)
                    yslab[sl, pl.ds(j, grp, stride=ROW_PHASES), :] = a
                out[:, off + sub * LANES: off + (sub + 1) * LANES] = yslab[sl].astype(BF16)
        return run

    def ep_z(lo):
        def run():
            z_out[:, lo - z_off: lo - z_off + gw] = st.pop(lo).astype(BF16)
        return run

    pairs = [(mm_group(lo), ep_qkv(lo)) for lo in range(0, qkv_w, gw)]
    pairs += [(mm_cq, ep_cq), (mm_ckv, ep_ckv), (mm_misc, ep_misc), (mm_uq, ep_uq), (mm_ukv, ep_ukv)]
    pairs += [(mm_group(lo), ep_z(lo)) for lo in range(z_off, cq_off, gw)]
    for mm, ep in pairs:
        mm()
        ep()


def _even_inproj(xf, g0, w_in_re, conv_w, alog_row, dtb_row, q_norm, kv_norm, wuq_re, wukv_re,
                 cos_t, sin_t, batch, tm):
    n, d = xf.shape
    t = n // batch
    nt = t // tm
    e_in = w_in_re.shape[1]
    qkv_w = conv_w.shape[1]
    qk_w = qkv_w // 3
    row = lambda w: pl.BlockSpec((tm, w), lambda b, i: (b * nt + i, 0))
    outs = [(qk_w, BF16)] * 4 + [(LANES, F32), (2 * LANES * MLA_HEADS, BF16),
                                 (2 * LANES * MLA_HEADS, BF16), (LANES * MLA_HEADS, BF16)]
    return pl.pallas_call(
        _even_in_kernel,
        grid=(batch, nt),
        in_specs=[row(d), _const_spec((1, d)), _const_spec((d, e_in)), _const_spec((CONV_W, qkv_w)),
                  _const_spec((1, LANES)), _const_spec((1, LANES)),
                  _const_spec(q_norm.shape), _const_spec(kv_norm.shape),
                  _const_spec(wuq_re.shape), _const_spec(wukv_re.shape),
                  row(LANES), row(LANES)],
        out_specs=[row(w) for w, _ in outs],
        out_shape=[jax.ShapeDtypeStruct((n, w), dt) for w, dt in outs],
        scratch_shapes=[pltpu.VMEM((qkv_w // LANES, tm + CARRY_ROWS, LANES), F32),
                        pltpu.VMEM((qkv_w // LANES, tm, LANES), F32)],
        compiler_params=_cparams(2),
        name="even_inproj",
    )(xf, g0, w_in_re, conv_w, alog_row, dtb_row, q_norm, kv_norm, wuq_re, wukv_re, cos_t, sin_t)


SUPER = 2 * CHUNK
G_LANE = MLA_ROPE
B_LANE = MLA_ROPE + GDN_HEADS
GDN_PIPE = 2


def _gdn_kernel(q_ref, k_ref, v_ref, z_ref, gb_ref, on_ref, out_ref,
                gc_scr, gct_scr, gl_scr, u_scr, w_scr, a_scr, qe_scr, kd_scr, s_scr):
    t = q_ref.shape[0]
    n_super = t // SUPER
    n_chunk = t // CHUNK

    gb = gb_ref[...]
    rowi = lax.broadcasted_iota(jnp.int32, gb.shape, 0) & (CHUNK - 1)
    gc = gb
    s = 1
    while s < CHUNK:
        gc = jnp.where(rowi >= s, gc + pltpu.roll(gc, s, axis=0), gc)
        s *= 2
    gc_scr[...] = gc
    gct_scr[...] = gc.T
    g3 = gc.reshape(n_chunk, CHUNK, LANES)
    gl_scr[...] = jnp.broadcast_to(g3[:, CHUNK - 1:CHUNK, :], g3.shape).reshape(t, LANES)

    ri = lax.broadcasted_iota(jnp.int32, (SUPER, SUPER), 0)
    ci = lax.broadcasted_iota(jnp.int32, (SUPER, SUPER), 1)
    same = (ri >= CHUNK) == (ci >= CHUNK)
    causal = same & (ri >= ci)
    strict = same & (ri > ci)
    eye = (ri == ci).astype(F32)

    on = on_ref[...]
    heads = range(GDN_HEADS)
    hsl = [slice(hh * GDN_DK, (hh + 1) * GDN_DK) for hh in heads]
    lane_g = [slice(G_LANE + hh, G_LANE + hh + 1) for hh in heads]
    lane_b = [slice(B_LANE + hh, B_LANE + hh + 1) for hh in heads]

    def solve_stages(sc0):
        probs = []
        for sub in range(GDN_PIPE):
            rows = pl.ds(pl.multiple_of((sc0 + sub) * SUPER, SUPER), SUPER)
            probs += [(rows, hh) for hh in heads]
        kc = [k_ref[rows, hsl[hh]] for rows, hh in probs]
        qc = [q_ref[rows, hsl[hh]] for rows, hh in probs]
        gcol = [gc_scr[rows, lane_g[hh]] for rows, hh in probs]
        bcol = [gb_ref[rows, lane_b[hh]] for rows, hh in probs]
        grow = [gct_scr[lane_g[hh], rows] for rows, hh in probs]
        kk = [_dot_nt(k, k) for k in kc]
        qk = [_dot_nt(q, k) for q, k in zip(qc, kc)]
        yield
        decay = [jnp.where(causal, jnp.exp(jnp.where(causal, gc - gr, 0.0)), 0.0)
                 for gc, gr in zip(gcol, grow)]
        m = [jnp.where(strict, -(x * b * dc), 0.0) for x, b, dc in zip(kk, bcol, decay)]
        qacc = [eye + x for x in m]
        mb = [x.astype(BF16) for x in m]
        mj = [_dot(x, x) for x in mb]
        yield
        lvl = 2
        while lvl < CHUNK // 2:
            mb = [x.astype(BF16) for x in mj]
            r = [_dot(x, jnp.concatenate([x, qa.astype(BF16)], axis=1)) for x, qa in zip(mb, qacc)]
            yield
            mj = [x[:, :SUPER] for x in r]
            qacc = [qa + x[:, SUPER:] for qa, x in zip(qacc, r)]
            lvl *= 2
        last = [_dot(x.astype(BF16), qa.astype(BF16)) for qa, x in zip(qacc, mj)]
        yield
        qacc = [qa + x for qa, x in zip(qacc, last)]
        eg = [jnp.exp(gc) for gc in gcol]
        sol = []
        for i, (rows, hh) in enumerate(probs):
            vc = v_ref[rows, hsl[hh]].astype(F32)
            kf = kc[i].astype(F32)
            rhs = jnp.concatenate([vc * bcol[i], kf * (bcol[i] * eg[i])], axis=1).astype(BF16)
            sol.append(_dot(qacc[i].astype(BF16), rhs))
        yield
        for i, (rows, hh) in enumerate(probs):
            u_scr[rows, hsl[hh]] = sol[i][:, :GDN_DK]
            w_scr[rows, hsl[hh]] = sol[i][:, GDN_DK:].astype(BF16)
            aqk = qk[i] * decay[i]
            a_sh = pltpu.roll(aqk, CHUNK, axis=1)
            a_scr[rows, hsl[hh]] = jnp.where(ri < CHUNK, aqk, a_sh).astype(BF16)
            qe_scr[rows, hsl[hh]] = (qc[i].astype(F32) * eg[i]).astype(BF16)
            glast = gl_scr[rows, lane_g[hh]]
            kd_scr[rows, hsl[hh]] = (kc[i].astype(F32) * jnp.exp(glast - gcol[i])).astype(BF16)

    def scan_stages(sc0, st):
        for half in range(GDN_PIPE * SUPER // CHUNK):
            r0 = pl.multiple_of(sc0 * SUPER + half * CHUNK, CHUNK)
            rows = pl.ds(r0, CHUNK)
            sb = [x.astype(BF16) for x in st]
            t1 = [_dot(w_scr[rows, hsl[hh]], sb[hh]) for hh in heads]
            oq = [_dot(qe_scr[rows, hsl[hh]], sb[hh]) for hh in heads]
            yield
            vb = [(u_scr[rows, hsl[hh]] - t1[hh]).astype(BF16) for hh in heads]
            oa = [_dot(a_scr[rows, hsl[hh]][:, :CHUNK], vb[hh]) for hh in heads]
            kv = [lax.dot_general(kd_scr[rows, hsl[hh]], vb[hh], TN_DIMS,
                                  preferred_element_type=F32) for hh in heads]
            yield
            for hh in heads:
                gam = jnp.exp(gl_scr[pl.ds(r0, 1), lane_g[hh]])
                st[hh] = st[hh] * gam + kv[hh]
                z = z_ref[rows, hsl[hh]].astype(F32)
                out_ref[rows, hsl[hh]] = (
                    _rms(oq[hh] + oa[hh], on) * (z * _sigmoid(z))).astype(BF16)

    _round_robin([solve_stages(0)])
    s_scr[...] = jnp.zeros(s_scr.shape, F32)
    n_steps = n_super // GDN_PIPE

    def pipe_body(it, _):
        st = [s_scr[hh] for hh in heads]
        _round_robin([solve_stages((it + 1) * GDN_PIPE), scan_stages(it * GDN_PIPE, st)])
        for hh in heads:
            s_scr[hh] = st[hh]
        return 0

    lax.fori_loop(0, n_steps - 1, pipe_body, 0)
    st = [s_scr[hh] for hh in heads]
    _round_robin([scan_stages((n_steps - 1) * GDN_PIPE, st)])


def _gdn(q, k, v, z, gb, o_norm, batch):
    n, w = q.shape
    t = n // batch
    row = lambda ww: pl.BlockSpec((t, ww), lambda b: (b, 0))
    return pl.pallas_call(
        _gdn_kernel,
        grid=(batch,),
        in_specs=[row(w), row(w), row(w), row(w), row(LANES), _const_spec((1, GDN_DK))],
        out_specs=row(w),
        out_shape=jax.ShapeDtypeStruct((n, w), BF16),
        scratch_shapes=[pltpu.VMEM((t, LANES), F32),
                        pltpu.VMEM((LANES, t), F32),
                        pltpu.VMEM((t, LANES), F32),
                        pltpu.VMEM((t, w), F32),
                        pltpu.VMEM((t, w), BF16),
                        pltpu.VMEM((t, w), BF16),
                        pltpu.VMEM((t, w), BF16),
                        pltpu.VMEM((t, w), BF16),
                        pltpu.VMEM((GDN_HEADS, GDN_DK, GDN_DK), F32)],
        compiler_params=_cparams(1),
        name="gdn",
    )(q, k, v, z, gb, o_norm)


MLA_TQ = 256


def _mla_kernel(q_ref, k_ref, v_ref, *rest):
    n_cast = (len(rest) - 1) // 2
    o_ref = rest[n_cast]
    for src, dst in zip(rest[:n_cast], rest[n_cast + 1:]):
        dst[...] = src[...].astype(BF16)
    t = q_ref.shape[0]
    tq = MLA_TQ
    nq = t // tq
    hw = 2 * LANES
    heads = range(q_ref.shape[1] // hw)
    ri = lax.broadcasted_iota(jnp.int32, (tq, tq), 0) // CHUNK
    ci = lax.broadcasted_iota(jnp.int32, (tq, tq), 1) // CHUNK
    diag_mask = ci <= ri

    groups = [(i, h) for i in range(nq) for h in heads]

    def scores(i, h):
        kw = (i + 1) * tq
        return _dot_nt(q_ref[i * tq:(i + 1) * tq, h * hw:(h + 1) * hw],
                       k_ref[0:kw, h * hw:(h + 1) * hw])

    s_next = scores(*groups[0])
    for g, (i, h) in enumerate(groups):
        s = s_next
        if g + 1 < len(groups):
            s_next = scores(*groups[g + 1])
        kw = (i + 1) * tq
        s_diag = jnp.where(diag_mask, s[:, kw - tq:], -jnp.inf)
        s = s_diag if i == 0 else jnp.concatenate([s[:, :kw - tq], s_diag], axis=1)
        m = jnp.max(s, axis=-1, keepdims=True)
        p = jnp.exp2(s - m)
        l = jnp.sum(p, axis=-1, keepdims=True)
        pv = _dot(p.astype(BF16), v_ref[0:kw, h * LANES:(h + 1) * LANES])
        o_ref[i * tq:(i + 1) * tq, h * LANES:(h + 1) * LANES] = (pv / l).astype(BF16)


MLA_HEADS_PER_STEP = 2


def _mla(qf, kf, vb, batch, cast_weights=()):
    n = qf.shape[0]
    t = n // batch
    hps = MLA_HEADS_PER_STEP
    hg = MLA_HEADS // hps
    steps = batch * hg
    spec = lambda w: pl.BlockSpec((t, w * hps), lambda b, g: (b, g))
    cast_specs = [pl.BlockSpec((w.shape[0], w.shape[1] // steps, w.shape[2]),
                               lambda b, g: (0, b * hg + g, 0)) for w in cast_weights]
    outs = pl.pallas_call(
        _mla_kernel,
        grid=(batch, hg),
        in_specs=[spec(2 * LANES), spec(2 * LANES), spec(LANES)] + cast_specs,
        out_specs=[spec(LANES)] + cast_specs,
        out_shape=[jax.ShapeDtypeStruct((n, vb.shape[1]), BF16)]
                  + [jax.ShapeDtypeStruct(w.shape, BF16) for w in cast_weights],
        compiler_params=_cparams(2),
        name="mla_attn",
    )(qf, kf, vb, *cast_weights)
    return outs[0], outs[1:]


SCAN_PHASES = ROW_PHASES


def _gelu_tanh(x):
    c2 = 2.0 * math.sqrt(2.0 / math.pi) * LOG2E
    return x / (1.0 + jnp.exp2(x * (-c2 - (c2 * 0.044715) * (x * x))))


def _odd_kernel(x_ref, g0_ref, win_f32, cw_ref, cb_ref, gaw_f32, gab_ref, gxw_f32, gxb_ref,
                ap_ref, wout_f32, g1_ref, o_ref, xslab, hslab, hcar,
                win_ref, gaw_ref, gxw_ref, wout_ref):
    tm, d = x_ref.shape
    bw = d // LRU_BLOCKS
    n_slab = d // LANES
    assert cw_ref.shape[0] == CONV_W and CONV_W - 1 <= CARRY_ROWS
    _cast_once([(win_f32, win_ref), (gaw_f32, gaw_ref), (gxw_f32, gxw_ref), (wout_f32, wout_ref)])

    @pl.when(pl.program_id(1) == 0)
    def _():
        xslab[:, 0:CARRY_ROWS, :] = jnp.zeros((n_slab, CARRY_ROWS, LANES), F32)
        hcar[...] = jnp.zeros(hcar.shape, F32)

    x = x_ref[...]
    h = _rms(x, g0_ref[...]).astype(BF16)
    grp = tm // SCAN_PHASES
    slabs_per_block = bw // LANES
    rowg = lax.broadcasted_iota(jnp.int32, (grp, 1), 0)

    blocks = range(LRU_BLOCKS)
    csl = [slice(n * bw, (n + 1) * bw) for n in blocks]
    st = [dict() for _ in blocks]

    def projx(n):
        st[n]["xb"] = _dot(h, win_ref[:, csl[n]])

    def projy(n):
        st[n]["yb"] = _dot(h, win_ref[:, d + n * bw: d + (n + 1) * bw])

    def conv(n):
        xb = st[n].pop("xb")
        cols = []
        for k in range(slabs_per_block):
            sl = n * slabs_per_block + k
            ls = slice(n * bw + k * LANES, n * bw + (k + 1) * LANES)
            out = _causal_conv_phases(xb[:, k * LANES:(k + 1) * LANES], xslab, sl, cw_ref, ls,
                                      bias=cb_ref[:, ls])
            cols.append(jnp.concatenate(out, axis=0))
        st[n]["xc"] = jnp.concatenate(cols, axis=1)

    def gates(n):
        xcb = st[n]["xc"].astype(BF16)
        st[n]["ra"] = _dot(xcb, gaw_ref[n])
        st[n]["ia"] = _dot(xcb, gxw_ref[n])

    def recur(n):
        cs, xc = csl[n], st[n].pop("xc")
        r = _sigmoid(st[n].pop("ra") + gab_ref[:, cs])
        ig = _sigmoid(st[n].pop("ia") + gxb_ref[:, cs])
        a = jnp.exp2(r * ((-LRU_C * LOG2E) * _softplus(-ap_ref[:, cs])))
        om = 1.0 - a * a
        u = jnp.where(om > 0.0, om * lax.rsqrt(om), 0.0) * (ig * xc)

        ca, cu = a[0:grp], u[0:grp]
        la, lu = [ca], [cu]
        for j in range(1, SCAN_PHASES):
            aj, uj = a[j * grp:(j + 1) * grp], u[j * grp:(j + 1) * grp]
            cu = aj * cu + uj
            ca = aj * ca
            la.append(ca)
            lu.append(cu)
        ga, gu = ca, cu
        s = 1
        while s < grp:
            valid = rowg >= s
            gu = jnp.where(valid, ga * pltpu.roll(gu, s, axis=0) + gu, gu)
            ga = jnp.where(valid, ga * pltpu.roll(ga, s, axis=0), ga)
            s *= 2
        h_in = hcar[:, cs]
        h_end = ga * h_in + gu
        hcar[:, cs] = h_end[grp - 1:grp, :]
        h_prev = jnp.where(rowg == 0, h_in, pltpu.roll(h_end, 1, axis=0))
        slabs = range(n * slabs_per_block, (n + 1) * slabs_per_block)
        for j in range(SCAN_PHASES):
            hj = la[j] * h_prev + lu[j]
            for k, sl in enumerate(slabs):
                hslab[sl, pl.ds(j, grp, stride=SCAN_PHASES), :] = hj[:, k * LANES:(k + 1) * LANES]
        st[n]["hs"] = jnp.concatenate([hslab[sl] for sl in slabs], axis=1)

    def gate(n):
        st[n]["hg"] = (st[n].pop("hs") * _gelu_tanh(st[n].pop("yb"))).astype(BF16)

    def outp(n):
        st[n]["y"] = _dot(st[n].pop("hg"), wout_ref[csl[n], :])

    chain = (projx, conv, gates, recur, projy, gate, outp)
    for k in range(LRU_BLOCKS + len(chain) - 1):
        for s, stage in enumerate(chain):
            if 0 <= k - s < LRU_BLOCKS:
                stage(k - s)
    y = functools.reduce(lambda p, q: p + q, [st[n]["y"] for n in blocks])
    o_ref[...] = x + _rms(y, g1_ref[...])


def _odd_mixer(xf, g0, g1, w_in, conv_w, conv_b, gaw, gab, gxw, gxb, a_param, w_out, o, batch, tm):
    n, d = xf.shape
    t = n // batch
    nt = t // tm
    row = pl.BlockSpec((tm, d), lambda b, i: (b * nt + i, 0))
    vec = _const_spec((1, d))
    weights = (w_in, gaw, gxw, w_out)
    wspec = [_layer_spec(w.shape, o) for w in weights]
    return pl.pallas_call(
        _odd_kernel,
        grid=(batch, nt),
        in_specs=[row, vec, wspec[0], _const_spec(conv_w.shape), vec,
                  wspec[1], vec, wspec[2], vec, vec, wspec[3], vec],
        out_specs=row,
        out_shape=jax.ShapeDtypeStruct((n, d), F32),
        scratch_shapes=[pltpu.VMEM((d // LANES, tm + CARRY_ROWS, LANES), F32),
                        pltpu.VMEM((d // LANES, tm, LANES), F32),
                        pltpu.VMEM((1, d), F32)]
                       + [pltpu.VMEM(w.shape[1:], BF16) for w in weights],
        compiler_params=_cparams(2),
        name="odd_mixer",
    )(xf, g0, w_in, conv_w, conv_b, gaw, gab, gxw, gxb, a_param, w_out, g1)


XA_SPLIT = 4


def _round_robin(gens):
    live = list(gens)
    while live:
        live = [g for g in live if next(g, StopIteration) is not StopIteration]


def _cast_once(pairs, n_axes=2):
    first = pl.program_id(0) == 0
    for axis in range(1, n_axes):
        first = first & (pl.program_id(axis) == 0)

    @pl.when(first)
    def _():
        for src, dst in pairs:
            dst[...] = src[...].astype(BF16)


def _xattn_kernel(x_ref, g2_ref, wq_f32, kv_ref, wo_f32, g3_ref, o_ref, wq_ref, wo_ref):
    _cast_once([(wq_f32, wq_ref), (wo_f32, wo_ref)])
    tm = x_ref.shape[0]
    rs = tm // XA_SPLIT
    _round_robin([_xattn_stages(slice(k * rs, (k + 1) * rs), x_ref, None, g2_ref, wq_ref, kv_ref,
                                wo_ref, g3_ref, o_ref) for k in range(XA_SPLIT)])


def _outproj_xattn_kernel(x_ref, a_ref, b_ref, w_f32, g1_ref,
                          g2_ref, wq_f32, kv_ref, wo_f32, g3_ref, o_ref, wq_ref, wo_ref, w_ref):
    _cast_once([(wq_f32, wq_ref), (wo_f32, wo_ref), (w_f32, w_ref)])
    tm = x_ref.shape[0]
    rs = tm // XA_SPLIT
    _round_robin([_xattn_stages(slice(k * rs, (k + 1) * rs), x_ref, (a_ref, b_ref, w_ref, g1_ref),
                                g2_ref, wq_ref, kv_ref, wo_ref, g3_ref, o_ref)
                  for k in range(XA_SPLIT)])


def _xattn_stages(rows, x_ref, mix, g2_ref, wq_ref, kv_ref, wo_ref, g3_ref, o_ref):
    x = x_ref[rows, :]
    d = x.shape[1]
    hd = d // XA_HEADS
    if mix is not None:
        a_ref, b_ref, w_ref, g1_ref = mix
        ka = a_ref.shape[1]
        y = _dot(a_ref[rows, :], w_ref[0:ka, :]) + _dot(b_ref[rows, :], w_ref[ka:, :])
        yield
        x = x + _rms(y, g1_ref[...])
    h = (x * g2_ref[...]).astype(BF16)
    r = lax.rsqrt(jnp.mean(x * x, axis=-1, keepdims=True) + EPS)
    q = _dot(h, wq_ref[...])
    yield
    q = (q * (r * (hd ** -0.5 * LOG2E))).astype(BF16)
    s = [_dot_nt(q[:, hh * hd:(hh + 1) * hd], kv_ref[:, hh * hd:(hh + 1) * hd])
         for hh in range(XA_HEADS)]
    yield
    outs = []
    for hh in range(XA_HEADS):
        m = jnp.max(s[hh], axis=-1, keepdims=True)
        p = jnp.exp2(s[hh] - m)
        l = jnp.sum(p, axis=-1, keepdims=True)
        pv = _dot(p.astype(BF16), kv_ref[:, d + hh * hd: d + (hh + 1) * hd])
        outs.append((pv / l).astype(BF16))
    yield
    y = _dot(jnp.concatenate(outs, axis=1), wo_ref[...])
    yield
    o_ref[rows, :] = x + _rms(y, g3_ref[...])


def _xattn(xf, g2, g3, wq, kv_mem, wo, layer, batch, tm, mix=None):
    n, d = xf.shape
    t = n // batch
    nt = t // tm
    n_mem = kv_mem.shape[1] // batch
    row = lambda w: pl.BlockSpec((tm, w), lambda b, i: (b * nt + i, 0))
    vec = _const_spec((1, d))
    xa_specs = [vec, _layer_spec(wq.shape, layer),
                pl.BlockSpec((None, n_mem, 2 * d), lambda b, i: (layer, b, 0)),
                _layer_spec(wo.shape, layer), vec]
    xa_args = (g2, wq, kv_mem, wo, g3)
    scratch = [pltpu.VMEM(wq.shape[1:], BF16), pltpu.VMEM(wo.shape[1:], BF16)]
    if mix is None:
        body, specs, args, name = _xattn_kernel, [row(d)] + xa_specs, (xf,) + xa_args, "xattn"
    else:
        a, b, w_out, w_out_layer, g1 = mix
        body, name = _outproj_xattn_kernel, "outproj_xattn"
        specs = [row(d), row(a.shape[1]), row(b.shape[1]), _layer_spec(w_out.shape, w_out_layer),
                 vec] + xa_specs
        args = (xf, a, b, w_out, g1) + xa_args
        scratch.append(pltpu.VMEM(w_out.shape[1:], BF16))
    return pl.pallas_call(
        body,
        grid=(batch, nt),
        in_specs=specs,
        out_specs=row(d),
        out_shape=jax.ShapeDtypeStruct((n, d), F32),
        scratch_shapes=scratch,
        compiler_params=_cparams(2),
        name=name,
    )(*args)


FFN_TF = 256


def _ffn_kernel(x_ref, g4_ref, win_ref, wo_ref, g5_ref, o_ref, act_scr):
    dff = wo_ref.shape[0]
    x = x_ref[...]
    h = (x * g4_ref[...]).astype(BF16)
    r = lax.rsqrt(jnp.mean(x * x, axis=-1, keepdims=True) + EPS)
    for c in range(dff // FFN_TF):
        cols = slice(c * FFN_TF, (c + 1) * FFN_TF)
        gt = _dot(h, win_ref[:, cols]) * r
        up = _dot(h, win_ref[:, dff + c * FFN_TF: dff + (c + 1) * FFN_TF]) * r
        act_scr[:, cols] = (gt * _sigmoid(gt) * up).astype(BF16)
    half = x.shape[0] // 2
    for rows in (slice(0, half), slice(half, 2 * half)):
        y = _dot(act_scr[rows, :], wo_ref[...])
        o_ref[rows, :] = x[rows, :] + _rms(y, g5_ref[...])


def _ffn(xf, g4, g5, w_in, w_out, layer, tm):
    n, d = xf.shape
    dff = w_out.shape[1]
    row = pl.BlockSpec((tm, d), lambda i: (i, 0))
    vec = _const_spec((1, d))
    return pl.pallas_call(
        _ffn_kernel,
        grid=(n // tm,),
        in_specs=[row, vec, _layer_spec(w_in.shape, layer), _layer_spec(w_out.shape, layer), vec],
        out_specs=row,
        out_shape=jax.ShapeDtypeStruct((n, d), F32),
        scratch_shapes=[pltpu.VMEM((tm, dff), BF16)],
        compiler_params=_cparams(1),
        name="ffn",
    )(xf, g4, w_in, w_out, g5)


def _relayout_even(w_in, w_uq, w_ukv, a_log, dt_bias):
    d = w_in.shape[0]
    qkvz = 3 * GDN_HEADS * GDN_DK + GDN_HEADS * GDN_DK
    ab = 2 * GDN_HEADS
    lat = w_in.shape[1] - qkvz - ab - MLA_ROPE
    pad = LANES - MLA_ROPE - ab
    w_in = w_in.astype(BF16)
    w_in_re = jnp.concatenate(
        [w_in[:, :qkvz], w_in[:, qkvz + ab:qkvz + ab + lat], w_in[:, qkvz + ab + lat:],
         w_in[:, qkvz:qkvz + ab], jnp.zeros((d, pad), w_in.dtype)], axis=1)
    r = w_uq.shape[0]
    uq = w_uq.reshape(r, MLA_HEADS, MLA_NOPE + MLA_ROPE)
    pe = jnp.pad(uq[:, :, MLA_NOPE:], ((0, 0), (0, 0), (0, LANES - MLA_ROPE)))
    wuq_re = jnp.concatenate([uq[:, :, :MLA_NOPE].reshape(r, -1), pe.reshape(r, -1)],
                             axis=1).astype(BF16)
    ukv = w_ukv.reshape(w_ukv.shape[0], MLA_HEADS, -1)
    wukv_re = jnp.concatenate([ukv[:, :, :MLA_NOPE].reshape(r, -1),
                               ukv[:, :, MLA_NOPE:].reshape(r, -1)], axis=1).astype(BF16)
    alog_row = jnp.zeros((1, LANES), F32).at[0, G_LANE:G_LANE + GDN_HEADS].set(a_log)
    dtb_row = jnp.zeros((1, LANES), F32).at[0, G_LANE:G_LANE + GDN_HEADS].set(dt_bias)
    return w_in_re, wuq_re, wukv_re, alog_row, dtb_row


def kernel(x, mem, positions, norm_gains, mem_norm, e_w_in, e_conv_w, e_a_log, e_dt_bias, e_o_norm,
           e_q_norm, e_kv_norm, e_w_uq, e_w_ukv, e_w_out, o_w_in, o_conv_w, o_conv_b, o_gate_a_w,
           o_gate_a_b, o_gate_x_w, o_gate_x_b, o_a_param, o_w_out, xa_wq, xa_wkv, xa_wo, ffn_w_in,
           ffn_w_out):
    batch, t, d = x.shape
    n = batch * t
    depth = norm_gains.shape[0]
    xf = x.reshape(n, d)
    gains = norm_gains.reshape(depth, -1, 1, d)

    cos_t, sin_t = _rope_tables(positions)
    kv_mem = _mem_kv(mem.reshape(-1, d), mem_norm, xa_wkv)
    ffn_w_b = None

    for layer in range(depth):
        g = gains[layer]
        mix = None
        if layer % 2 == 0:
            e = layer // 2
            w_in_re, wuq_re, wukv_re, alog_row, dtb_row = _relayout_even(
                e_w_in[e], e_w_uq[e], e_w_ukv[e], e_a_log[e], e_dt_bias[e])
            q, k, v, z, gb, qf, kf, vb = _even_inproj(
                xf, g[0], w_in_re, e_conv_w[e], alog_row, dtb_row,
                e_q_norm[e].reshape(1, -1), e_kv_norm[e].reshape(1, -1), wuq_re, wukv_re,
                cos_t, sin_t, batch, tm=ROW_TILE["even_inproj"])
            out_a = _gdn(q, k, v, z, gb, e_o_norm[e].reshape(1, -1), batch)
            if ffn_w_b is None:
                out_b, ffn_w_b = _mla(qf, kf, vb, batch, cast_weights=(ffn_w_in, ffn_w_out))
            else:
                out_b, _ = _mla(qf, kf, vb, batch)
            mix = (out_a, out_b, e_w_out, e, g[1])
        else:
            o = layer // 2
            xf = _odd_mixer(xf, g[0], g[1], o_w_in, o_conv_w[o],
                            o_conv_b[o].reshape(1, d), o_gate_a_w,
                            o_gate_a_b[o].reshape(1, d), o_gate_x_w,
                            o_gate_x_b[o].reshape(1, d), o_a_param[o].reshape(1, d),
                            o_w_out, o, batch, tm=ROW_TILE["odd_mixer"])
        xf = _xattn(xf, g[2], g[3], xa_wq, kv_mem, xa_wo, layer, batch, tm=ROW_TILE["xattn"],
                    mix=mix)
        xf = _ffn(xf, g[4], g[5], ffn_w_b[0], ffn_w_b[1], layer, tm=ROW_TILE["ffn"])
    return xf.reshape(batch, t, d)
```

```python
import functools
import math

import jax
import jax.numpy as jnp
from jax import lax
from jax.experimental import pallas as pl
from jax.experimental.pallas import tpu as pltpu

F32 = jnp.float32
BF16 = jnp.bfloat16

EPS = 1e-6
CHUNK = 64
GDN_HEADS = 4
GDN_DK = 128
MLA_HEADS = 4
MLA_NOPE = 128
MLA_ROPE = 64
MLA_SCALE = (MLA_NOPE + MLA_ROPE) ** -0.5
LOG2E = math.log2(math.e)
ROPE_BASE = 10000.0
LRU_BLOCKS = 4
LRU_C = 8.0
XA_HEADS = 4
CONV_W = 4

LANES = 128
CARRY_ROWS = 8
V7X_VMEM_LIMIT = 56 * 1024 * 1024

ROW_TILE = {"even_inproj": 512, "odd_mixer": 512, "xattn": 1024, "ffn": 1024}

NT_DIMS = (((1,), (1,)), ((), ()))
TN_DIMS = (((0,), (0,)), ((), ()))


def _cparams(n_axes):
    return pltpu.CompilerParams(
        dimension_semantics=("arbitrary",) * n_axes,
        vmem_limit_bytes=V7X_VMEM_LIMIT)


def _dot(a, b):
    return jnp.dot(a, b, preferred_element_type=F32)


def _dot_nt(a, b):
    return lax.dot_general(a, b, NT_DIMS, preferred_element_type=F32)


def _rms(xf, g):
    ms = jnp.mean(xf * xf, axis=-1, keepdims=True)
    return xf * lax.rsqrt(ms + EPS) * g


def _sigmoid(x):
    return 1.0 / (1.0 + jnp.exp(-x))


def _softplus(x):
    return jnp.maximum(x, 0.0) + jnp.log1p(jnp.exp(-jnp.abs(x)))


def _const_spec(shape):
    nd = len(shape)
    return pl.BlockSpec(shape, lambda *_: (0,) * nd)


def _layer_spec(shape, layer):
    nd = len(shape) - 1
    return pl.BlockSpec((None,) + tuple(shape[1:]), lambda *_: (layer,) + (0,) * nd,
                        pipeline_mode=pl.Buffered(1))


def _rope_kernel(pos_ref, inv_ref, cos_ref, sin_ref):
    half = MLA_ROPE // 2
    per_row = LANES // half
    tr = pos_ref.shape[0]
    ang = pos_ref[...].astype(F32) * inv_ref[...]
    lane = lax.broadcasted_iota(jnp.int32, ang.shape, 1)
    for tbl, ref in ((jnp.cos(ang), cos_ref), (jnp.sin(ang), sin_ref)):
        for j in range(per_row):
            rep = jnp.where((lane >= half * j) & (lane < half * (j + 1)), tbl, 0.0)
            width = half
            while width < LANES:
                rep = rep + pltpu.roll(rep, width, axis=1)
                width *= 2
            ref[pl.ds(j, tr, stride=per_row), :] = rep


def _rope_tables(positions):
    n = positions.size
    half = MLA_ROPE // 2
    per_row = LANES // half
    inv_freq = ROPE_BASE ** (-jnp.arange(0, MLA_ROPE, 2, dtype=F32) / MLA_ROPE)
    inv_row = jnp.tile(inv_freq, per_row).reshape(1, LANES)
    pos_rep = jnp.repeat(positions.reshape(n // per_row, per_row), half, axis=1)
    rows = n // per_row
    tr = min(rows, 512)
    out_spec = pl.BlockSpec((tr * per_row, LANES), lambda i: (i, 0))
    return pl.pallas_call(
        _rope_kernel,
        grid=(rows // tr,),
        in_specs=[pl.BlockSpec((tr, LANES), lambda i: (i, 0)), _const_spec((1, LANES))],
        out_specs=[out_spec, out_spec],
        out_shape=[jax.ShapeDtypeStruct((n, LANES), F32)] * 2,
        compiler_params=_cparams(1),
        name="rope_tables",
    )(pos_rep, inv_row)


def _memkv_kernel(mem_ref, g_ref, w_ref, o_ref, wb_scr):
    @pl.when(pl.program_id(1) == 0)
    def _():
        wb_scr[...] = w_ref[...].astype(BF16)

    mn = _rms(mem_ref[...], g_ref[...]).astype(BF16)
    o_ref[...] = _dot(mn, wb_scr[...]).astype(BF16)


def _mem_kv(mem2d, mem_norm, wkv):
    depth, d, d2 = wkv.shape
    rows = mem2d.shape[0]
    tr = min(rows, 512)
    return pl.pallas_call(
        _memkv_kernel,
        grid=(depth, rows // tr),
        in_specs=[pl.BlockSpec((tr, d), lambda l, i: (i, 0)),
                  _const_spec((1, d)),
                  pl.BlockSpec((None, d, d2), lambda l, i: (l, 0, 0))],
        out_specs=pl.BlockSpec((None, tr, d2), lambda l, i: (l, i, 0)),
        out_shape=jax.ShapeDtypeStruct((depth, rows, d2), BF16),
        scratch_shapes=[pltpu.VMEM((d, d2), BF16)],
        compiler_params=_cparams(2),
        name="mem_kv",
    )(mem2d, mem_norm.reshape(1, d), wkv)


def _rope_tile(x, cos, sin, lane):
    half = MLA_ROPE // 2
    up = pltpu.roll(x, LANES - half, axis=1)
    dn = pltpu.roll(x, half, axis=1)
    rot = jnp.where(lane < half, -up, dn)
    return jnp.where(lane < MLA_ROPE, x * cos + rot * sin, 0.0)


ROW_PHASES = 8


def _causal_conv_phases(x, xslab, sl, cw_ref, ls, bias=None):
    assert cw_ref.shape[0] == CONV_W and CONV_W - 1 <= CARRY_ROWS
    tm = x.shape[0]
    grp = tm // ROW_PHASES
    xslab[sl, CARRY_ROWS:CARRY_ROWS + tm, :] = x
    ph = [xslab[sl, pl.ds(CARRY_ROWS + j, grp, stride=ROW_PHASES), :]
          for j in range(-(CONV_W - 1), ROW_PHASES)]
    xslab[sl, 0:CARRY_ROWS, :] = x[tm - CARRY_ROWS:, :]
    w = [cw_ref[i:i + 1, ls] for i in range(CONV_W)]
    out = []
    for j in range(ROW_PHASES):
        acc = w[CONV_W - 1] * ph[j + CONV_W - 1]
        if bias is not None:
            acc = acc + bias
        for i in range(CONV_W - 1):
            acc = acc + w[i] * ph[j + i]
        out.append(acc)
    return out


def _even_in_kernel(x_ref, g_ref, win_ref, cw_ref, alog_ref, dtb_ref, qn_ref, kvn_ref,
                    wuq_ref, wukv_ref, cos_ref, sin_ref,
                    q_out, k_out, v_out, z_out, gb_out, qf_out, kf_out, vb_out,
                    xslab, yslab):
    tm = x_ref.shape[0]
    qk_w = GDN_HEADS * GDN_DK
    qkv_w = 3 * qk_w

    @pl.when(pl.program_id(1) == 0)
    def _():
        xslab[:, 0:CARRY_ROWS, :] = jnp.zeros((qkv_w // LANES, CARRY_ROWS, LANES), F32)

    h = _rms(x_ref[...], g_ref[...]).astype(BF16)
    lane = lax.broadcasted_iota(jnp.int32, (tm, LANES), 1)
    cos = cos_ref[...]
    sin = sin_ref[...]
    nope_w = MLA_HEADS * MLA_NOPE
    z_off = qkv_w
    cq_off = z_off + qk_w
    ckv_off = cq_off + qn_ref.shape[1]
    misc_off = ckv_off + kvn_ref.shape[1]
    gw = 2 * GDN_DK
    st = {}

    def proj_cols(lo, width):
        return _dot(h, win_ref[:, lo:lo + width])

    def mm_cq():
        st["cq"] = proj_cols(cq_off, qn_ref.shape[1])

    def ep_cq():
        st["cqb"] = _rms(st.pop("cq"), qn_ref[...] * (MLA_SCALE * LOG2E)).astype(BF16)

    def mm_ckv():
        st["ckv"] = proj_cols(ckv_off, kvn_ref.shape[1])

    def ep_ckv():
        st["ckvb"] = _rms(st.pop("ckv"), kvn_ref[...]).astype(BF16)

    def mm_misc():
        st["misc"] = proj_cols(misc_off, LANES)

    def ep_misc():
        misc = st.pop("misc")
        gdec = -jnp.exp(alog_ref[...]) * _softplus(misc + dtb_ref[...])
        beta = _sigmoid(misc)
        gb_out[...] = jnp.where((lane >= G_LANE) & (lane < G_LANE + GDN_HEADS), gdec,
                                jnp.where((lane >= B_LANE) & (lane < B_LANE + GDN_HEADS),
                                          beta, 0.0))
        st["kpe"] = _rope_tile(misc, cos, sin, lane).astype(BF16)

    def mm_uq():
        st["qf"] = _dot(st.pop("cqb"), wuq_ref[...])

    def ep_uq():
        qf = st.pop("qf")
        for hh in range(MLA_HEADS):
            base = 2 * LANES * hh
            qf_out[:, base:base + LANES] = qf[:, hh * LANES:(hh + 1) * LANES].astype(BF16)
            pe = qf[:, nope_w + hh * LANES: nope_w + (hh + 1) * LANES]
            qf_out[:, base + LANES:base + 2 * LANES] = _rope_tile(pe, cos, sin, lane).astype(BF16)

    def mm_ukv():
        st["kv"] = _dot(st.pop("ckvb"), wukv_ref[...])

    def ep_ukv():
        kv = st.pop("kv")
        for hh in range(MLA_HEADS):
            base = 2 * LANES * hh
            kf_out[:, base:base + LANES] = kv[:, hh * LANES:(hh + 1) * LANES].astype(BF16)
            kf_out[:, base + LANES:base + 2 * LANES] = st["kpe"]
        vb_out[...] = kv[:, nope_w:].astype(BF16)

    def mm_group(lo):
        def run():
            st[lo] = proj_cols(lo, gw)
        return run

    def ep_qkv(lo):
        def run():
            raw = st.pop(lo)
            which, off = divmod(lo, qk_w)
            out, scale = ((q_out, GDN_DK ** -0.5), (k_out, 1.0), (v_out, None))[which]
            grp = tm // ROW_PHASES
            for sub in range(gw // LANES):
                sl = lo // LANES + sub
                ls = slice(lo + sub * LANES, lo + (sub + 1) * LANES)
                blocks = _causal_conv_phases(raw[:, sub * LANES:(sub + 1) * LANES], xslab, sl,
                                             cw_ref, ls)
                for j, acc in enumerate(blocks):
                    a = acc * _sigmoid(acc)
                    if scale is not None:
                        a = a * (lax.rsqrt(jnp.sum(a * a, axis=-1, keepdims=True) + EPS) * scale)
                    yslab[sl, pl.ds(j, grp, stride=ROW_PHASES), :] = a
                out[:, off + sub * LANES: off + (sub + 1) * LANES] = yslab[sl].astype(BF16)
        return run

    def ep_z(lo):
        def run():
            z_out[:, lo - z_off: lo - z_off + gw] = st.pop(lo).astype(BF16)
        return run

    pairs = [(mm_group(lo), ep_qkv(lo)) for lo in range(0, qkv_w, gw)]
    pairs += [(mm_cq, ep_cq), (mm_ckv, ep_ckv), (mm_misc, ep_misc), (mm_uq, ep_uq), (mm_ukv, ep_ukv)]
    pairs += [(mm_group(lo), ep_z(lo)) for lo in range(z_off, cq_off, gw)]
    for mm, ep in pairs:
        mm()
        ep()


def _even_inproj(xf, g0, w_in_re, conv_w, alog_row, dtb_row, q_norm, kv_norm, wuq_re, wukv_re,
                 cos_t, sin_t, batch, tm):
    n, d = xf.shape
    t = n // batch
    nt = t // tm
    e_in = w_in_re.shape[1]
    qkv_w = conv_w.shape[1]
    qk_w = qkv_w // 3
    row = lambda w: pl.BlockSpec((tm, w), lambda b, i: (b * nt + i, 0))
    outs = [(qk_w, BF16)] * 4 + [(LANES, F32), (2 * LANES * MLA_HEADS, BF16),
                                 (2 * LANES * MLA_HEADS, BF16), (LANES * MLA_HEADS, BF16)]
    return pl.pallas_call(
        _even_in_kernel,
        grid=(batch, nt),
        in_specs=[row(d), _const_spec((1, d)), _const_spec((d, e_in)), _const_spec((CONV_W, qkv_w)),
                  _const_spec((1, LANES)), _const_spec((1, LANES)),
                  _const_spec(q_norm.shape), _const_spec(kv_norm.shape),
                  _const_spec(wuq_re.shape), _const_spec(wukv_re.shape),
                  row(LANES), row(LANES)],
        out_specs=[row(w) for w, _ in outs],
        out_shape=[jax.ShapeDtypeStruct((n, w), dt) for w, dt in outs],
        scratch_shapes=[pltpu.VMEM((qkv_w // LANES, tm + CARRY_ROWS, LANES), F32),
                        pltpu.VMEM((qkv_w // LANES, tm, LANES), F32)],
        compiler_params=_cparams(2),
        name="even_inproj",
    )(xf, g0, w_in_re, conv_w, alog_row, dtb_row, q_norm, kv_norm, wuq_re, wukv_re, cos_t, sin_t)


SUPER = 2 * CHUNK
G_LANE = MLA_ROPE
B_LANE = MLA_ROPE + GDN_HEADS
GDN_PIPE = 2


def _gdn_kernel(q_ref, k_ref, v_ref, z_ref, gb_ref, on_ref, out_ref,
                gc_scr, gct_scr, gl_scr, u_scr, w_scr, a_scr, qe_scr, kd_scr, s_scr):
    t = q_ref.shape[0]
    n_super = t // SUPER
    n_chunk = t // CHUNK

    gb = gb_ref[...]
    rowi = lax.broadcasted_iota(jnp.int32, gb.shape, 0) & (CHUNK - 1)
    gc = gb
    s = 1
    while s < CHUNK:
        gc = jnp.where(rowi >= s, gc + pltpu.roll(gc, s, axis=0), gc)
        s *= 2
    gc_scr[...] = gc
    gct_scr[...] = gc.T
    g3 = gc.reshape(n_chunk, CHUNK, LANES)
    gl_scr[...] = jnp.broadcast_to(g3[:, CHUNK - 1:CHUNK, :], g3.shape).reshape(t, LANES)

    ri = lax.broadcasted_iota(jnp.int32, (SUPER, SUPER), 0)
    ci = lax.broadcasted_iota(jnp.int32, (SUPER, SUPER), 1)
    same = (ri >= CHUNK) == (ci >= CHUNK)
    causal = same & (ri >= ci)
    strict = same & (ri > ci)
    eye = (ri == ci).astype(F32)

    on = on_ref[...]
    heads = range(GDN_HEADS)
    hsl = [slice(hh * GDN_DK, (hh + 1) * GDN_DK) for hh in heads]
    lane_g = [slice(G_LANE + hh, G_LANE + hh + 1) for hh in heads]
    lane_b = [slice(B_LANE + hh, B_LANE + hh + 1) for hh in heads]

    def solve_stages(sc0):
        probs = []
        for sub in range(GDN_PIPE):
            rows = pl.ds(pl.multiple_of((sc0 + sub) * SUPER, SUPER), SUPER)
            probs += [(rows, hh) for hh in heads]
        kc = [k_ref[rows, hsl[hh]] for rows, hh in probs]
        qc = [q_ref[rows, hsl[hh]] for rows, hh in probs]
        gcol = [gc_scr[rows, lane_g[hh]] for rows, hh in probs]
        bcol = [gb_ref[rows, lane_b[hh]] for rows, hh in probs]
        grow = [gct_scr[lane_g[hh], rows] for rows, hh in probs]
        kk = [_dot_nt(k, k) for k in kc]
        qk = [_dot_nt(q, k) for q, k in zip(qc, kc)]
        yield
        decay = [jnp.where(causal, jnp.exp(jnp.where(causal, gc - gr, 0.0)), 0.0)
                 for gc, gr in zip(gcol, grow)]
        m = [jnp.where(strict, -(x * b * dc), 0.0) for x, b, dc in zip(kk, bcol, decay)]
        qacc = [eye + x for x in m]
        mb = [x.astype(BF16) for x in m]
        mj = [_dot(x, x) for x in mb]
        yield
        lvl = 2
        while lvl < CHUNK // 2:
            mb = [x.astype(BF16) for x in mj]
            r = [_dot(x, jnp.concatenate([x, qa.astype(BF16)], axis=1)) for x, qa in zip(mb, qacc)]
            yield
            mj = [x[:, :SUPER] for x in r]
            qacc = [qa + x[:, SUPER:] for qa, x in zip(qacc, r)]
            lvl *= 2
        last = [_dot(x.astype(BF16), qa.astype(BF16)) for qa, x in zip(qacc, mj)]
        yield
        qacc = [qa + x for qa, x in zip(qacc, last)]
        eg = [jnp.exp(gc) for gc in gcol]
        sol = []
        for i, (rows, hh) in enumerate(probs):
            vc = v_ref[rows, hsl[hh]].astype(F32)
            kf = kc[i].astype(F32)
            rhs = jnp.concatenate([vc * bcol[i], kf * (bcol[i] * eg[i])], axis=1).astype(BF16)
            sol.append(_dot(qacc[i].astype(BF16), rhs))
        yield
        for i, (rows, hh) in enumerate(probs):
            u_scr[rows, hsl[hh]] = sol[i][:, :GDN_DK]
            w_scr[rows, hsl[hh]] = sol[i][:, GDN_DK:].astype(BF16)
            aqk = qk[i] * decay[i]
            a_sh = pltpu.roll(aqk, CHUNK, axis=1)
            a_scr[rows, hsl[hh]] = jnp.where(ri < CHUNK, aqk, a_sh).astype(BF16)
            qe_scr[rows, hsl[hh]] = (qc[i].astype(F32) * eg[i]).astype(BF16)
            glast = gl_scr[rows, lane_g[hh]]
            kd_scr[rows, hsl[hh]] = (kc[i].astype(F32) * jnp.exp(glast - gcol[i])).astype(BF16)

    def scan_stages(sc0, st):
        for half in range(GDN_PIPE * SUPER // CHUNK):
            r0 = pl.multiple_of(sc0 * SUPER + half * CHUNK, CHUNK)
            rows = pl.ds(r0, CHUNK)
            sb = [x.astype(BF16) for x in st]
            t1 = [_dot(w_scr[rows, hsl[hh]], sb[hh]) for hh in heads]
            oq = [_dot(qe_scr[rows, hsl[hh]], sb[hh]) for hh in heads]
            yield
            vb = [(u_scr[rows, hsl[hh]] - t1[hh]).astype(BF16) for hh in heads]
            oa = [_dot(a_scr[rows, hsl[hh]][:, :CHUNK], vb[hh]) for hh in heads]
            kv = [lax.dot_general(kd_scr[rows, hsl[hh]], vb[hh], TN_DIMS,
                                  preferred_element_type=F32) for hh in heads]
            yield
            for hh in heads:
                gam = jnp.exp(gl_scr[pl.ds(r0, 1), lane_g[hh]])
                st[hh] = st[hh] * gam + kv[hh]
                z = z_ref[rows, hsl[hh]].astype(F32)
                out_ref[rows, hsl[hh]] = (
                    _rms(oq[hh] + oa[hh], on) * (z * _sigmoid(z))).astype(BF16)

    _round_robin([solve_stages(0)])
    s_scr[...] = jnp.zeros(s_scr.shape, F32)
    n_steps = n_super // GDN_PIPE

    def pipe_body(it, _):
        st = [s_scr[hh] for hh in heads]
        _round_robin([solve_stages((it + 1) * GDN_PIPE), scan_stages(it * GDN_PIPE, st)])
        for hh in heads:
            s_scr[hh] = st[hh]
        return 0

    lax.fori_loop(0, n_steps - 1, pipe_body, 0)
    st = [s_scr[hh] for hh in heads]
    _round_robin([scan_stages((n_steps - 1) * GDN_PIPE, st)])


def _gdn(q, k, v, z, gb, o_norm, batch):
    n, w = q.shape
    t = n // batch
    row = lambda ww: pl.BlockSpec((t, ww), lambda b: (b, 0))
    return pl.pallas_call(
        _gdn_kernel,
        grid=(batch,),
        in_specs=[row(w), row(w), row(w), row(w), row(LANES), _const_spec((1, GDN_DK))],
        out_specs=row(w),
        out_shape=jax.ShapeDtypeStruct((n, w), BF16),
        scratch_shapes=[pltpu.VMEM((t, LANES), F32),
                        pltpu.VMEM((LANES, t), F32),
                        pltpu.VMEM((t, LANES), F32),
                        pltpu.VMEM((t, w), F32),
                        pltpu.VMEM((t, w), BF16),
                        pltpu.VMEM((t, w), BF16),
                        pltpu.VMEM((t, w), BF16),
                        pltpu.VMEM((t, w), BF16),
                        pltpu.VMEM((GDN_HEADS, GDN_DK, GDN_DK), F32)],
        compiler_params=_cparams(1),
        name="gdn",
    )(q, k, v, z, gb, o_norm)


MLA_TQ = 256


def _mla_kernel(q_ref, k_ref, v_ref, *rest):
    n_cast = (len(rest) - 1) // 2
    o_ref = rest[n_cast]
    for src, dst in zip(rest[:n_cast], rest[n_cast + 1:]):
        dst[...] = src[...].astype(BF16)
    t = q_ref.shape[0]
    tq = MLA_TQ
    nq = t // tq
    hw = 2 * LANES
    heads = range(q_ref.shape[1] // hw)
    ri = lax.broadcasted_iota(jnp.int32, (tq, tq), 0) // CHUNK
    ci = lax.broadcasted_iota(jnp.int32, (tq, tq), 1) // CHUNK
    diag_mask = ci <= ri

    groups = [(i, h) for i in range(nq) for h in heads]

    def scores(i, h):
        kw = (i + 1) * tq
        return _dot_nt(q_ref[i * tq:(i + 1) * tq, h * hw:(h + 1) * hw],
                       k_ref[0:kw, h * hw:(h + 1) * hw])

    s_next = scores(*groups[0])
    for g, (i, h) in enumerate(groups):
        s = s_next
        if g + 1 < len(groups):
            s_next = scores(*groups[g + 1])
        kw = (i + 1) * tq
        s_diag = jnp.where(diag_mask, s[:, kw - tq:], -jnp.inf)
        s = s_diag if i == 0 else jnp.concatenate([s[:, :kw - tq], s_diag], axis=1)
        m = jnp.max(s, axis=-1, keepdims=True)
        p = jnp.exp2(s - m)
        l = jnp.sum(p, axis=-1, keepdims=True)
        pv = _dot(p.astype(BF16), v_ref[0:kw, h * LANES:(h + 1) * LANES])
        o_ref[i * tq:(i + 1) * tq, h * LANES:(h + 1) * LANES] = (pv / l).astype(BF16)


MLA_HEADS_PER_STEP = 2


def _mla(qf, kf, vb, batch, cast_weights=()):
    n = qf.shape[0]
    t = n // batch
    hps = MLA_HEADS_PER_STEP
    hg = MLA_HEADS // hps
    steps = batch * hg
    spec = lambda w: pl.BlockSpec((t, w * hps), lambda b, g: (b, g))
    cast_specs = [pl.BlockSpec((w.shape[0], w.shape[1] // steps, w.shape[2]),
                               lambda b, g: (0, b * hg + g, 0)) for w in cast_weights]
    outs = pl.pallas_call(
        _mla_kernel,
        grid=(batch, hg),
        in_specs=[spec(2 * LANES), spec(2 * LANES), spec(LANES)] + cast_specs,
        out_specs=[spec(LANES)] + cast_specs,
        out_shape=[jax.ShapeDtypeStruct((n, vb.shape[1]), BF16)]
                  + [jax.ShapeDtypeStruct(w.shape, BF16) for w in cast_weights],
        compiler_params=_cparams(2),
        name="mla_attn",
    )(qf, kf, vb, *cast_weights)
    return outs[0], outs[1:]


SCAN_PHASES = ROW_PHASES


def _gelu_tanh(x):
    c2 = 2.0 * math.sqrt(2.0 / math.pi) * LOG2E
    return x / (1.0 + jnp.exp2(x * (-c2 - (c2 * 0.044715) * (x * x))))


def _odd_kernel(x_ref, g0_ref, win_f32, cw_ref, cb_ref, gaw_f32, gab_ref, gxw_f32, gxb_ref,
                ap_ref, wout_f32, g1_ref, o_ref, xslab, hslab, hcar,
                win_ref, gaw_ref, gxw_ref, wout_ref):
    tm, d = x_ref.shape
    bw = d // LRU_BLOCKS
    n_slab = d // LANES
    assert cw_ref.shape[0] == CONV_W and CONV_W - 1 <= CARRY_ROWS
    _cast_once([(win_f32, win_ref), (gaw_f32, gaw_ref), (gxw_f32, gxw_ref), (wout_f32, wout_ref)])

    @pl.when(pl.program_id(1) == 0)
    def _():
        xslab[:, 0:CARRY_ROWS, :] = jnp.zeros((n_slab, CARRY_ROWS, LANES), F32)
        hcar[...] = jnp.zeros(hcar.shape, F32)

    x = x_ref[...]
    h = _rms(x, g0_ref[...]).astype(BF16)
    grp = tm // SCAN_PHASES
    slabs_per_block = bw // LANES
    rowg = lax.broadcasted_iota(jnp.int32, (grp, 1), 0)

    blocks = range(LRU_BLOCKS)
    csl = [slice(n * bw, (n + 1) * bw) for n in blocks]
    st = [dict() for _ in blocks]

    def projx(n):
        st[n]["xb"] = _dot(h, win_ref[:, csl[n]])

    def projy(n):
        st[n]["yb"] = _dot(h, win_ref[:, d + n * bw: d + (n + 1) * bw])

    def conv(n):
        xb = st[n].pop("xb")
        cols = []
        for k in range(slabs_per_block):
            sl = n * slabs_per_block + k
            ls = slice(n * bw + k * LANES, n * bw + (k + 1) * LANES)
            out = _causal_conv_phases(xb[:, k * LANES:(k + 1) * LANES], xslab, sl, cw_ref, ls,
                                      bias=cb_ref[:, ls])
            cols.append(jnp.concatenate(out, axis=0))
        st[n]["xc"] = jnp.concatenate(cols, axis=1)

    def gates(n):
        xcb = st[n]["xc"].astype(BF16)
        st[n]["ra"] = _dot(xcb, gaw_ref[n])
        st[n]["ia"] = _dot(xcb, gxw_ref[n])

    def recur(n):
        cs, xc = csl[n], st[n].pop("xc")
        r = _sigmoid(st[n].pop("ra") + gab_ref[:, cs])
        ig = _sigmoid(st[n].pop("ia") + gxb_ref[:, cs])
        a = jnp.exp2(r * ((-LRU_C * LOG2E) * _softplus(-ap_ref[:, cs])))
        om = 1.0 - a * a
        u = jnp.where(om > 0.0, om * lax.rsqrt(om), 0.0) * (ig * xc)

        ca, cu = a[0:grp], u[0:grp]
        la, lu = [ca], [cu]
        for j in range(1, SCAN_PHASES):
            aj, uj = a[j * grp:(j + 1) * grp], u[j * grp:(j + 1) * grp]
            cu = aj * cu + uj
            ca = aj * ca
            la.append(ca)
            lu.append(cu)
        ga, gu = ca, cu
        s = 1
        while s < grp:
            valid = rowg >= s
            gu = jnp.where(valid, ga * pltpu.roll(gu, s, axis=0) + gu, gu)
            ga = jnp.where(valid, ga * pltpu.roll(ga, s, axis=0), ga)
            s *= 2
        h_in = hcar[:, cs]
        h_end = ga * h_in + gu
        hcar[:, cs] = h_end[grp - 1:grp, :]
        h_prev = jnp.where(rowg == 0, h_in, pltpu.roll(h_end, 1, axis=0))
        slabs = range(n * slabs_per_block, (n + 1) * slabs_per_block)
        for j in range(SCAN_PHASES):
            hj = la[j] * h_prev + lu[j]
            for k, sl in enumerate(slabs):
                hslab[sl, pl.ds(j, grp, stride=SCAN_PHASES), :] = hj[:, k * LANES:(k + 1) * LANES]
        st[n]["hs"] = jnp.concatenate([hslab[sl] for sl in slabs], axis=1)

    def gate(n):
        st[n]["hg"] = (st[n].pop("hs") * _gelu_tanh(st[n].pop("yb"))).astype(BF16)

    def outp(n):
        st[n]["y"] = _dot(st[n].pop("hg"), wout_ref[csl[n], :])

    chain = (projx, conv, gates, recur, projy, gate, outp)
    for k in range(LRU_BLOCKS + len(chain) - 1):
        for s, stage in enumerate(chain):
            if 0 <= k - s < LRU_BLOCKS:
                stage(k - s)
    y = functools.reduce(lambda p, q: p + q, [st[n]["y"] for n in blocks])
    o_ref[...] = x + _rms(y, g1_ref[...])


def _odd_mixer(xf, g0, g1, w_in, conv_w, conv_b, gaw, gab, gxw, gxb, a_param, w_out, o, batch, tm):
    n, d = xf.shape
    t = n // batch
    nt = t // tm
    row = pl.BlockSpec((tm, d), lambda b, i: (b * nt + i, 0))
    vec = _const_spec((1, d))
    weights = (w_in, gaw, gxw, w_out)
    wspec = [_layer_spec(w.shape, o) for w in weights]
    return pl.pallas_call(
        _odd_kernel,
        grid=(batch, nt),
        in_specs=[row, vec, wspec[0], _const_spec(conv_w.shape), vec,
                  wspec[1], vec, wspec[2], vec, vec, wspec[3], vec],
        out_specs=row,
        out_shape=jax.ShapeDtypeStruct((n, d), F32),
        scratch_shapes=[pltpu.VMEM((d // LANES, tm + CARRY_ROWS, LANES), F32),
                        pltpu.VMEM((d // LANES, tm, LANES), F32),
                        pltpu.VMEM((1, d), F32)]
                       + [pltpu.VMEM(w.shape[1:], BF16) for w in weights],
        compiler_params=_cparams(2),
        name="odd_mixer",
    )(xf, g0, w_in, conv_w, conv_b, gaw, gab, gxw, gxb, a_param, w_out, g1)


XA_SPLIT = 4


def _round_robin(gens):
    live = list(gens)
    while live:
        live = [g for g in live if next(g, StopIteration) is not StopIteration]


def _cast_once(pairs, n_axes=2):
    first = pl.program_id(0) == 0
    for axis in range(1, n_axes):
        first = first & (pl.program_id(axis) == 0)

    @pl.when(first)
    def _():
        for src, dst in pairs:
            dst[...] = src[...].astype(BF16)


def _xattn_kernel(x_ref, g2_ref, wq_f32, kv_ref, wo_f32, g3_ref, o_ref, wq_ref, wo_ref):
    _cast_once([(wq_f32, wq_ref), (wo_f32, wo_ref)])
    tm = x_ref.shape[0]
    rs = tm // XA_SPLIT
    _round_robin([_xattn_stages(slice(k * rs, (k + 1) * rs), x_ref, None, g2_ref, wq_ref, kv_ref,
                                wo_ref, g3_ref, o_ref) for k in range(XA_SPLIT)])


def _outproj_xattn_kernel(x_ref, a_ref, b_ref, w_f32, g1_ref,
                          g2_ref, wq_f32, kv_ref, wo_f32, g3_ref, o_ref, wq_ref, wo_ref, w_ref):
    _cast_once([(wq_f32, wq_ref), (wo_f32, wo_ref), (w_f32, w_ref)])
    tm = x_ref.shape[0]
    rs = tm // XA_SPLIT
    _round_robin([_xattn_stages(slice(k * rs, (k + 1) * rs), x_ref, (a_ref, b_ref, w_ref, g1_ref),
                                g2_ref, wq_ref, kv_ref, wo_ref, g3_ref, o_ref)
                  for k in range(XA_SPLIT)])


def _xattn_stages(rows, x_ref, mix, g2_ref, wq_ref, kv_ref, wo_ref, g3_ref, o_ref):
    x = x_ref[rows, :]
    d = x.shape[1]
    hd = d // XA_HEADS
    if mix is not None:
        a_ref, b_ref, w_ref, g1_ref = mix
        ka = a_ref.shape[1]
        y = _dot(a_ref[rows, :], w_ref[0:ka, :]) + _dot(b_ref[rows, :], w_ref[ka:, :])
        yield
        x = x + _rms(y, g1_ref[...])
    h = (x * g2_ref[...]).astype(BF16)
    r = lax.rsqrt(jnp.mean(x * x, axis=-1, keepdims=True) + EPS)
    q = _dot(h, wq_ref[...])
    yield
    q = (q * (r * (hd ** -0.5 * LOG2E))).astype(BF16)
    s = [_dot_nt(q[:, hh * hd:(hh + 1) * hd], kv_ref[:, hh * hd:(hh + 1) * hd])
         for hh in range(XA_HEADS)]
    yield
    outs = []
    for hh in range(XA_HEADS):
        m = jnp.max(s[hh], axis=-1, keepdims=True)
        p = jnp.exp2(s[hh] - m)
        l = jnp.sum(p, axis=-1, keepdims=True)
        pv = _dot(p.astype(BF16), kv_ref[:, d + hh * hd: d + (hh + 1) * hd])
        outs.append((pv / l).astype(BF16))
    yield
    y = _dot(jnp.concatenate(outs, axis=1), wo_ref[...])
    yield
    o_ref[rows, :] = x + _rms(y, g3_ref[...])


def _xattn(xf, g2, g3, wq, kv_mem, wo, layer, batch, tm, mix=None):
    n, d = xf.shape
    t = n // batch
    nt = t // tm
    n_mem = kv_mem.shape[1] // batch
    row = lambda w: pl.BlockSpec((tm, w), lambda b, i: (b * nt + i, 0))
    vec = _const_spec((1, d))
    xa_specs = [vec, _layer_spec(wq.shape, layer),
                pl.BlockSpec((None, n_mem, 2 * d), lambda b, i: (layer, b, 0)),
                _layer_spec(wo.shape, layer), vec]
    xa_args = (g2, wq, kv_mem, wo, g3)
    scratch = [pltpu.VMEM(wq.shape[1:], BF16), pltpu.VMEM(wo.shape[1:], BF16)]
    if mix is None:
        body, specs, args, name = _xattn_kernel, [row(d)] + xa_specs, (xf,) + xa_args, "xattn"
    else:
        a, b, w_out, w_out_layer, g1 = mix
        body, name = _outproj_xattn_kernel, "outproj_xattn"
        specs = [row(d), row(a.shape[1]), row(b.shape[1]), _layer_spec(w_out.shape, w_out_layer),
                 vec] + xa_specs
        args = (xf, a, b, w_out, g1) + xa_args
        scratch.append(pltpu.VMEM(w_out.shape[1:], BF16))
    return pl.pallas_call(
        body,
        grid=(batch, nt),
        in_specs=specs,
        out_specs=row(d),
        out_shape=jax.ShapeDtypeStruct((n, d), F32),
        scratch_shapes=scratch,
        compiler_params=_cparams(2),
        name=name,
    )(*args)


FFN_TF = 256


def _ffn_kernel(x_ref, g4_ref, win_ref, wo_ref, g5_ref, o_ref, act_scr):
    dff = wo_ref.shape[0]
    x = x_ref[...]
    h = (x * g4_ref[...]).astype(BF16)
    r = lax.rsqrt(jnp.mean(x * x, axis=-1, keepdims=True) + EPS)
    for c in range(dff // FFN_TF):
        cols = slice(c * FFN_TF, (c + 1) * FFN_TF)
        gt = _dot(h, win_ref[:, cols]) * r
        up = _dot(h, win_ref[:, dff + c * FFN_TF: dff + (c + 1) * FFN_TF]) * r
        act_scr[:, cols] = (gt * _sigmoid(gt) * up).astype(BF16)
    half = x.shape[0] // 2
    for rows in (slice(0, half), slice(half, 2 * half)):
        y = _dot(act_scr[rows, :], wo_ref[...])
        o_ref[rows, :] = x[rows, :] + _rms(y, g5_ref[...])


def _ffn(xf, g4, g5, w_in, w_out, layer, tm):
    n, d = xf.shape
    dff = w_out.shape[1]
    row = pl.BlockSpec((tm, d), lambda i: (i, 0))
    vec = _const_spec((1, d))
    return pl.pallas_call(
        _ffn_kernel,
        grid=(n // tm,),
        in_specs=[row, vec, _layer_spec(w_in.shape, layer), _layer_spec(w_out.shape, layer), vec],
        out_specs=row,
        out_shape=jax.ShapeDtypeStruct((n, d), F32),
        scratch_shapes=[pltpu.VMEM((tm, dff), BF16)],
        compiler_params=_cparams(1),
        name="ffn",
    )(xf, g4, w_in, w_out, g5)


def _relayout_even(w_in, w_uq, w_ukv, a_log, dt_bias):
    d = w_in.shape[0]
    qkvz = 3 * GDN_HEADS * GDN_DK + GDN_HEADS * GDN_DK
    ab = 2 * GDN_HEADS
    lat = w_in.shape[1] - qkvz - ab - MLA_ROPE
    pad = LANES - MLA_ROPE - ab
    w_in = w_in.astype(BF16)
    w_in_re = jnp.concatenate(
        [w_in[:, :qkvz], w_in[:, qkvz + ab:qkvz + ab + lat], w_in[:, qkvz + ab + lat:],
         w_in[:, qkvz:qkvz + ab], jnp.zeros((d, pad), w_in.dtype)], axis=1)
    r = w_uq.shape[0]
    uq = w_uq.reshape(r, MLA_HEADS, MLA_NOPE + MLA_ROPE)
    pe = jnp.pad(uq[:, :, MLA_NOPE:], ((0, 0), (0, 0), (0, LANES - MLA_ROPE)))
    wuq_re = jnp.concatenate([uq[:, :, :MLA_NOPE].reshape(r, -1), pe.reshape(r, -1)],
                             axis=1).astype(BF16)
    ukv = w_ukv.reshape(w_ukv.shape[0], MLA_HEADS, -1)
    wukv_re = jnp.concatenate([ukv[:, :, :MLA_NOPE].reshape(r, -1),
                               ukv[:, :, MLA_NOPE:].reshape(r, -1)], axis=1).astype(BF16)
    alog_row = jnp.zeros((1, LANES), F32).at[0, G_LANE:G_LANE + GDN_HEADS].set(a_log)
    dtb_row = jnp.zeros((1, LANES), F32).at[0, G_LANE:G_LANE + GDN_HEADS].set(dt_bias)
    return w_in_re, wuq_re, wukv_re, alog_row, dtb_row


def kernel(x, mem, positions, norm_gains, mem_norm, e_w_in, e_conv_w, e_a_log, e_dt_bias, e_o_norm,
           e_q_norm, e_kv_norm, e_w_uq, e_w_ukv, e_w_out, o_w_in, o_conv_w, o_conv_b, o_gate_a_w,
           o_gate_a_b, o_gate_x_w, o_gate_x_b, o_a_param, o_w_out, xa_wq, xa_wkv, xa_wo, ffn_w_in,
           ffn_w_out):
    batch, t, d = x.shape
    n = batch * t
    depth = norm_gains.shape[0]
    assert t % (GDN_PIPE * SUPER) == 0 and t % MLA_TQ == 0 and MLA_TQ % CHUNK == 0
    assert all(t % tile == 0 for tile in ROW_TILE.values())
    assert d % (LRU_BLOCKS * LANES) == 0 and d % XA_HEADS == 0
    assert (GDN_DK, MLA_NOPE, 2 * MLA_ROPE) == (LANES,) * 3
    xf = x.reshape(n, d)
    gains = norm_gains.reshape(depth, -1, 1, d)

    cos_t, sin_t = _rope_tables(positions)
    kv_mem = _mem_kv(mem.reshape(-1, d), mem_norm, xa_wkv)
    ffn_w_b = None

    for layer in range(depth):
        g = gains[layer]
        mix = None
        if layer % 2 == 0:
            e = layer // 2
            w_in_re, wuq_re, wukv_re, alog_row, dtb_row = _relayout_even(
                e_w_in[e], e_w_uq[e], e_w_ukv[e], e_a_log[e], e_dt_bias[e])
            q, k, v, z, gb, qf, kf, vb = _even_inproj(
                xf, g[0], w_in_re, e_conv_w[e], alog_row, dtb_row,
                e_q_norm[e].reshape(1, -1), e_kv_norm[e].reshape(1, -1), wuq_re, wukv_re,
                cos_t, sin_t, batch, tm=ROW_TILE["even_inproj"])
            out_a = _gdn(q, k, v, z, gb, e_o_norm[e].reshape(1, -1), batch)
            if ffn_w_b is None:
                out_b, ffn_w_b = _mla(qf, kf, vb, batch, cast_weights=(ffn_w_in, ffn_w_out))
            else:
                out_b, _ = _mla(qf, kf, vb, batch)
            mix = (out_a, out_b, e_w_out, e, g[1])
        else:
            o = layer // 2
            xf = _odd_mixer(xf, g[0], g[1], o_w_in, o_conv_w[o],
                            o_conv_b[o].reshape(1, d), o_gate_a_w,
                            o_gate_a_b[o].reshape(1, d), o_gate_x_w,
                            o_gate_x_b[o].reshape(1, d), o_a_param[o].reshape(1, d),
                            o_w_out, o, batch, tm=ROW_TILE["odd_mixer"])
        xf = _xattn(xf, g[2], g[3], xa_wq, kv_mem, xa_wo, layer, batch, tm=ROW_TILE["xattn"],
                    mix=mix)
        xf = _ffn(xf, g[4], g[5], ffn_w_b[0], ffn_w_b[1], layer, tm=ROW_TILE["ffn"])
    return xf.reshape(batch, t, d)
```

```python
import functools
import math

import jax
import jax.numpy as jnp
from jax import lax
from jax.experimental import pallas as pl
from jax.experimental.pallas import tpu as pltpu

F32 = jnp.float32
BF16 = jnp.bfloat16

EPS = 1e-6
CHUNK = 64
GDN_HEADS = 4
GDN_DK = 128
MLA_HEADS = 4
MLA_NOPE = 128
MLA_ROPE = 64
MLA_SCALE = (MLA_NOPE + MLA_ROPE) ** -0.5
LOG2E = math.log2(math.e)
ROPE_BASE = 10000.0
LRU_BLOCKS = 4
LRU_C = 8.0
XA_HEADS = 4
CONV_W = 4

LANES = 128
CARRY_ROWS = 8
V7X_VMEM_LIMIT = 56 * 1024 * 1024

ROW_TILE = {"even_inproj": 512, "odd_mixer": 512, "xattn": 1024, "ffn": 1024}

NT_DIMS = (((1,), (1,)), ((), ()))
TN_DIMS = (((0,), (0,)), ((), ()))


def _cparams(n_axes):
    return pltpu.CompilerParams(
        dimension_semantics=("arbitrary",) * n_axes,
        vmem_limit_bytes=V7X_VMEM_LIMIT)


def _dot(a, b):
    return jnp.dot(a, b, preferred_element_type=F32)


def _dot_nt(a, b):
    return lax.dot_general(a, b, NT_DIMS, preferred_element_type=F32)


def _rms(xf, g):
    ms = jnp.mean(xf * xf, axis=-1, keepdims=True)
    return xf * lax.rsqrt(ms + EPS) * g


def _sigmoid(x):
    return 1.0 / (1.0 + jnp.exp(-x))


def _softplus(x):
    return jnp.maximum(x, 0.0) + jnp.log1p(jnp.exp(-jnp.abs(x)))


def _const_spec(shape):
    nd = len(shape)
    return pl.BlockSpec(shape, lambda *_: (0,) * nd)


def _layer_spec(shape, layer):
    nd = len(shape) - 1
    return pl.BlockSpec((None,) + tuple(shape[1:]), lambda *_: (layer,) + (0,) * nd,
                        pipeline_mode=pl.Buffered(1))


def _rope_kernel(pos_ref, inv_ref, cos_ref, sin_ref):
    half = MLA_ROPE // 2
    per_row = LANES // half
    tr = pos_ref.shape[0]
    ang = pos_ref[...].astype(F32) * inv_ref[...]
    group = lax.broadcasted_iota(jnp.int32, ang.shape, 1) // half
    for tbl, ref in ((jnp.cos(ang), cos_ref), (jnp.sin(ang), sin_ref)):
        rot = [tbl] + [pltpu.roll(tbl, half * k, axis=1) for k in range(1, per_row)]
        for j in range(per_row):
            rep = rot[(per_row - j) % per_row]
            for g in range(1, per_row):
                rep = jnp.where(group == g, rot[(g - j) % per_row], rep)
            ref[pl.ds(j, tr, stride=per_row), :] = rep


def _rope_tables(positions):
    n = positions.size
    half = MLA_ROPE // 2
    per_row = LANES // half
    inv_freq = ROPE_BASE ** (-jnp.arange(0, MLA_ROPE, 2, dtype=F32) / MLA_ROPE)
    inv_row = jnp.tile(inv_freq, per_row).reshape(1, LANES)
    pos_rep = jnp.repeat(positions.reshape(n // per_row, per_row), half, axis=1)
    rows = n // per_row
    tr = min(rows, 512)
    out_spec = pl.BlockSpec((tr * per_row, LANES), lambda i: (i, 0))
    return pl.pallas_call(
        _rope_kernel,
        grid=(rows // tr,),
        in_specs=[pl.BlockSpec((tr, LANES), lambda i: (i, 0)), _const_spec((1, LANES))],
        out_specs=[out_spec, out_spec],
        out_shape=[jax.ShapeDtypeStruct((n, LANES), F32)] * 2,
        compiler_params=_cparams(1),
        name="rope_tables",
    )(pos_rep, inv_row)


def _memkv_kernel(mem_ref, g_ref, w_ref, o_ref, wb_scr):
    @pl.when(pl.program_id(1) == 0)
    def _():
        wb_scr[...] = w_ref[...].astype(BF16)

    mn = _rms(mem_ref[...], g_ref[...]).astype(BF16)
    o_ref[...] = _dot(mn, wb_scr[...]).astype(BF16)


def _mem_kv(mem2d, mem_norm, wkv):
    depth, d, d2 = wkv.shape
    rows = mem2d.shape[0]
    tr = min(rows, 512)
    return pl.pallas_call(
        _memkv_kernel,
        grid=(depth, rows // tr),
        in_specs=[pl.BlockSpec((tr, d), lambda l, i: (i, 0)),
                  _const_spec((1, d)),
                  pl.BlockSpec((None, d, d2), lambda l, i: (l, 0, 0))],
        out_specs=pl.BlockSpec((None, tr, d2), lambda l, i: (l, i, 0)),
        out_shape=jax.ShapeDtypeStruct((depth, rows, d2), BF16),
        scratch_shapes=[pltpu.VMEM((d, d2), BF16)],
        compiler_params=_cparams(2),
        name="mem_kv",
    )(mem2d, mem_norm.reshape(1, d), wkv)


def _rope_tile(x, cos, sin, lane):
    half = MLA_ROPE // 2
    up = pltpu.roll(x, LANES - half, axis=1)
    dn = pltpu.roll(x, half, axis=1)
    rot = jnp.where(lane < half, -up, dn)
    return jnp.where(lane < MLA_ROPE, x * cos + rot * sin, 0.0)


ROW_PHASES = 8


def _causal_conv_phases(x, xslab, sl, cw_ref, ls, bias=None):
    assert cw_ref.shape[0] == CONV_W and CONV_W - 1 <= CARRY_ROWS
    tm = x.shape[0]
    grp = tm // ROW_PHASES
    xslab[sl, CARRY_ROWS:CARRY_ROWS + tm, :] = x
    ph = [xslab[sl, pl.ds(CARRY_ROWS + j, grp, stride=ROW_PHASES), :]
          for j in range(-(CONV_W - 1), ROW_PHASES)]
    xslab[sl, 0:CARRY_ROWS, :] = x[tm - CARRY_ROWS:, :]
    w = [cw_ref[i:i + 1, ls] for i in range(CONV_W)]
    out = []
    for j in range(ROW_PHASES):
        acc = w[CONV_W - 1] * ph[j + CONV_W - 1]
        if bias is not None:
            acc = acc + bias
        for i in range(CONV_W - 1):
            acc = acc + w[i] * ph[j + i]
        out.append(acc)
    return out


def _even_in_kernel(x_ref, g_ref, win_ref, cw_ref, alog_ref, dtb_ref, qn_ref, kvn_ref,
                    wuq_ref, wukv_ref, cos_ref, sin_ref,
                    q_out, k_out, v_out, z_out, gb_out, qf_out, kf_out, vb_out,
                    xslab, yslab):
    tm = x_ref.shape[0]
    qk_w = GDN_HEADS * GDN_DK
    qkv_w = 3 * qk_w

    @pl.when(pl.program_id(1) == 0)
    def _():
        xslab[:, 0:CARRY_ROWS, :] = jnp.zeros((qkv_w // LANES, CARRY_ROWS, LANES), F32)

    h = _rms(x_ref[...], g_ref[...]).astype(BF16)
    lane = lax.broadcasted_iota(jnp.int32, (tm, LANES), 1)
    cos = cos_ref[...]
    sin = sin_ref[...]
    nope_w = MLA_HEADS * MLA_NOPE
    z_off = qkv_w
    cq_off = z_off + qk_w
    ckv_off = cq_off + qn_ref.shape[1]
    misc_off = ckv_off + kvn_ref.shape[1]
    gw = 2 * GDN_DK
    st = {}

    def proj_cols(lo, width):
        return _dot(h, win_ref[:, lo:lo + width])

    def mm_cq():
        st["cq"] = proj_cols(cq_off, qn_ref.shape[1])

    def ep_cq():
        st["cqb"] = _rms(st.pop("cq"), qn_ref[...] * (MLA_SCALE * LOG2E)).astype(BF16)

    def mm_ckv():
        st["ckv"] = proj_cols(ckv_off, kvn_ref.shape[1])

    def ep_ckv():
        st["ckvb"] = _rms(st.pop("ckv"), kvn_ref[...]).astype(BF16)

    def mm_misc():
        st["misc"] = proj_cols(misc_off, LANES)

    def ep_misc():
        misc = st.pop("misc")
        gdec = -jnp.exp(alog_ref[...]) * _softplus(misc + dtb_ref[...])
        beta = _sigmoid(misc)
        gb_out[...] = jnp.where((lane >= G_LANE) & (lane < G_LANE + GDN_HEADS), gdec,
                                jnp.where((lane >= B_LANE) & (lane < B_LANE + GDN_HEADS),
                                          beta, 0.0))
        st["kpe"] = _rope_tile(misc, cos, sin, lane).astype(BF16)

    def mm_uq():
        st["qf"] = _dot(st.pop("cqb"), wuq_ref[...])

    def ep_uq():
        qf = st.pop("qf")
        for hh in range(MLA_HEADS):
            base = 2 * LANES * hh
            qf_out[:, base:base + LANES] = qf[:, hh * LANES:(hh + 1) * LANES].astype(BF16)
            pe = qf[:, nope_w + hh * LANES: nope_w + (hh + 1) * LANES]
            qf_out[:, base + LANES:base + 2 * LANES] = _rope_tile(pe, cos, sin, lane).astype(BF16)

    def mm_ukv():
        st["kv"] = _dot(st.pop("ckvb"), wukv_ref[...])

    def ep_ukv():
        kv = st.pop("kv")
        for hh in range(MLA_HEADS):
            base = 2 * LANES * hh
            kf_out[:, base:base + LANES] = kv[:, hh * LANES:(hh + 1) * LANES].astype(BF16)
            kf_out[:, base + LANES:base + 2 * LANES] = st["kpe"]
        vb_out[...] = kv[:, nope_w:].astype(BF16)

    def mm_group(lo):
        def run():
            st[lo] = proj_cols(lo, gw)
        return run

    def ep_qkv(lo):
        def run():
            raw = st.pop(lo)
            which, off = divmod(lo, qk_w)
            out, scale = ((q_out, GDN_DK ** -0.5), (k_out, 1.0), (v_out, None))[which]
            grp = tm // ROW_PHASES
            for sub in range(gw // LANES):
                sl = lo // LANES + sub
                ls = slice(lo + sub * LANES, lo + (sub + 1) * LANES)
                blocks = _causal_conv_phases(raw[:, sub * LANES:(sub + 1) * LANES], xslab, sl,
                                             cw_ref, ls)
                for j, acc in enumerate(blocks):
                    a = acc * _sigmoid(acc)
                    if scale is not None:
                        a = a * (lax.rsqrt(jnp.sum(a * a, axis=-1, keepdims=True) + EPS) * scale)
                    yslab[sl, pl.ds(j, grp, stride=ROW_PHASES), :] = a
                out[:, off + sub * LANES: off + (sub + 1) * LANES] = yslab[sl].astype(BF16)
        return run

    def ep_z(lo):
        def run():
            z_out[:, lo - z_off: lo - z_off + gw] = st.pop(lo).astype(BF16)
        return run

    pairs = [(mm_group(lo), ep_qkv(lo)) for lo in range(0, qkv_w, gw)]
    pairs += [(mm_cq, ep_cq), (mm_ckv, ep_ckv), (mm_misc, ep_misc), (mm_uq, ep_uq), (mm_ukv, ep_ukv)]
    pairs += [(mm_group(lo), ep_z(lo)) for lo in range(z_off, cq_off, gw)]
    for mm, ep in pairs:
        mm()
        ep()


def _even_inproj(xf, g0, w_in_re, conv_w, alog_row, dtb_row, q_norm, kv_norm, wuq_re, wukv_re,
                 cos_t, sin_t, batch, tm):
    n, d = xf.shape
    t = n // batch
    nt = t // tm
    e_in = w_in_re.shape[1]
    qkv_w = conv_w.shape[1]
    qk_w = qkv_w // 3
    row = lambda w: pl.BlockSpec((tm, w), lambda b, i: (b * nt + i, 0))
    outs = [(qk_w, BF16)] * 4 + [(LANES, F32), (2 * LANES * MLA_HEADS, BF16),
                                 (2 * LANES * MLA_HEADS, BF16), (LANES * MLA_HEADS, BF16)]
    return pl.pallas_call(
        _even_in_kernel,
        grid=(batch, nt),
        in_specs=[row(d), _const_spec((1, d)), _const_spec((d, e_in)), _const_spec((CONV_W, qkv_w)),
                  _const_spec((1, LANES)), _const_spec((1, LANES)),
                  _const_spec(q_norm.shape), _const_spec(kv_norm.shape),
                  _const_spec(wuq_re.shape), _const_spec(wukv_re.shape),
                  row(LANES), row(LANES)],
        out_specs=[row(w) for w, _ in outs],
        out_shape=[jax.ShapeDtypeStruct((n, w), dt) for w, dt in outs],
        scratch_shapes=[pltpu.VMEM((qkv_w // LANES, tm + CARRY_ROWS, LANES), F32),
                        pltpu.VMEM((qkv_w // LANES, tm, LANES), F32)],
        compiler_params=_cparams(2),
        name="even_inproj",
    )(xf, g0, w_in_re, conv_w, alog_row, dtb_row, q_norm, kv_norm, wuq_re, wukv_re, cos_t, sin_t)


SUPER = 2 * CHUNK
G_LANE = MLA_ROPE
B_LANE = MLA_ROPE + GDN_HEADS
GDN_PIPE = 2


def _gdn_kernel(q_ref, k_ref, v_ref, z_ref, gb_ref, on_ref, out_ref,
                gc_scr, gct_scr, gl_scr, u_scr, w_scr, a_scr, qe_scr, kd_scr, s_scr):
    t = q_ref.shape[0]
    n_super = t // SUPER
    n_chunk = t // CHUNK

    gb = gb_ref[...]
    rowi = lax.broadcasted_iota(jnp.int32, gb.shape, 0) & (CHUNK - 1)
    gc = gb
    s = 1
    while s < CHUNK:
        gc = jnp.where(rowi >= s, gc + pltpu.roll(gc, s, axis=0), gc)
        s *= 2
    gc_scr[...] = gc
    gct_scr[...] = gc.T
    g3 = gc.reshape(n_chunk, CHUNK, LANES)
    gl_scr[...] = jnp.broadcast_to(g3[:, CHUNK - 1:CHUNK, :], g3.shape).reshape(t, LANES)

    ri = lax.broadcasted_iota(jnp.int32, (SUPER, SUPER), 0)
    ci = lax.broadcasted_iota(jnp.int32, (SUPER, SUPER), 1)
    same = (ri >= CHUNK) == (ci >= CHUNK)
    causal = same & (ri >= ci)
    strict = same & (ri > ci)
    eye = (ri == ci).astype(F32)

    on = on_ref[...]
    heads = range(GDN_HEADS)
    hsl = [slice(hh * GDN_DK, (hh + 1) * GDN_DK) for hh in heads]
    lane_g = [slice(G_LANE + hh, G_LANE + hh + 1) for hh in heads]
    lane_b = [slice(B_LANE + hh, B_LANE + hh + 1) for hh in heads]

    def solve_stages(sc0):
        probs = []
        for sub in range(GDN_PIPE):
            rows = pl.ds(pl.multiple_of((sc0 + sub) * SUPER, SUPER), SUPER)
            probs += [(rows, hh) for hh in heads]
        kc = [k_ref[rows, hsl[hh]] for rows, hh in probs]
        qc = [q_ref[rows, hsl[hh]] for rows, hh in probs]
        gcol = [gc_scr[rows, lane_g[hh]] for rows, hh in probs]
        bcol = [gb_ref[rows, lane_b[hh]] for rows, hh in probs]
        grow = [gct_scr[lane_g[hh], rows] for rows, hh in probs]
        kk = [_dot_nt(k, k) for k in kc]
        qk = [_dot_nt(q, k) for q, k in zip(qc, kc)]
        yield
        decay = [jnp.where(causal, jnp.exp(jnp.where(causal, gc - gr, 0.0)), 0.0)
                 for gc, gr in zip(gcol, grow)]
        m = [jnp.where(strict, -(x * b * dc), 0.0) for x, b, dc in zip(kk, bcol, decay)]
        qacc = [eye + x for x in m]
        mb = [x.astype(BF16) for x in m]
        mj = [_dot(x, x) for x in mb]
        yield
        lvl = 2
        while lvl < CHUNK // 2:
            mb = [x.astype(BF16) for x in mj]
            r = [_dot(x, jnp.concatenate([x, qa.astype(BF16)], axis=1)) for x, qa in zip(mb, qacc)]
            yield
            mj = [x[:, :SUPER] for x in r]
            qacc = [qa + x[:, SUPER:] for qa, x in zip(qacc, r)]
            lvl *= 2
        last = [_dot(x.astype(BF16), qa.astype(BF16)) for qa, x in zip(qacc, mj)]
        yield
        qacc = [qa + x for qa, x in zip(qacc, last)]
        eg = [jnp.exp(gc) for gc in gcol]
        sol = []
        for i, (rows, hh) in enumerate(probs):
            vc = v_ref[rows, hsl[hh]].astype(F32)
            kf = kc[i].astype(F32)
            rhs = jnp.concatenate([vc * bcol[i], kf * (bcol[i] * eg[i])], axis=1).astype(BF16)
            sol.append(_dot(qacc[i].astype(BF16), rhs))
        yield
        for i, (rows, hh) in enumerate(probs):
            u_scr[rows, hsl[hh]] = sol[i][:, :GDN_DK]
            w_scr[rows, hsl[hh]] = sol[i][:, GDN_DK:].astype(BF16)
            aqk = qk[i] * decay[i]
            a_sh = pltpu.roll(aqk, CHUNK, axis=1)
            a_scr[rows, hsl[hh]] = jnp.where(ri < CHUNK, aqk, a_sh).astype(BF16)
            qe_scr[rows, hsl[hh]] = (qc[i].astype(F32) * eg[i]).astype(BF16)
            glast = gl_scr[rows, lane_g[hh]]
            kd_scr[rows, hsl[hh]] = (kc[i].astype(F32) * jnp.exp(glast - gcol[i])).astype(BF16)

    def scan_stages(sc0, st):
        for half in range(GDN_PIPE * SUPER // CHUNK):
            r0 = pl.multiple_of(sc0 * SUPER + half * CHUNK, CHUNK)
            rows = pl.ds(r0, CHUNK)
            sb = [x.astype(BF16) for x in st]
            t1 = [_dot(w_scr[rows, hsl[hh]], sb[hh]) for hh in heads]
            oq = [_dot(qe_scr[rows, hsl[hh]], sb[hh]) for hh in heads]
            yield
            vb = [(u_scr[rows, hsl[hh]] - t1[hh]).astype(BF16) for hh in heads]
            oa = [_dot(a_scr[rows, hsl[hh]][:, :CHUNK], vb[hh]) for hh in heads]
            kv = [lax.dot_general(kd_scr[rows, hsl[hh]], vb[hh], TN_DIMS,
                                  preferred_element_type=F32) for hh in heads]
            yield
            for hh in heads:
                gam = jnp.exp(gl_scr[pl.ds(r0, 1), lane_g[hh]])
                st[hh] = st[hh] * gam + kv[hh]
                z = z_ref[rows, hsl[hh]].astype(F32)
                out_ref[rows, hsl[hh]] = (
                    _rms(oq[hh] + oa[hh], on) * (z * _sigmoid(z))).astype(BF16)

    _round_robin([solve_stages(0)])
    s_scr[...] = jnp.zeros(s_scr.shape, F32)
    n_steps = n_super // GDN_PIPE

    def pipe_body(it, _):
        st = [s_scr[hh] for hh in heads]
        _round_robin([solve_stages((it + 1) * GDN_PIPE), scan_stages(it * GDN_PIPE, st)])
        for hh in heads:
            s_scr[hh] = st[hh]
        return 0

    lax.fori_loop(0, n_steps - 1, pipe_body, 0)
    st = [s_scr[hh] for hh in heads]
    _round_robin([scan_stages((n_steps - 1) * GDN_PIPE, st)])


def _gdn(q, k, v, z, gb, o_norm, batch):
    n, w = q.shape
    t = n // batch
    row = lambda ww: pl.BlockSpec((t, ww), lambda b: (b, 0))
    return pl.pallas_call(
        _gdn_kernel,
        grid=(batch,),
        in_specs=[row(w), row(w), row(w), row(w), row(LANES), _const_spec((1, GDN_DK))],
        out_specs=row(w),
        out_shape=jax.ShapeDtypeStruct((n, w), BF16),
        scratch_shapes=[pltpu.VMEM((t, LANES), F32),
                        pltpu.VMEM((LANES, t), F32),
                        pltpu.VMEM((t, LANES), F32),
                        pltpu.VMEM((t, w), F32),
                        pltpu.VMEM((t, w), BF16),
                        pltpu.VMEM((t, w), BF16),
                        pltpu.VMEM((t, w), BF16),
                        pltpu.VMEM((t, w), BF16),
                        pltpu.VMEM((GDN_HEADS, GDN_DK, GDN_DK), F32)],
        compiler_params=_cparams(1),
        name="gdn",
    )(q, k, v, z, gb, o_norm)


MLA_TQ = 256


def _mla_kernel(q_ref, k_ref, v_ref, *rest):
    n_cast = (len(rest) - 1) // 2
    o_ref = rest[n_cast]
    for src, dst in zip(rest[:n_cast], rest[n_cast + 1:]):
        dst[...] = src[...].astype(BF16)
    t = q_ref.shape[0]
    tq = MLA_TQ
    nq = t // tq
    hw = 2 * LANES
    heads = range(q_ref.shape[1] // hw)
    ri = lax.broadcasted_iota(jnp.int32, (tq, tq), 0) // CHUNK
    ci = lax.broadcasted_iota(jnp.int32, (tq, tq), 1) // CHUNK
    diag_mask = ci <= ri

    groups = [(i, h) for i in range(nq) for h in heads]

    def scores(i, h):
        kw = (i + 1) * tq
        return _dot_nt(q_ref[i * tq:(i + 1) * tq, h * hw:(h + 1) * hw],
                       k_ref[0:kw, h * hw:(h + 1) * hw])

    s_next = scores(*groups[0])
    for g, (i, h) in enumerate(groups):
        s = s_next
        if g + 1 < len(groups):
            s_next = scores(*groups[g + 1])
        kw = (i + 1) * tq
        s_diag = jnp.where(diag_mask, s[:, kw - tq:], -jnp.inf)
        s = s_diag if i == 0 else jnp.concatenate([s[:, :kw - tq], s_diag], axis=1)
        m = jnp.max(s, axis=-1, keepdims=True)
        p = jnp.exp2(s - m)
        l = jnp.sum(p, axis=-1, keepdims=True)
        pv = _dot(p.astype(BF16), v_ref[0:kw, h * LANES:(h + 1) * LANES])
        o_ref[i * tq:(i + 1) * tq, h * LANES:(h + 1) * LANES] = (pv / l).astype(BF16)


MLA_HEADS_PER_STEP = 2


def _mla(qf, kf, vb, batch, cast_weights=()):
    n = qf.shape[0]
    t = n // batch
    hps = MLA_HEADS_PER_STEP
    hg = MLA_HEADS // hps
    steps = batch * hg
    spec = lambda w: pl.BlockSpec((t, w * hps), lambda b, g: (b, g))
    cast_specs = [pl.BlockSpec((w.shape[0], w.shape[1] // steps, w.shape[2]),
                               lambda b, g: (0, b * hg + g, 0)) for w in cast_weights]
    outs = pl.pallas_call(
        _mla_kernel,
        grid=(batch, hg),
        in_specs=[spec(2 * LANES), spec(2 * LANES), spec(LANES)] + cast_specs,
        out_specs=[spec(LANES)] + cast_specs,
        out_shape=[jax.ShapeDtypeStruct((n, vb.shape[1]), BF16)]
                  + [jax.ShapeDtypeStruct(w.shape, BF16) for w in cast_weights],
        compiler_params=_cparams(2),
        name="mla_attn",
    )(qf, kf, vb, *cast_weights)
    return outs[0], outs[1:]


SCAN_PHASES = ROW_PHASES


def _gelu_tanh(x):
    c2 = 2.0 * math.sqrt(2.0 / math.pi) * LOG2E
    return x / (1.0 + jnp.exp2(x * (-c2 - (c2 * 0.044715) * (x * x))))


def _odd_kernel(x_ref, g0_ref, win_f32, cw_ref, cb_ref, gaw_f32, gab_ref, gxw_f32, gxb_ref,
                ap_ref, wout_f32, g1_ref, o_ref, xslab, hslab, hcar,
                win_ref, gaw_ref, gxw_ref, wout_ref):
    tm, d = x_ref.shape
    bw = d // LRU_BLOCKS
    n_slab = d // LANES
    assert cw_ref.shape[0] == CONV_W and CONV_W - 1 <= CARRY_ROWS
    _cast_once([(win_f32, win_ref), (gaw_f32, gaw_ref), (gxw_f32, gxw_ref), (wout_f32, wout_ref)])

    @pl.when(pl.program_id(1) == 0)
    def _():
        xslab[:, 0:CARRY_ROWS, :] = jnp.zeros((n_slab, CARRY_ROWS, LANES), F32)
        hcar[...] = jnp.zeros(hcar.shape, F32)

    x = x_ref[...]
    h = _rms(x, g0_ref[...]).astype(BF16)
    grp = tm // SCAN_PHASES
    slabs_per_block = bw // LANES
    rowg = lax.broadcasted_iota(jnp.int32, (grp, 1), 0)

    blocks = range(LRU_BLOCKS)
    csl = [slice(n * bw, (n + 1) * bw) for n in blocks]
    st = [dict() for _ in blocks]

    def projx(n):
        st[n]["xb"] = _dot(h, win_ref[:, csl[n]])

    def projy(n):
        st[n]["yb"] = _dot(h, win_ref[:, d + n * bw: d + (n + 1) * bw])

    def conv(n):
        xb = st[n].pop("xb")
        cols = []
        for k in range(slabs_per_block):
            sl = n * slabs_per_block + k
            ls = slice(n * bw + k * LANES, n * bw + (k + 1) * LANES)
            out = _causal_conv_phases(xb[:, k * LANES:(k + 1) * LANES], xslab, sl, cw_ref, ls,
                                      bias=cb_ref[:, ls])
            cols.append(jnp.concatenate(out, axis=0))
        st[n]["xc"] = jnp.concatenate(cols, axis=1)

    def gates(n):
        xcb = st[n]["xc"].astype(BF16)
        st[n]["ra"] = _dot(xcb, gaw_ref[n])
        st[n]["ia"] = _dot(xcb, gxw_ref[n])

    def recur(n):
        cs, xc = csl[n], st[n].pop("xc")
        r = _sigmoid(st[n].pop("ra") + gab_ref[:, cs])
        ig = _sigmoid(st[n].pop("ia") + gxb_ref[:, cs])
        a = jnp.exp2(r * ((-LRU_C * LOG2E) * _softplus(-ap_ref[:, cs])))
        om = 1.0 - a * a
        u = jnp.where(om > 0.0, om * lax.rsqrt(om), 0.0) * (ig * xc)

        ca, cu = a[0:grp], u[0:grp]
        la, lu = [ca], [cu]
        for j in range(1, SCAN_PHASES):
            aj, uj = a[j * grp:(j + 1) * grp], u[j * grp:(j + 1) * grp]
            cu = aj * cu + uj
            ca = aj * ca
            la.append(ca)
            lu.append(cu)
        ga, gu = ca, cu
        s = 1
        while s < grp:
            valid = rowg >= s
            gu = jnp.where(valid, ga * pltpu.roll(gu, s, axis=0) + gu, gu)
            ga = jnp.where(valid, ga * pltpu.roll(ga, s, axis=0), ga)
            s *= 2
        h_in = hcar[:, cs]
        h_end = ga * h_in + gu
        hcar[:, cs] = h_end[grp - 1:grp, :]
        h_prev = jnp.where(rowg == 0, h_in, pltpu.roll(h_end, 1, axis=0))
        slabs = range(n * slabs_per_block, (n + 1) * slabs_per_block)
        for j in range(SCAN_PHASES):
            hj = la[j] * h_prev + lu[j]
            for k, sl in enumerate(slabs):
                hslab[sl, pl.ds(j, grp, stride=SCAN_PHASES), :] = hj[:, k * LANES:(k + 1) * LANES]
        st[n]["hs"] = jnp.concatenate([hslab[sl] for sl in slabs], axis=1)

    def gate(n):
        st[n]["hg"] = (st[n].pop("hs") * _gelu_tanh(st[n].pop("yb"))).astype(BF16)

    def outp(n):
        st[n]["y"] = _dot(st[n].pop("hg"), wout_ref[csl[n], :])

    chain = (projx, conv, gates, recur, projy, gate, outp)
    for k in range(LRU_BLOCKS + len(chain) - 1):
        for s, stage in enumerate(chain):
            if 0 <= k - s < LRU_BLOCKS:
                stage(k - s)
    y = functools.reduce(lambda p, q: p + q, [st[n]["y"] for n in blocks])
    o_ref[...] = x + _rms(y, g1_ref[...])


def _odd_mixer(xf, g0, g1, w_in, conv_w, conv_b, gaw, gab, gxw, gxb, a_param, w_out, o, batch, tm):
    n, d = xf.shape
    t = n // batch
    nt = t // tm
    row = pl.BlockSpec((tm, d), lambda b, i: (b * nt + i, 0))
    vec = _const_spec((1, d))
    weights = (w_in, gaw, gxw, w_out)
    wspec = [_layer_spec(w.shape, o) for w in weights]
    return pl.pallas_call(
        _odd_kernel,
        grid=(batch, nt),
        in_specs=[row, vec, wspec[0], _const_spec(conv_w.shape), vec,
                  wspec[1], vec, wspec[2], vec, vec, wspec[3], vec],
        out_specs=row,
        out_shape=jax.ShapeDtypeStruct((n, d), F32),
        scratch_shapes=[pltpu.VMEM((d // LANES, tm + CARRY_ROWS, LANES), F32),
                        pltpu.VMEM((d // LANES, tm, LANES), F32),
                        pltpu.VMEM((1, d), F32)]
                       + [pltpu.VMEM(w.shape[1:], BF16) for w in weights],
        compiler_params=_cparams(2),
        name="odd_mixer",
    )(xf, g0, w_in, conv_w, conv_b, gaw, gab, gxw, gxb, a_param, w_out, g1)


XA_SPLIT = 4


def _round_robin(gens):
    live = list(gens)
    while live:
        live = [g for g in live if next(g, StopIteration) is not StopIteration]


def _cast_once(pairs, n_axes=2):
    first = pl.program_id(0) == 0
    for axis in range(1, n_axes):
        first = first & (pl.program_id(axis) == 0)

    @pl.when(first)
    def _():
        for src, dst in pairs:
            dst[...] = src[...].astype(BF16)


def _xattn_kernel(x_ref, g2_ref, wq_f32, kv_ref, wo_f32, g3_ref, o_ref, wq_ref, wo_ref):
    _cast_once([(wq_f32, wq_ref), (wo_f32, wo_ref)])
    tm = x_ref.shape[0]
    rs = tm // XA_SPLIT
    _round_robin([_xattn_stages(slice(k * rs, (k + 1) * rs), x_ref, None, g2_ref, wq_ref, kv_ref,
                                wo_ref, g3_ref, o_ref) for k in range(XA_SPLIT)])


def _outproj_xattn_kernel(x_ref, a_ref, b_ref, w_f32, g1_ref,
                          g2_ref, wq_f32, kv_ref, wo_f32, g3_ref, o_ref, wq_ref, wo_ref, w_ref):
    _cast_once([(wq_f32, wq_ref), (wo_f32, wo_ref), (w_f32, w_ref)])
    tm = x_ref.shape[0]
    rs = tm // XA_SPLIT
    _round_robin([_xattn_stages(slice(k * rs, (k + 1) * rs), x_ref, (a_ref, b_ref, w_ref, g1_ref),
                                g2_ref, wq_ref, kv_ref, wo_ref, g3_ref, o_ref)
                  for k in range(XA_SPLIT)])


def _xattn_stages(rows, x_ref, mix, g2_ref, wq_ref, kv_ref, wo_ref, g3_ref, o_ref):
    x = x_ref[rows, :]
    d = x.shape[1]
    hd = d // XA_HEADS
    if mix is not None:
        a_ref, b_ref, w_ref, g1_ref = mix
        ka = a_ref.shape[1]
        y = _dot(a_ref[rows, :], w_ref[0:ka, :]) + _dot(b_ref[rows, :], w_ref[ka:, :])
        yield
        x = x + _rms(y, g1_ref[...])
    h = (x * g2_ref[...]).astype(BF16)
    r = lax.rsqrt(jnp.mean(x * x, axis=-1, keepdims=True) + EPS)
    q = _dot(h, wq_ref[...])
    yield
    q = (q * (r * (hd ** -0.5 * LOG2E))).astype(BF16)
    s = [_dot_nt(q[:, hh * hd:(hh + 1) * hd], kv_ref[:, hh * hd:(hh + 1) * hd])
         for hh in range(XA_HEADS)]
    yield
    outs = []
    for hh in range(XA_HEADS):
        m = jnp.max(s[hh], axis=-1, keepdims=True)
        p = jnp.exp2(s[hh] - m)
        l = jnp.sum(p, axis=-1, keepdims=True)
        pv = _dot(p.astype(BF16), kv_ref[:, d + hh * hd: d + (hh + 1) * hd])
        outs.append((pv / l).astype(BF16))
    yield
    y = _dot(jnp.concatenate(outs, axis=1), wo_ref[...])
    yield
    o_ref[rows, :] = x + _rms(y, g3_ref[...])


def _xattn(xf, g2, g3, wq, kv_mem, wo, layer, batch, tm, mix=None):
    n, d = xf.shape
    t = n // batch
    nt = t // tm
    n_mem = kv_mem.shape[1] // batch
    row = lambda w: pl.BlockSpec((tm, w), lambda b, i: (b * nt + i, 0))
    vec = _const_spec((1, d))
    xa_specs = [vec, _layer_spec(wq.shape, layer),
                pl.BlockSpec((None, n_mem, 2 * d), lambda b, i: (layer, b, 0)),
                _layer_spec(wo.shape, layer), vec]
    xa_args = (g2, wq, kv_mem, wo, g3)
    scratch = [pltpu.VMEM(wq.shape[1:], BF16), pltpu.VMEM(wo.shape[1:], BF16)]
    if mix is None:
        body, specs, args, name = _xattn_kernel, [row(d)] + xa_specs, (xf,) + xa_args, "xattn"
    else:
        a, b, w_out, w_out_layer, g1 = mix
        body, name = _outproj_xattn_kernel, "outproj_xattn"
        specs = [row(d), row(a.shape[1]), row(b.shape[1]), _layer_spec(w_out.shape, w_out_layer),
                 vec] + xa_specs
        args = (xf, a, b, w_out, g1) + xa_args
        scratch.append(pltpu.VMEM(w_out.shape[1:], BF16))
    return pl.pallas_call(
        body,
        grid=(batch, nt),
        in_specs=specs,
        out_specs=row(d),
        out_shape=jax.ShapeDtypeStruct((n, d), F32),
        scratch_shapes=scratch,
        compiler_params=_cparams(2),
        name=name,
    )(*args)


FFN_TF = 256


def _ffn_kernel(x_ref, g4_ref, win_ref, wo_ref, g5_ref, o_ref, act_scr):
    dff = wo_ref.shape[0]
    x = x_ref[...]
    h = (x * g4_ref[...]).astype(BF16)
    r = lax.rsqrt(jnp.mean(x * x, axis=-1, keepdims=True) + EPS)
    for c in range(dff // FFN_TF):
        cols = slice(c * FFN_TF, (c + 1) * FFN_TF)
        gt = _dot(h, win_ref[:, cols]) * r
        up = _dot(h, win_ref[:, dff + c * FFN_TF: dff + (c + 1) * FFN_TF]) * r
        act_scr[:, cols] = (gt * _sigmoid(gt) * up).astype(BF16)
    half = x.shape[0] // 2
    for rows in (slice(0, half), slice(half, 2 * half)):
        y = _dot(act_scr[rows, :], wo_ref[...])
        o_ref[rows, :] = x[rows, :] + _rms(y, g5_ref[...])


def _ffn(xf, g4, g5, w_in, w_out, layer, tm):
    n, d = xf.shape
    dff = w_out.shape[1]
    row = pl.BlockSpec((tm, d), lambda i: (i, 0))
    vec = _const_spec((1, d))
    return pl.pallas_call(
        _ffn_kernel,
        grid=(n // tm,),
        in_specs=[row, vec, _layer_spec(w_in.shape, layer), _layer_spec(w_out.shape, layer), vec],
        out_specs=row,
        out_shape=jax.ShapeDtypeStruct((n, d), F32),
        scratch_shapes=[pltpu.VMEM((tm, dff), BF16)],
        compiler_params=_cparams(1),
        name="ffn",
    )(xf, g4, w_in, w_out, g5)


def _relayout_even(w_in, w_uq, w_ukv, a_log, dt_bias):
    d = w_in.shape[0]
    qkvz = 3 * GDN_HEADS * GDN_DK + GDN_HEADS * GDN_DK
    ab = 2 * GDN_HEADS
    lat = w_in.shape[1] - qkvz - ab - MLA_ROPE
    pad = LANES - MLA_ROPE - ab
    w_in = w_in.astype(BF16)
    w_in_re = jnp.concatenate(
        [w_in[:, :qkvz], w_in[:, qkvz + ab:qkvz + ab + lat], w_in[:, qkvz + ab + lat:],
         w_in[:, qkvz:qkvz + ab], jnp.zeros((d, pad), w_in.dtype)], axis=1)
    r = w_uq.shape[0]
    uq = w_uq.reshape(r, MLA_HEADS, MLA_NOPE + MLA_ROPE)
    pe = jnp.pad(uq[:, :, MLA_NOPE:], ((0, 0), (0, 0), (0, LANES - MLA_ROPE)))
    wuq_re = jnp.concatenate([uq[:, :, :MLA_NOPE].reshape(r, -1), pe.reshape(r, -1)],
                             axis=1).astype(BF16)
    ukv = w_ukv.reshape(w_ukv.shape[0], MLA_HEADS, -1)
    wukv_re = jnp.concatenate([ukv[:, :, :MLA_NOPE].reshape(r, -1),
                               ukv[:, :, MLA_NOPE:].reshape(r, -1)], axis=1).astype(BF16)
    alog_row = jnp.zeros((1, LANES), F32).at[0, G_LANE:G_LANE + GDN_HEADS].set(a_log)
    dtb_row = jnp.zeros((1, LANES), F32).at[0, G_LANE:G_LANE + GDN_HEADS].set(dt_bias)
    return w_in_re, wuq_re, wukv_re, alog_row, dtb_row


def kernel(x, mem, positions, norm_gains, mem_norm, e_w_in, e_conv_w, e_a_log, e_dt_bias, e_o_norm,
           e_q_norm, e_kv_norm, e_w_uq, e_w_ukv, e_w_out, o_w_in, o_conv_w, o_conv_b, o_gate_a_w,
           o_gate_a_b, o_gate_x_w, o_gate_x_b, o_a_param, o_w_out, xa_wq, xa_wkv, xa_wo, ffn_w_in,
           ffn_w_out):
    batch, t, d = x.shape
    n = batch * t
    depth = norm_gains.shape[0]
    assert t % (GDN_PIPE * SUPER) == 0 and t % MLA_TQ == 0 and MLA_TQ % CHUNK == 0
    assert all(t % tile == 0 for tile in ROW_TILE.values())
    assert d % (LRU_BLOCKS * LANES) == 0 and d % XA_HEADS == 0
    assert (GDN_DK, MLA_NOPE, 2 * MLA_ROPE) == (LANES,) * 3
    xf = x.reshape(n, d)
    gains = norm_gains.reshape(depth, -1, 1, d)

    cos_t, sin_t = _rope_tables(positions)
    kv_mem = _mem_kv(mem.reshape(-1, d), mem_norm, xa_wkv)
    ffn_w_b = None

    for layer in range(depth):
        g = gains[layer]
        mix = None
        if layer % 2 == 0:
            e = layer // 2
            w_in_re, wuq_re, wukv_re, alog_row, dtb_row = _relayout_even(
                e_w_in[e], e_w_uq[e], e_w_ukv[e], e_a_log[e], e_dt_bias[e])
            q, k, v, z, gb, qf, kf, vb = _even_inproj(
                xf, g[0], w_in_re, e_conv_w[e], alog_row, dtb_row,
                e_q_norm[e].reshape(1, -1), e_kv_norm[e].reshape(1, -1), wuq_re, wukv_re,
                cos_t, sin_t, batch, tm=ROW_TILE["even_inproj"])
            out_a = _gdn(q, k, v, z, gb, e_o_norm[e].reshape(1, -1), batch)
            if ffn_w_b is None:
                out_b, ffn_w_b = _mla(qf, kf, vb, batch, cast_weights=(ffn_w_in, ffn_w_out))
            else:
                out_b, _ = _mla(qf, kf, vb, batch)
            mix = (out_a, out_b, e_w_out, e, g[1])
        else:
            o = layer // 2
            xf = _odd_mixer(xf, g[0], g[1], o_w_in, o_conv_w[o],
                            o_conv_b[o].reshape(1, d), o_gate_a_w,
                            o_gate_a_b[o].reshape(1, d), o_gate_x_w,
                            o_gate_x_b[o].reshape(1, d), o_a_param[o].reshape(1, d),
                            o_w_out, o, batch, tm=ROW_TILE["odd_mixer"])
        xf = _xattn(xf, g[2], g[3], xa_wq, kv_mem, xa_wo, layer, batch, tm=ROW_TILE["xattn"],
                    mix=mix)
        xf = _ffn(xf, g[4], g[5], ffn_w_b[0], ffn_w_b[1], layer, tm=ROW_TILE["ffn"])
    return xf.reshape(batch, t, d)
```

```python
import functools
import math

import jax
import jax.numpy as jnp
from jax import lax
from jax.experimental import pallas as pl
from jax.experimental.pallas import tpu as pltpu

F32 = jnp.float32
BF16 = jnp.bfloat16

EPS = 1e-6
CHUNK = 64
GDN_HEADS = 4
GDN_DK = 128
MLA_HEADS = 4
MLA_NOPE = 128
MLA_ROPE = 64
MLA_SCALE = (MLA_NOPE + MLA_ROPE) ** -0.5
LOG2E = math.log2(math.e)
ROPE_BASE = 10000.0
LRU_BLOCKS = 4
LRU_C = 8.0
XA_HEADS = 4
CONV_W = 4

LANES = 128
CARRY_ROWS = 8
V7X_VMEM_LIMIT = 56 * 1024 * 1024

ROW_TILE = {"even_inproj": 512, "odd_mixer": 512, "xattn": 1024, "ffn": 1024}

NT_DIMS = (((1,), (1,)), ((), ()))
TN_DIMS = (((0,), (0,)), ((), ()))


def _cparams(n_axes):
    return pltpu.CompilerParams(
        dimension_semantics=("arbitrary",) * n_axes,
        vmem_limit_bytes=V7X_VMEM_LIMIT)


def _dot(a, b):
    return jnp.dot(a, b, preferred_element_type=F32)


def _dot_nt(a, b):
    return lax.dot_general(a, b, NT_DIMS, preferred_element_type=F32)


def _rms(xf, g):
    ms = jnp.mean(xf * xf, axis=-1, keepdims=True)
    return xf * lax.rsqrt(ms + EPS) * g


def _sigmoid(x):
    return 1.0 / (1.0 + jnp.exp(-x))


def _softplus(x):
    return jnp.maximum(x, 0.0) + jnp.log1p(jnp.exp(-jnp.abs(x)))


def _const_spec(shape):
    nd = len(shape)
    return pl.BlockSpec(shape, lambda *_: (0,) * nd)


def _layer_spec(shape, layer):
    nd = len(shape) - 1
    return pl.BlockSpec((None,) + tuple(shape[1:]), lambda *_: (layer,) + (0,) * nd,
                        pipeline_mode=pl.Buffered(1))


def _rope_kernel(pos_ref, inv_ref, cos_ref, sin_ref):
    half = MLA_ROPE // 2
    per_row = LANES // half
    tr = pos_ref.shape[0]
    ang = pos_ref[...].astype(F32) * inv_ref[...]
    group = lax.broadcasted_iota(jnp.int32, ang.shape, 1) // half
    for tbl, ref in ((jnp.cos(ang), cos_ref), (jnp.sin(ang), sin_ref)):
        rot = [tbl] + [pltpu.roll(tbl, half * k, axis=1) for k in range(1, per_row)]
        for j in range(per_row):
            rep = rot[(per_row - j) % per_row]
            for g in range(1, per_row):
                rep = jnp.where(group == g, rot[(g - j) % per_row], rep)
            ref[pl.ds(j, tr, stride=per_row), :] = rep


def _rope_tables(positions):
    n = positions.size
    half = MLA_ROPE // 2
    per_row = LANES // half
    inv_freq = ROPE_BASE ** (-jnp.arange(0, MLA_ROPE, 2, dtype=F32) / MLA_ROPE)
    inv_row = jnp.tile(inv_freq, per_row).reshape(1, LANES)
    pos_rep = jnp.repeat(positions.reshape(n // per_row, per_row), half, axis=1)
    rows = n // per_row
    tr = min(rows, 512)
    out_spec = pl.BlockSpec((tr * per_row, LANES), lambda i: (i, 0))
    return pl.pallas_call(
        _rope_kernel,
        grid=(rows // tr,),
        in_specs=[pl.BlockSpec((tr, LANES), lambda i: (i, 0)), _const_spec((1, LANES))],
        out_specs=[out_spec, out_spec],
        out_shape=[jax.ShapeDtypeStruct((n, LANES), F32)] * 2,
        compiler_params=_cparams(1),
        name="rope_tables",
    )(pos_rep, inv_row)


def _memkv_kernel(mem_ref, g_ref, w_ref, o_ref, wb_scr):
    @pl.when(pl.program_id(1) == 0)
    def _():
        wb_scr[...] = w_ref[...].astype(BF16)

    mn = _rms(mem_ref[...], g_ref[...]).astype(BF16)
    o_ref[...] = _dot(mn, wb_scr[...]).astype(BF16)


def _mem_kv(mem2d, mem_norm, wkv):
    depth, d, d2 = wkv.shape
    rows = mem2d.shape[0]
    tr = min(rows, 512)
    return pl.pallas_call(
        _memkv_kernel,
        grid=(depth, rows // tr),
        in_specs=[pl.BlockSpec((tr, d), lambda l, i: (i, 0)),
                  _const_spec((1, d)),
                  pl.BlockSpec((None, d, d2), lambda l, i: (l, 0, 0))],
        out_specs=pl.BlockSpec((None, tr, d2), lambda l, i: (l, i, 0)),
        out_shape=jax.ShapeDtypeStruct((depth, rows, d2), BF16),
        scratch_shapes=[pltpu.VMEM((d, d2), BF16)],
        compiler_params=_cparams(2),
        name="mem_kv",
    )(mem2d, mem_norm.reshape(1, d), wkv)


def _rope_tile(x, cos, sin, lane):
    half = MLA_ROPE // 2
    up = pltpu.roll(x, LANES - half, axis=1)
    dn = pltpu.roll(x, half, axis=1)
    rot = jnp.where(lane < half, -up, dn)
    return jnp.where(lane < MLA_ROPE, x * cos + rot * sin, 0.0)


ROW_PHASES = 8


def _causal_conv_phases(x, xslab, sl, cw_ref, ls, bias=None):
    assert cw_ref.shape[0] == CONV_W and CONV_W - 1 <= CARRY_ROWS
    tm = x.shape[0]
    grp = tm // ROW_PHASES
    xslab[sl, CARRY_ROWS:CARRY_ROWS + tm, :] = x
    ph = [xslab[sl, pl.ds(CARRY_ROWS + j, grp, stride=ROW_PHASES), :]
          for j in range(-(CONV_W - 1), ROW_PHASES)]
    xslab[sl, 0:CARRY_ROWS, :] = x[tm - CARRY_ROWS:, :]
    w = [cw_ref[i:i + 1, ls] for i in range(CONV_W)]
    out = []
    for j in range(ROW_PHASES):
        acc = w[CONV_W - 1] * ph[j + CONV_W - 1]
        if bias is not None:
            acc = acc + bias
        for i in range(CONV_W - 1):
            acc = acc + w[i] * ph[j + i]
        out.append(acc)
    return out


def _even_in_kernel(x_ref, g_ref, win_ref, wtail_ref, cw_ref, alog_ref, dtb_ref, qn_ref, kvn_ref,
                    wuq_ref, wukv_ref, cos_ref, sin_ref,
                    q_out, k_out, v_out, z_out, gb_out, qf_out, kf_out, vb_out,
                    xslab, yslab):
    tm = x_ref.shape[0]
    qk_w = GDN_HEADS * GDN_DK
    qkv_w = 3 * qk_w

    @pl.when(pl.program_id(1) == 0)
    def _():
        xslab[:, 0:CARRY_ROWS, :] = jnp.zeros((qkv_w // LANES, CARRY_ROWS, LANES), F32)

    h = _rms(x_ref[...], g_ref[...]).astype(BF16)
    lane = lax.broadcasted_iota(jnp.int32, (tm, LANES), 1)
    cos = cos_ref[...]
    sin = sin_ref[...]
    nope_w = MLA_HEADS * MLA_NOPE
    z_off = qkv_w
    cq_off = z_off + qk_w
    ckv_off = cq_off + qn_ref.shape[1]
    misc_off = ckv_off + kvn_ref.shape[1]
    gw = 2 * GDN_DK
    st = {}

    def proj_cols(lo, width):
        if lo < cq_off:
            return _dot(h, win_ref[:, lo:lo + width])
        return _dot(h, wtail_ref[:, lo - cq_off:lo - cq_off + width])

    def mm_cq():
        st["cq"] = proj_cols(cq_off, qn_ref.shape[1])

    def ep_cq():
        st["cqb"] = _rms(st.pop("cq"), qn_ref[...] * (MLA_SCALE * LOG2E)).astype(BF16)

    def mm_ckv():
        st["ckv"] = proj_cols(ckv_off, kvn_ref.shape[1])

    def ep_ckv():
        st["ckvb"] = _rms(st.pop("ckv"), kvn_ref[...]).astype(BF16)

    def mm_misc():
        st["misc"] = proj_cols(misc_off, LANES)

    def ep_misc():
        misc = st.pop("misc")
        gdec = -jnp.exp(alog_ref[...]) * _softplus(misc + dtb_ref[...])
        beta = _sigmoid(misc)
        gb_out[...] = jnp.where((lane >= G_LANE) & (lane < G_LANE + GDN_HEADS), gdec,
                                jnp.where((lane >= B_LANE) & (lane < B_LANE + GDN_HEADS),
                                          beta, 0.0))
        st["kpe"] = _rope_tile(misc, cos, sin, lane).astype(BF16)

    def mm_uq():
        st["qf"] = _dot(st.pop("cqb"), wuq_ref[...])

    def ep_uq():
        qf = st.pop("qf")
        for hh in range(MLA_HEADS):
            base = 2 * LANES * hh
            qf_out[:, base:base + LANES] = qf[:, hh * LANES:(hh + 1) * LANES].astype(BF16)
            pe = qf[:, nope_w + hh * LANES: nope_w + (hh + 1) * LANES]
            qf_out[:, base + LANES:base + 2 * LANES] = _rope_tile(pe, cos, sin, lane).astype(BF16)

    def mm_ukv():
        st["kv"] = _dot(st.pop("ckvb"), wukv_ref[...])

    def ep_ukv():
        kv = st.pop("kv")
        for hh in range(MLA_HEADS):
            base = 2 * LANES * hh
            kf_out[:, base:base + LANES] = kv[:, hh * LANES:(hh + 1) * LANES].astype(BF16)
            kf_out[:, base + LANES:base + 2 * LANES] = st["kpe"]
        vb_out[...] = kv[:, nope_w:].astype(BF16)

    def mm_group(lo):
        def run():
            st[lo] = proj_cols(lo, gw)
        return run

    def ep_qkv(lo):
        def run():
            raw = st.pop(lo)
            which, off = divmod(lo, qk_w)
            out, scale = ((q_out, GDN_DK ** -0.5), (k_out, 1.0), (v_out, None))[which]
            grp = tm // ROW_PHASES
            for sub in range(gw // LANES):
                sl = lo // LANES + sub
                ls = slice(lo + sub * LANES, lo + (sub + 1) * LANES)
                blocks = _causal_conv_phases(raw[:, sub * LANES:(sub + 1) * LANES], xslab, sl,
                                             cw_ref, ls)
                for j, acc in enumerate(blocks):
                    a = acc * _sigmoid(acc)
                    if scale is not None:
                        a = a * (lax.rsqrt(jnp.sum(a * a, axis=-1, keepdims=True) + EPS) * scale)
                    yslab[sl, pl.ds(j, grp, stride=ROW_PHASES), :] = a
                out[:, off + sub * LANES: off + (sub + 1) * LANES] = yslab[sl].astype(BF16)
        return run

    def ep_z(lo):
        def run():
            z_out[:, lo - z_off: lo - z_off + gw] = st.pop(lo).astype(BF16)
        return run

    pairs = [(mm_group(lo), ep_qkv(lo)) for lo in range(0, qkv_w, gw)]
    pairs += [(mm_cq, ep_cq), (mm_ckv, ep_ckv), (mm_misc, ep_misc), (mm_uq, ep_uq), (mm_ukv, ep_ukv)]
    pairs += [(mm_group(lo), ep_z(lo)) for lo in range(z_off, cq_off, gw)]
    for mm, ep in pairs:
        mm()
        ep()


def _even_inproj(xf, g0, w_in_re, conv_w, alog_row, dtb_row, q_norm, kv_norm, wuq_re, wukv_re,
                 cos_t, sin_t, batch, tm):
    n, d = xf.shape
    t = n // batch
    nt = t // tm
    w_in, w_tail = w_in_re
    qkv_w = conv_w.shape[1]
    qk_w = qkv_w // 3
    row = lambda w: pl.BlockSpec((tm, w), lambda b, i: (b * nt + i, 0))
    outs = [(qk_w, BF16)] * 4 + [(LANES, F32), (2 * LANES * MLA_HEADS, BF16),
                                 (2 * LANES * MLA_HEADS, BF16), (LANES * MLA_HEADS, BF16)]
    return pl.pallas_call(
        _even_in_kernel,
        grid=(batch, nt),
        in_specs=[row(d), _const_spec((1, d)), _const_spec(w_in.shape), _const_spec(w_tail.shape),
                  _const_spec((CONV_W, qkv_w)),
                  _const_spec((1, LANES)), _const_spec((1, LANES)),
                  _const_spec(q_norm.shape), _const_spec(kv_norm.shape),
                  _const_spec(wuq_re.shape), _const_spec(wukv_re.shape),
                  row(LANES), row(LANES)],
        out_specs=[row(w) for w, _ in outs],
        out_shape=[jax.ShapeDtypeStruct((n, w), dt) for w, dt in outs],
        scratch_shapes=[pltpu.VMEM((qkv_w // LANES, tm + CARRY_ROWS, LANES), F32),
                        pltpu.VMEM((qkv_w // LANES, tm, LANES), F32)],
        compiler_params=_cparams(2),
        name="even_inproj",
    )(xf, g0, w_in, w_tail, conv_w, alog_row, dtb_row, q_norm, kv_norm, wuq_re, wukv_re, cos_t,
      sin_t)


SUPER = 2 * CHUNK
G_LANE = MLA_ROPE
B_LANE = MLA_ROPE + GDN_HEADS
GDN_PIPE = 2


def _gdn_kernel(q_ref, k_ref, v_ref, z_ref, gb_ref, on_ref, out_ref,
                gc_scr, gct_scr, gl_scr, u_scr, w_scr, a_scr, qe_scr, kd_scr, s_scr):
    t = q_ref.shape[0]
    n_super = t // SUPER
    n_chunk = t // CHUNK

    gb = gb_ref[...]
    rowi = lax.broadcasted_iota(jnp.int32, gb.shape, 0) & (CHUNK - 1)
    gc = gb
    s = 1
    while s < CHUNK:
        gc = jnp.where(rowi >= s, gc + pltpu.roll(gc, s, axis=0), gc)
        s *= 2
    gc_scr[...] = gc
    gct_scr[...] = gc.T
    g3 = gc.reshape(n_chunk, CHUNK, LANES)
    gl_scr[...] = jnp.broadcast_to(g3[:, CHUNK - 1:CHUNK, :], g3.shape).reshape(t, LANES)

    ri = lax.broadcasted_iota(jnp.int32, (SUPER, SUPER), 0)
    ci = lax.broadcasted_iota(jnp.int32, (SUPER, SUPER), 1)
    same = (ri >= CHUNK) == (ci >= CHUNK)
    causal = same & (ri >= ci)
    strict = same & (ri > ci)
    eye = (ri == ci).astype(F32)

    on = on_ref[...]
    heads = range(GDN_HEADS)
    hsl = [slice(hh * GDN_DK, (hh + 1) * GDN_DK) for hh in heads]
    lane_g = [slice(G_LANE + hh, G_LANE + hh + 1) for hh in heads]
    lane_b = [slice(B_LANE + hh, B_LANE + hh + 1) for hh in heads]

    def solve_stages(sc0):
        probs = []
        for sub in range(GDN_PIPE):
            rows = pl.ds(pl.multiple_of((sc0 + sub) * SUPER, SUPER), SUPER)
            probs += [(rows, hh) for hh in heads]
        kc = [k_ref[rows, hsl[hh]] for rows, hh in probs]
        qc = [q_ref[rows, hsl[hh]] for rows, hh in probs]
        gcol = [gc_scr[rows, lane_g[hh]] for rows, hh in probs]
        bcol = [gb_ref[rows, lane_b[hh]] for rows, hh in probs]
        grow = [gct_scr[lane_g[hh], rows] for rows, hh in probs]
        kk = [_dot_nt(k, k) for k in kc]
        qk = [_dot_nt(q, k) for q, k in zip(qc, kc)]
        yield
        decay = [jnp.where(causal, jnp.exp(jnp.where(causal, gc - gr, 0.0)), 0.0)
                 for gc, gr in zip(gcol, grow)]
        m = [jnp.where(strict, -(x * b * dc), 0.0) for x, b, dc in zip(kk, bcol, decay)]
        qacc = [eye + x for x in m]
        mb = [x.astype(BF16) for x in m]
        mj = [_dot(x, x) for x in mb]
        yield
        lvl = 2
        while lvl < CHUNK // 2:
            mb = [x.astype(BF16) for x in mj]
            r = [_dot(x, jnp.concatenate([x, qa.astype(BF16)], axis=1)) for x, qa in zip(mb, qacc)]
            yield
            mj = [x[:, :SUPER] for x in r]
            qacc = [qa + x[:, SUPER:] for qa, x in zip(qacc, r)]
            lvl *= 2
        last = [_dot(x.astype(BF16), qa.astype(BF16)) for qa, x in zip(qacc, mj)]
        yield
        qacc = [qa + x for qa, x in zip(qacc, last)]
        eg = [jnp.exp(gc) for gc in gcol]
        sol = []
        for i, (rows, hh) in enumerate(probs):
            vc = v_ref[rows, hsl[hh]].astype(F32)
            kf = kc[i].astype(F32)
            rhs = jnp.concatenate([vc * bcol[i], kf * (bcol[i] * eg[i])], axis=1).astype(BF16)
            sol.append(_dot(qacc[i].astype(BF16), rhs))
        yield
        for i, (rows, hh) in enumerate(probs):
            u_scr[rows, hsl[hh]] = sol[i][:, :GDN_DK]
            w_scr[rows, hsl[hh]] = sol[i][:, GDN_DK:].astype(BF16)
            aqk = qk[i] * decay[i]
            a_sh = pltpu.roll(aqk, CHUNK, axis=1)
            a_scr[rows, hsl[hh]] = jnp.where(ri < CHUNK, aqk, a_sh).astype(BF16)
            qe_scr[rows, hsl[hh]] = (qc[i].astype(F32) * eg[i]).astype(BF16)
            glast = gl_scr[rows, lane_g[hh]]
            kd_scr[rows, hsl[hh]] = (kc[i].astype(F32) * jnp.exp(glast - gcol[i])).astype(BF16)

    def scan_stages(sc0, st):
        for half in range(GDN_PIPE * SUPER // CHUNK):
            r0 = pl.multiple_of(sc0 * SUPER + half * CHUNK, CHUNK)
            rows = pl.ds(r0, CHUNK)
            sb = [x.astype(BF16) for x in st]
            t1 = [_dot(w_scr[rows, hsl[hh]], sb[hh]) for hh in heads]
            oq = [_dot(qe_scr[rows, hsl[hh]], sb[hh]) for hh in heads]
            yield
            vb = [(u_scr[rows, hsl[hh]] - t1[hh]).astype(BF16) for hh in heads]
            oa = [_dot(a_scr[rows, hsl[hh]][:, :CHUNK], vb[hh]) for hh in heads]
            kv = [lax.dot_general(kd_scr[rows, hsl[hh]], vb[hh], TN_DIMS,
                                  preferred_element_type=F32) for hh in heads]
            yield
            for hh in heads:
                gam = jnp.exp(gl_scr[pl.ds(r0, 1), lane_g[hh]])
                st[hh] = st[hh] * gam + kv[hh]
                z = z_ref[rows, hsl[hh]].astype(F32)
                out_ref[rows, hsl[hh]] = (
                    _rms(oq[hh] + oa[hh], on) * (z * _sigmoid(z))).astype(BF16)

    _round_robin([solve_stages(0)])
    s_scr[...] = jnp.zeros(s_scr.shape, F32)
    n_steps = n_super // GDN_PIPE

    def pipe_body(it, _):
        st = [s_scr[hh] for hh in heads]
        _round_robin([solve_stages((it + 1) * GDN_PIPE), scan_stages(it * GDN_PIPE, st)])
        for hh in heads:
            s_scr[hh] = st[hh]
        return 0

    lax.fori_loop(0, n_steps - 1, pipe_body, 0)
    st = [s_scr[hh] for hh in heads]
    _round_robin([scan_stages((n_steps - 1) * GDN_PIPE, st)])


def _gdn(q, k, v, z, gb, o_norm, batch):
    n, w = q.shape
    t = n // batch
    row = lambda ww: pl.BlockSpec((t, ww), lambda b: (b, 0))
    return pl.pallas_call(
        _gdn_kernel,
        grid=(batch,),
        in_specs=[row(w), row(w), row(w), row(w), row(LANES), _const_spec((1, GDN_DK))],
        out_specs=row(w),
        out_shape=jax.ShapeDtypeStruct((n, w), BF16),
        scratch_shapes=[pltpu.VMEM((t, LANES), F32),
                        pltpu.VMEM((LANES, t), F32),
                        pltpu.VMEM((t, LANES), F32),
                        pltpu.VMEM((t, w), F32),
                        pltpu.VMEM((t, w), BF16),
                        pltpu.VMEM((t, w), BF16),
                        pltpu.VMEM((t, w), BF16),
                        pltpu.VMEM((t, w), BF16),
                        pltpu.VMEM((GDN_HEADS, GDN_DK, GDN_DK), F32)],
        compiler_params=_cparams(1),
        name="gdn",
    )(q, k, v, z, gb, o_norm)


MLA_TQ = 256


def _mla_kernel(q_ref, k_ref, v_ref, *rest):
    n_cast = (len(rest) - 1) // 2
    o_ref = rest[n_cast]
    for src, dst in zip(rest[:n_cast], rest[n_cast + 1:]):
        dst[...] = src[...].astype(BF16)
    t = q_ref.shape[0]
    tq = MLA_TQ
    nq = t // tq
    hw = 2 * LANES
    heads = range(q_ref.shape[1] // hw)
    ri = lax.broadcasted_iota(jnp.int32, (tq, tq), 0) // CHUNK
    ci = lax.broadcasted_iota(jnp.int32, (tq, tq), 1) // CHUNK
    diag_mask = ci <= ri

    groups = [(i, h) for i in range(nq) for h in heads]

    def scores(i, h):
        kw = (i + 1) * tq
        return _dot_nt(q_ref[i * tq:(i + 1) * tq, h * hw:(h + 1) * hw],
                       k_ref[0:kw, h * hw:(h + 1) * hw])

    s_next = scores(*groups[0])
    for g, (i, h) in enumerate(groups):
        s = s_next
        if g + 1 < len(groups):
            s_next = scores(*groups[g + 1])
        kw = (i + 1) * tq
        s_diag = jnp.where(diag_mask, s[:, kw - tq:], -jnp.inf)
        s = s_diag if i == 0 else jnp.concatenate([s[:, :kw - tq], s_diag], axis=1)
        m = jnp.max(s, axis=-1, keepdims=True)
        p = jnp.exp2(s - m)
        l = jnp.sum(p, axis=-1, keepdims=True)
        pv = _dot(p.astype(BF16), v_ref[0:kw, h * LANES:(h + 1) * LANES])
        o_ref[i * tq:(i + 1) * tq, h * LANES:(h + 1) * LANES] = (pv / l).astype(BF16)


MLA_HEADS_PER_STEP = 2


def _mla(qf, kf, vb, batch, cast_weights=()):
    n = qf.shape[0]
    t = n // batch
    hps = MLA_HEADS_PER_STEP
    hg = MLA_HEADS // hps
    steps = batch * hg
    spec = lambda w: pl.BlockSpec((t, w * hps), lambda b, g: (b, g))
    cast_specs = [pl.BlockSpec((w.shape[0], w.shape[1] // steps, w.shape[2]),
                               lambda b, g: (0, b * hg + g, 0)) for w in cast_weights]
    outs = pl.pallas_call(
        _mla_kernel,
        grid=(batch, hg),
        in_specs=[spec(2 * LANES), spec(2 * LANES), spec(LANES)] + cast_specs,
        out_specs=[spec(LANES)] + cast_specs,
        out_shape=[jax.ShapeDtypeStruct((n, vb.shape[1]), BF16)]
                  + [jax.ShapeDtypeStruct(w.shape, BF16) for w in cast_weights],
        compiler_params=_cparams(2),
        name="mla_attn",
    )(qf, kf, vb, *cast_weights)
    return outs[0], outs[1:]


SCAN_PHASES = ROW_PHASES


def _gelu_tanh(x):
    c2 = 2.0 * math.sqrt(2.0 / math.pi) * LOG2E
    return x / (1.0 + jnp.exp2(x * (-c2 - (c2 * 0.044715) * (x * x))))


def _odd_kernel(x_ref, g0_ref, win_f32, cw_ref, cb_ref, gaw_f32, gab_ref, gxw_f32, gxb_ref,
                ap_ref, wout_f32, g1_ref, o_ref, xslab, hslab, hcar,
                win_ref, gaw_ref, gxw_ref, wout_ref):
    tm, d = x_ref.shape
    bw = d // LRU_BLOCKS
    n_slab = d // LANES
    assert cw_ref.shape[0] == CONV_W and CONV_W - 1 <= CARRY_ROWS
    _cast_once([(win_f32, win_ref), (gaw_f32, gaw_ref), (gxw_f32, gxw_ref), (wout_f32, wout_ref)])

    @pl.when(pl.program_id(1) == 0)
    def _():
        xslab[:, 0:CARRY_ROWS, :] = jnp.zeros((n_slab, CARRY_ROWS, LANES), F32)
        hcar[...] = jnp.zeros(hcar.shape, F32)

    x = x_ref[...]
    h = _rms(x, g0_ref[...]).astype(BF16)
    grp = tm // SCAN_PHASES
    slabs_per_block = bw // LANES
    rowg = lax.broadcasted_iota(jnp.int32, (grp, 1), 0)

    blocks = range(LRU_BLOCKS)
    csl = [slice(n * bw, (n + 1) * bw) for n in blocks]
    st = [dict() for _ in blocks]

    def projx(n):
        st[n]["xb"] = _dot(h, win_ref[:, csl[n]])

    def projy(n):
        st[n]["yb"] = _dot(h, win_ref[:, d + n * bw: d + (n + 1) * bw])

    def conv(n):
        xb = st[n].pop("xb")
        cols = []
        for k in range(slabs_per_block):
            sl = n * slabs_per_block + k
            ls = slice(n * bw + k * LANES, n * bw + (k + 1) * LANES)
            out = _causal_conv_phases(xb[:, k * LANES:(k + 1) * LANES], xslab, sl, cw_ref, ls,
                                      bias=cb_ref[:, ls])
            cols.append(jnp.concatenate(out, axis=0))
        st[n]["xc"] = jnp.concatenate(cols, axis=1)

    def gates(n):
        xcb = st[n]["xc"].astype(BF16)
        st[n]["ra"] = _dot(xcb, gaw_ref[n])
        st[n]["ia"] = _dot(xcb, gxw_ref[n])

    def recur(n):
        cs, xc = csl[n], st[n].pop("xc")
        r = _sigmoid(st[n].pop("ra") + gab_ref[:, cs])
        ig = _sigmoid(st[n].pop("ia") + gxb_ref[:, cs])
        a = jnp.exp2(r * ((-LRU_C * LOG2E) * _softplus(-ap_ref[:, cs])))
        om = 1.0 - a * a
        u = jnp.where(om > 0.0, om * lax.rsqrt(om), 0.0) * (ig * xc)

        ca, cu = a[0:grp], u[0:grp]
        la, lu = [ca], [cu]
        for j in range(1, SCAN_PHASES):
            aj, uj = a[j * grp:(j + 1) * grp], u[j * grp:(j + 1) * grp]
            cu = aj * cu + uj
            ca = aj * ca
            la.append(ca)
            lu.append(cu)
        ga, gu = ca, cu
        s = 1
        while s < grp:
            valid = rowg >= s
            gu = jnp.where(valid, ga * pltpu.roll(gu, s, axis=0) + gu, gu)
            ga = jnp.where(valid, ga * pltpu.roll(ga, s, axis=0), ga)
            s *= 2
        h_in = hcar[:, cs]
        h_end = ga * h_in + gu
        hcar[:, cs] = h_end[grp - 1:grp, :]
        h_prev = jnp.where(rowg == 0, h_in, pltpu.roll(h_end, 1, axis=0))
        slabs = range(n * slabs_per_block, (n + 1) * slabs_per_block)
        for j in range(SCAN_PHASES):
            hj = la[j] * h_prev + lu[j]
            for k, sl in enumerate(slabs):
                hslab[sl, pl.ds(j, grp, stride=SCAN_PHASES), :] = hj[:, k * LANES:(k + 1) * LANES]
        st[n]["hs"] = jnp.concatenate([hslab[sl] for sl in slabs], axis=1)

    def gate(n):
        st[n]["hg"] = (st[n].pop("hs") * _gelu_tanh(st[n].pop("yb"))).astype(BF16)

    def outp(n):
        st[n]["y"] = _dot(st[n].pop("hg"), wout_ref[csl[n], :])

    chain = (projx, conv, gates, recur, projy, gate, outp)
    for k in range(LRU_BLOCKS + len(chain) - 1):
        for s, stage in enumerate(chain):
            if 0 <= k - s < LRU_BLOCKS:
                stage(k - s)
    y = functools.reduce(lambda p, q: p + q, [st[n]["y"] for n in blocks])
    o_ref[...] = x + _rms(y, g1_ref[...])


def _odd_mixer(xf, g0, g1, w_in, conv_w, conv_b, gaw, gab, gxw, gxb, a_param, w_out, o, batch, tm):
    n, d = xf.shape
    t = n // batch
    nt = t // tm
    row = pl.BlockSpec((tm, d), lambda b, i: (b * nt + i, 0))
    vec = _const_spec((1, d))
    weights = (w_in, gaw, gxw, w_out)
    wspec = [_layer_spec(w.shape, o) for w in weights]
    return pl.pallas_call(
        _odd_kernel,
        grid=(batch, nt),
        in_specs=[row, vec, wspec[0], _const_spec(conv_w.shape), vec,
                  wspec[1], vec, wspec[2], vec, vec, wspec[3], vec],
        out_specs=row,
        out_shape=jax.ShapeDtypeStruct((n, d), F32),
        scratch_shapes=[pltpu.VMEM((d // LANES, tm + CARRY_ROWS, LANES), F32),
                        pltpu.VMEM((d // LANES, tm, LANES), F32),
                        pltpu.VMEM((1, d), F32)]
                       + [pltpu.VMEM(w.shape[1:], BF16) for w in weights],
        compiler_params=_cparams(2),
        name="odd_mixer",
    )(xf, g0, w_in, conv_w, conv_b, gaw, gab, gxw, gxb, a_param, w_out, g1)


XA_SPLIT = 4


def _round_robin(gens):
    live = list(gens)
    while live:
        live = [g for g in live if next(g, StopIteration) is not StopIteration]


def _cast_once(pairs, n_axes=2):
    first = pl.program_id(0) == 0
    for axis in range(1, n_axes):
        first = first & (pl.program_id(axis) == 0)

    @pl.when(first)
    def _():
        for src, dst in pairs:
            dst[...] = src[...].astype(BF16)


def _xattn_kernel(x_ref, g2_ref, wq_f32, kv_ref, wo_f32, g3_ref, o_ref, wq_ref, wo_ref):
    _cast_once([(wq_f32, wq_ref), (wo_f32, wo_ref)])
    tm = x_ref.shape[0]
    rs = tm // XA_SPLIT
    _round_robin([_xattn_stages(slice(k * rs, (k + 1) * rs), x_ref, None, g2_ref, wq_ref, kv_ref,
                                wo_ref, g3_ref, o_ref) for k in range(XA_SPLIT)])


def _outproj_xattn_kernel(x_ref, a_ref, b_ref, w_f32, g1_ref,
                          g2_ref, wq_f32, kv_ref, wo_f32, g3_ref, o_ref, wq_ref, wo_ref, w_ref):
    _cast_once([(wq_f32, wq_ref), (wo_f32, wo_ref), (w_f32, w_ref)])
    tm = x_ref.shape[0]
    rs = tm // XA_SPLIT
    _round_robin([_xattn_stages(slice(k * rs, (k + 1) * rs), x_ref, (a_ref, b_ref, w_ref, g1_ref),
                                g2_ref, wq_ref, kv_ref, wo_ref, g3_ref, o_ref)
                  for k in range(XA_SPLIT)])


def _xattn_stages(rows, x_ref, mix, g2_ref, wq_ref, kv_ref, wo_ref, g3_ref, o_ref):
    x = x_ref[rows, :]
    d = x.shape[1]
    hd = d // XA_HEADS
    if mix is not None:
        a_ref, b_ref, w_ref, g1_ref = mix
        ka = a_ref.shape[1]
        y = _dot(a_ref[rows, :], w_ref[0:ka, :]) + _dot(b_ref[rows, :], w_ref[ka:, :])
        yield
        x = x + _rms(y, g1_ref[...])
    h = (x * g2_ref[...]).astype(BF16)
    r = lax.rsqrt(jnp.mean(x * x, axis=-1, keepdims=True) + EPS)
    q = _dot(h, wq_ref[...])
    yield
    q = (q * (r * (hd ** -0.5 * LOG2E))).astype(BF16)
    s = [_dot_nt(q[:, hh * hd:(hh + 1) * hd], kv_ref[:, hh * hd:(hh + 1) * hd])
         for hh in range(XA_HEADS)]
    yield
    outs = []
    for hh in range(XA_HEADS):
        m = jnp.max(s[hh], axis=-1, keepdims=True)
        p = jnp.exp2(s[hh] - m)
        l = jnp.sum(p, axis=-1, keepdims=True)
        pv = _dot(p.astype(BF16), kv_ref[:, d + hh * hd: d + (hh + 1) * hd])
        outs.append((pv / l).astype(BF16))
    yield
    y = _dot(jnp.concatenate(outs, axis=1), wo_ref[...])
    yield
    o_ref[rows, :] = x + _rms(y, g3_ref[...])


def _xattn(xf, g2, g3, wq, kv_mem, wo, layer, batch, tm, mix=None):
    n, d = xf.shape
    t = n // batch
    nt = t // tm
    n_mem = kv_mem.shape[1] // batch
    row = lambda w: pl.BlockSpec((tm, w), lambda b, i: (b * nt + i, 0))
    vec = _const_spec((1, d))
    xa_specs = [vec, _layer_spec(wq.shape, layer),
                pl.BlockSpec((None, n_mem, 2 * d), lambda b, i: (layer, b, 0)),
                _layer_spec(wo.shape, layer), vec]
    xa_args = (g2, wq, kv_mem, wo, g3)
    scratch = [pltpu.VMEM(wq.shape[1:], BF16), pltpu.VMEM(wo.shape[1:], BF16)]
    if mix is None:
        body, specs, args, name = _xattn_kernel, [row(d)] + xa_specs, (xf,) + xa_args, "xattn"
    else:
        a, b, w_out, w_out_layer, g1 = mix
        body, name = _outproj_xattn_kernel, "outproj_xattn"
        specs = [row(d), row(a.shape[1]), row(b.shape[1]), _layer_spec(w_out.shape, w_out_layer),
                 vec] + xa_specs
        args = (xf, a, b, w_out, g1) + xa_args
        scratch.append(pltpu.VMEM(w_out.shape[1:], BF16))
    return pl.pallas_call(
        body,
        grid=(batch, nt),
        in_specs=specs,
        out_specs=row(d),
        out_shape=jax.ShapeDtypeStruct((n, d), F32),
        scratch_shapes=scratch,
        compiler_params=_cparams(2),
        name=name,
    )(*args)


FFN_TF = 256


def _ffn_kernel(x_ref, g4_ref, win_ref, wo_ref, g5_ref, o_ref, act_scr):
    dff = wo_ref.shape[0]
    x = x_ref[...]
    h = (x * g4_ref[...]).astype(BF16)
    r = lax.rsqrt(jnp.mean(x * x, axis=-1, keepdims=True) + EPS)
    for c in range(dff // FFN_TF):
        cols = slice(c * FFN_TF, (c + 1) * FFN_TF)
        gt = _dot(h, win_ref[:, cols]) * r
        up = _dot(h, win_ref[:, dff + c * FFN_TF: dff + (c + 1) * FFN_TF]) * r
        act_scr[:, cols] = (gt * _sigmoid(gt) * up).astype(BF16)
    half = x.shape[0] // 2
    for rows in (slice(0, half), slice(half, 2 * half)):
        y = _dot(act_scr[rows, :], wo_ref[...])
        o_ref[rows, :] = x[rows, :] + _rms(y, g5_ref[...])


def _ffn(xf, g4, g5, w_in, w_out, layer, tm):
    n, d = xf.shape
    dff = w_out.shape[1]
    row = pl.BlockSpec((tm, d), lambda i: (i, 0))
    vec = _const_spec((1, d))
    return pl.pallas_call(
        _ffn_kernel,
        grid=(n // tm,),
        in_specs=[row, vec, _layer_spec(w_in.shape, layer), _layer_spec(w_out.shape, layer), vec],
        out_specs=row,
        out_shape=jax.ShapeDtypeStruct((n, d), F32),
        scratch_shapes=[pltpu.VMEM((tm, dff), BF16)],
        compiler_params=_cparams(1),
        name="ffn",
    )(xf, g4, w_in, w_out, g5)


def _relayout_even(w_in, w_uq, w_ukv, a_log, dt_bias):
    d = w_in.shape[0]
    qkvz = 3 * GDN_HEADS * GDN_DK + GDN_HEADS * GDN_DK
    ab = 2 * GDN_HEADS
    lat = w_in.shape[1] - qkvz - ab - MLA_ROPE
    pad = LANES - MLA_ROPE - ab
    w_in = w_in.astype(BF16)
    w_tail = jnp.concatenate(
        [w_in[:, qkvz + ab:qkvz + ab + lat], w_in[:, qkvz + ab + lat:],
         w_in[:, qkvz:qkvz + ab], jnp.zeros((d, pad), w_in.dtype)], axis=1)
    w_in_re = (w_in, w_tail)
    r = w_uq.shape[0]
    uq = w_uq.reshape(r, MLA_HEADS, MLA_NOPE + MLA_ROPE)
    pe = jnp.pad(uq[:, :, MLA_NOPE:], ((0, 0), (0, 0), (0, LANES - MLA_ROPE)))
    wuq_re = jnp.concatenate([uq[:, :, :MLA_NOPE].reshape(r, -1), pe.reshape(r, -1)],
                             axis=1).astype(BF16)
    ukv = w_ukv.reshape(w_ukv.shape[0], MLA_HEADS, -1)
    wukv_re = jnp.concatenate([ukv[:, :, :MLA_NOPE].reshape(r, -1),
                               ukv[:, :, MLA_NOPE:].reshape(r, -1)], axis=1).astype(BF16)
    alog_row = jnp.zeros((1, LANES), F32).at[0, G_LANE:G_LANE + GDN_HEADS].set(a_log)
    dtb_row = jnp.zeros((1, LANES), F32).at[0, G_LANE:G_LANE + GDN_HEADS].set(dt_bias)
    return w_in_re, wuq_re, wukv_re, alog_row, dtb_row


def kernel(x, mem, positions, norm_gains, mem_norm, e_w_in, e_conv_w, e_a_log, e_dt_bias, e_o_norm,
           e_q_norm, e_kv_norm, e_w_uq, e_w_ukv, e_w_out, o_w_in, o_conv_w, o_conv_b, o_gate_a_w,
           o_gate_a_b, o_gate_x_w, o_gate_x_b, o_a_param, o_w_out, xa_wq, xa_wkv, xa_wo, ffn_w_in,
           ffn_w_out):
    batch, t, d = x.shape
    n = batch * t
    depth = norm_gains.shape[0]
    assert t % (GDN_PIPE * SUPER) == 0 and t % MLA_TQ == 0 and MLA_TQ % CHUNK == 0
    assert all(t % tile == 0 for tile in ROW_TILE.values())
    assert d % (LRU_BLOCKS * LANES) == 0 and d % XA_HEADS == 0
    assert (GDN_DK, MLA_NOPE, 2 * MLA_ROPE) == (LANES,) * 3
    xf = x.reshape(n, d)
    gains = norm_gains.reshape(depth, -1, 1, d)

    cos_t, sin_t = _rope_tables(positions)
    kv_mem = _mem_kv(mem.reshape(-1, d), mem_norm, xa_wkv)
    ffn_w_b = None

    for layer in range(depth):
        g = gains[layer]
        mix = None
        if layer % 2 == 0:
            e = layer // 2
            w_in_re, wuq_re, wukv_re, alog_row, dtb_row = _relayout_even(
                e_w_in[e], e_w_uq[e], e_w_ukv[e], e_a_log[e], e_dt_bias[e])
            q, k, v, z, gb, qf, kf, vb = _even_inproj(
                xf, g[0], w_in_re, e_conv_w[e], alog_row, dtb_row,
                e_q_norm[e].reshape(1, -1), e_kv_norm[e].reshape(1, -1), wuq_re, wukv_re,
                cos_t, sin_t, batch, tm=ROW_TILE["even_inproj"])
            out_a = _gdn(q, k, v, z, gb, e_o_norm[e].reshape(1, -1), batch)
            if ffn_w_b is None:
                out_b, ffn_w_b = _mla(qf, kf, vb, batch, cast_weights=(ffn_w_in, ffn_w_out))
            else:
                out_b, _ = _mla(qf, kf, vb, batch)
            mix = (out_a, out_b, e_w_out, e, g[1])
        else:
            o = layer // 2
            xf = _odd_mixer(xf, g[0], g[1], o_w_in, o_conv_w[o],
                            o_conv_b[o].reshape(1, d), o_gate_a_w,
                            o_gate_a_b[o].reshape(1, d), o_gate_x_w,
                            o_gate_x_b[o].reshape(1, d), o_a_param[o].reshape(1, d),
                            o_w_out, o, batch, tm=ROW_TILE["odd_mixer"])
        xf = _xattn(xf, g[2], g[3], xa_wq, kv_mem, xa_wo, layer, batch, tm=ROW_TILE["xattn"],
                    mix=mix)
        xf = _ffn(xf, g[4], g[5], ffn_w_b[0], ffn_w_b[1], layer, tm=ROW_TILE["ffn"])
    return xf.reshape(batch, t, d)
```
